```python
import math
import jax, jax.numpy as jnp
from jax import lax
import numpy as np

D_MODEL = 1024
BATCH = 4
SEQ = 8192
DEPTH = 1

CHUNK = 64
EPS = 1e-6
FOX_HEADS = 8
FOX_DH = 64
Q_BLOCK = 128
GDN_HEADS = 4
GDN_DK = 128
GDN_DV = 128
CONV_W = 4
PEER_HEADS = 8
N_KEYS = 128
N_EXPERTS = N_KEYS * N_KEYS
PEER_DQ = 256
PEER_TOPK = 16
PEER_BLOCK = 128

FOX_W = FOX_HEADS * FOX_DH
GDN_KW = GDN_HEADS * GDN_DK
GDN_VW = GDN_HEADS * GDN_DV
SPLITS = (FOX_W, FOX_W, FOX_W, FOX_HEADS, GDN_KW, GDN_KW, GDN_VW, GDN_VW, GDN_HEADS, GDN_HEADS, D_MODEL, D_MODEL)
IN_W = sum(SPLITS)

kernel_name = "hybrid_fox_gdn_peer_block"


def rms_norm(x, g):
    xf = x.astype(jnp.float32)
    y = xf * lax.rsqrt(jnp.mean(xf * xf, axis=-1, keepdims=True) + EPS)
    return (y * g.astype(jnp.float32)).astype(x.dtype)


def l2_norm(t):
    tf = t.astype(jnp.float32)
    return tf * lax.rsqrt(jnp.sum(tf * tf, axis=-1, keepdims=True) + EPS)


def forgetting_attention(q, k, v, log_f):
    B, S, H, dh = q.shape
    nb = S // Q_BLOCK
    cum = jnp.cumsum(log_f.astype(jnp.float32), axis=1).transpose(0, 2, 1)
    kh = k.transpose(0, 2, 1, 3)
    vh = v.transpose(0, 2, 1, 3)
    qb = q.reshape(B, nb, Q_BLOCK, H, dh).transpose(1, 0, 3, 2, 4)
    cq = cum.reshape(B, H, nb, Q_BLOCK).transpose(2, 0, 1, 3)
    pos_k = jnp.arange(S)
    scale = dh ** -0.5

    def block(args):
        i, qi, ci = args
        s = jnp.einsum('bhqd,bhkd->bhqk', qi, kh).astype(jnp.float32) * scale
        s = s + ci[..., :, None] - cum[..., None, :]
        pos_q = i * Q_BLOCK + jnp.arange(Q_BLOCK)
        mask = pos_k[None, :] <= pos_q[:, None]
        s = jnp.where(mask, s, -jnp.inf)
        p = jax.nn.softmax(s, axis=-1).astype(vh.dtype)
        return jnp.einsum('bhqk,bhkd->bhqd', p, vh)

    o = lax.map(block, (jnp.arange(nb), qb, cq))
    return o.transpose(1, 0, 3, 2, 4).reshape(B, S, H * dh)


def causal_short_conv(x, w):
    S = x.shape[1]
    xp = jnp.pad(x, ((0, 0), (CONV_W - 1, 0), (0, 0)))
    y = sum(w[i] * xp[:, i:i + S] for i in range(CONV_W))
    return jax.nn.silu(y)


def gated_delta_rule(q, k, v, g, beta):
    B, S, H, dk = q.shape
    dv = v.shape[-1]
    nc = S // CHUNK
    f32 = jnp.float32

    def chunked(t):
        t = t.reshape((B, nc, CHUNK, H) + t.shape[3:])
        return jnp.moveaxis(t, 3, 1)

    q = chunked(q.astype(f32)) * (dk ** -0.5)
    k = chunked(k.astype(f32))
    v = chunked(v.astype(f32))
    g = jnp.cumsum(chunked(g.astype(f32)), axis=-1)
    beta = chunked(beta.astype(f32))
    idx = jnp.arange(CHUNK)
    causal = idx[:, None] >= idx[None, :]
    strict = idx[:, None] > idx[None, :]
    diff = g[..., :, None] - g[..., None, :]
    decay = jnp.where(causal, jnp.exp(jnp.where(causal, diff, 0.0)), 0.0)
    kb = k * beta[..., None]
    L = jnp.where(strict, jnp.einsum('bhnid,bhnjd->bhnij', kb, k) * decay, 0.0)
    A = L + jnp.eye(CHUNK, dtype=f32)
    rhs = jnp.concatenate([v * beta[..., None], kb * jnp.exp(g)[..., None]], axis=-1)
    sol = lax.linalg.triangular_solve(A, rhs, left_side=True, lower=True, unit_diagonal=True)
    u, w = sol[..., :dv], sol[..., dv:]
    attn_intra = jnp.einsum('bhnid,bhnjd->bhnij', q, k) * decay
    q_dec = q * jnp.exp(g)[..., None]
    g_last = g[..., -1]
    k_dec = k * jnp.exp(g_last[..., None] - g)[..., None]
    xs = tuple(jnp.moveaxis(t, 2, 0) for t in (u, w, attn_intra, q_dec, k_dec, g_last))

    def step(state, inp):
        u_c, w_c, a_c, qd_c, kd_c, gl_c = inp
        v_new = u_c - jnp.einsum('bhcd,bhde->bhce', w_c, state)
        o_c = jnp.einsum('bhcd,bhde->bhce', qd_c, state) + jnp.einsum('bhij,bhje->bhie', a_c, v_new)
        state = state * jnp.exp(gl_c)[..., None, None] + jnp.einsum('bhcd,bhce->bhde', kd_c, v_new)
        return state, o_c

    s0 = jnp.zeros((B, H, dk, dv), f32)
    _, o = lax.scan(step, s0, xs)
    o = jnp.moveaxis(o, 0, 2)
    return jnp.moveaxis(o, 1, 3).reshape(B, S, H, dv)


def peer(h, w_query, keys1, keys2, expert_down, expert_up):
    B, S, D = h.shape
    nb = (B * S) // PEER_BLOCK
    xt = h.reshape(nb, PEER_BLOCK, D)

    def block(xb):
        qy = (xb @ w_query).reshape(PEER_BLOCK, PEER_HEADS, 2, PEER_DQ // 2)
        s1 = jnp.einsum('thd,kd->thk', qy[:, :, 0], keys1).astype(jnp.float32)
        s2 = jnp.einsum('thd,kd->thk', qy[:, :, 1], keys2).astype(jnp.float32)
        v1, i1 = lax.top_k(s1, PEER_TOPK)
        v2, i2 = lax.top_k(s2, PEER_TOPK)
        cand = (v1[..., :, None] + v2[..., None, :]).reshape(PEER_BLOCK, PEER_HEADS, PEER_TOPK * PEER_TOPK)
        cidx = (i1[..., :, None] * N_KEYS + i2[..., None, :]).reshape(PEER_BLOCK, PEER_HEADS, PEER_TOPK * PEER_TOPK)
        top_s, pos = lax.top_k(cand, PEER_TOPK)
        eidx = jnp.take_along_axis(cidx, pos, axis=-1)
        gates = jax.nn.softmax(top_s, axis=-1)
        u = expert_down[eidx]
        act = jax.nn.gelu(jnp.einsum('td,thkd->thk', xb, u).astype(jnp.float32), approximate=False)
        ve = expert_up[eidx]
        return jnp.einsum('thk,thkd->td', (gates * act).astype(xb.dtype), ve)

    return lax.map(block, xt).reshape(B, S, D)


def setup_inputs(seed: int = 0) -> dict:
    key = jax.random.key(seed)
    ks = jax.random.split(key, 24)
    f32 = jnp.float32
    D = D_MODEL

    def nrm(k, shape, scale):
        return jax.random.normal(k, shape, f32) * scale

    def gain(k, shape):
        return 1.0 + 0.01 * jax.random.normal(k, shape, f32)

    dt = jnp.exp(jax.random.uniform(ks[9], (DEPTH, GDN_HEADS), f32, math.log(1e-3), math.log(0.1)))
    return {
        "x": nrm(ks[0], (BATCH, SEQ, D), 1.0),
        "c": nrm(ks[1], (BATCH, D), 1.0),
        "w_ada": nrm(ks[2], (DEPTH, D, 6 * D), 0.5 * D ** -0.5),
        "b_ada": nrm(ks[3], (DEPTH, 6 * D), 0.02),
        "g_pre_mix": gain(ks[4], (DEPTH, D)),
        "g_post_mix": gain(ks[5], (DEPTH, D)),
        "w_in": nrm(ks[6], (DEPTH, D, IN_W), D ** -0.5),
        "b_fgate": 3.0 + 0.1 * jax.random.normal(ks[7], (DEPTH, FOX_HEADS), f32),
        "fox_w_o": nrm(ks[8], (DEPTH, FOX_W, D), FOX_W ** -0.5),
        "conv_w": nrm(ks[10], (DEPTH, CONV_W, 2 * GDN_KW + GDN_VW), CONV_W ** -0.5),
        "a_log": jnp.log(jax.random.uniform(ks[11], (DEPTH, GDN_HEADS), f32, 1.0, 16.0)),
        "dt_bias": dt + jnp.log(-jnp.expm1(-dt)),
        "gdn_norm_w": gain(ks[12], (DEPTH, GDN_DV)),
        "gdn_w_o": nrm(ks[13], (DEPTH, GDN_VW, D), GDN_VW ** -0.5),
        "b_branch_gate": nrm(ks[14], (DEPTH, 2, D), 0.02),
        "w_out": nrm(ks[15], (DEPTH, D, D), D ** -0.5),
        "g_pre_ffn": gain(ks[16], (DEPTH, D)),
        "g_post_ffn": gain(ks[17], (DEPTH, D)),
        "peer_w_query": nrm(ks[18], (DEPTH, D, PEER_HEADS * PEER_DQ), D ** -0.5),
        "peer_keys1": nrm(ks[19], (DEPTH, N_KEYS, PEER_DQ // 2), (PEER_DQ // 2) ** -0.5),
        "peer_keys2": nrm(ks[20], (DEPTH, N_KEYS, PEER_DQ // 2), (PEER_DQ // 2) ** -0.5),
        "peer_u": nrm(ks[21], (DEPTH, N_EXPERTS, D), D ** -0.5),
        "peer_v": nrm(ks[22], (DEPTH, N_EXPERTS, D), D ** -0.5),
    }


def reference(x, c, w_ada, b_ada, g_pre_mix, g_post_mix, w_in, b_fgate, fox_w_o, conv_w, a_log,
              dt_bias, gdn_norm_w, gdn_w_o, b_branch_gate, w_out, g_pre_ffn, g_post_ffn,
              peer_w_query, peer_keys1, peer_keys2, peer_u, peer_v):
    B, S, D = x.shape
    split_idx = np.cumsum(SPLITS)[:-1].tolist()
    for l in range(DEPTH):
        mod = jax.nn.silu(c) @ w_ada[l] + b_ada[l]
        sh1, sc1, gt1, sh2, sc2, gt2 = [m[:, None, :] for m in jnp.split(mod, 6, axis=-1)]

        h = rms_norm(x, g_pre_mix[l]) * (1.0 + sc1) + sh1
        proj = h @ w_in[l]
        fq, fk, fv, f_logit, gq, gk, gv, gz, b_logit, a_logit, ga, gb = jnp.split(proj, split_idx, axis=-1)

        log_f = jax.nn.log_sigmoid((f_logit + b_fgate[l]).astype(jnp.float32))
        ya = forgetting_attention(fq.reshape(B, S, FOX_HEADS, FOX_DH),
                                  fk.reshape(B, S, FOX_HEADS, FOX_DH),
                                  fv.reshape(B, S, FOX_HEADS, FOX_DH), log_f)
        ya = ya @ fox_w_o[l]

        qkv = causal_short_conv(jnp.concatenate([gq, gk, gv], axis=-1), conv_w[l])
        cq, ck, cv = jnp.split(qkv, [GDN_KW, 2 * GDN_KW], axis=-1)
        q_b = l2_norm(cq.reshape(B, S, GDN_HEADS, GDN_DK))
        k_b = l2_norm(ck.reshape(B, S, GDN_HEADS, GDN_DK))
        v_b = cv.reshape(B, S, GDN_HEADS, GDN_DV)
        beta = jax.nn.sigmoid(b_logit.astype(jnp.float32))
        g_dec = -jnp.exp(a_log[l].astype(jnp.float32)) * jax.nn.softplus((a_logit + dt_bias[l]).astype(jnp.float32))
        ob = gated_delta_rule(q_b, k_b, v_b, g_dec, beta)
        ob = rms_norm(ob, gdn_norm_w[l]) * jax.nn.silu(gz.reshape(B, S, GDN_HEADS, GDN_DV).astype(jnp.float32))
        yb = ob.reshape(B, S, GDN_VW).astype(x.dtype) @ gdn_w_o[l]

        gate_a = jax.nn.sigmoid(ga + b_branch_gate[l, 0])
        gate_b = jax.nn.sigmoid(gb + b_branch_gate[l, 1])
        mix = (gate_a * ya + gate_b * yb) @ w_out[l]
        x = x + gt1 * rms_norm(mix, g_post_mix[l])

        h2 = rms_norm(x, g_pre_ffn[l]) * (1.0 + sc2) + sh2
        y2 = peer(h2, peer_w_query[l], peer_keys1[l], peer_keys2[l], peer_u[l], peer_v[l])
        x = x + gt2 * rms_norm(y2, g_post_ffn[l])
    return x
```

```python
import functools
import math

import jax
import jax.numpy as jnp
from jax import lax
from jax.experimental import pallas as pl
from jax.experimental.pallas import tpu as pltpu

F32 = jnp.float32
BF16 = jnp.bfloat16
HIGHEST = lax.Precision.HIGHEST

D_MODEL = 1024
EPS = 1e-6
FOX_HEADS = 8
FOX_DH = 64
GDN_HEADS = 4
GDN_DK = 128
GDN_DV = 128
CONV_W = 4
PEER_HEADS = 8
N_KEYS = 128
PEER_DQ = 256
PEER_TOPK = 16
FOX_W = FOX_HEADS * FOX_DH
GDN_KW = GDN_HEADS * GDN_DK
GDN_VW = GDN_HEADS * GDN_DV
SPLITS = (FOX_W, FOX_W, FOX_W, FOX_HEADS, GDN_KW, GDN_KW, GDN_VW, GDN_VW, GDN_HEADS, GDN_HEADS,
          D_MODEL, D_MODEL)

LANES = 128
GDN_CHUNK = 128
NEG_BIG = -1e30
VMEM_LIMIT = 56 * 1024 * 1024


def _params(sem):
    return pltpu.CompilerParams(dimension_semantics=sem, vmem_limit_bytes=VMEM_LIMIT)


def _dot(a, b, precision=None):
    return jnp.dot(a, b, preferred_element_type=F32, precision=precision)


def _dot_nt(a, b, precision=None):
    return lax.dot_general(a, b, (((1,), (1,)), ((), ())), preferred_element_type=F32,
                           precision=precision)


def _sigmoid(x):
    return 1.0 / (1.0 + jnp.exp(-x))


def _silu(x):
    return x * _sigmoid(x)


def _softplus(x):
    return jnp.maximum(x, 0.0) + jnp.log(1.0 + jnp.exp(-jnp.abs(x)))


def _log_sigmoid(x):
    return -_softplus(-x)


def _rms(x, g):
    return x * lax.rsqrt(jnp.mean(x * x, axis=-1, keepdims=True) + EPS) * g


def _ada_kernel(c_ref, w_ref, b_ref, o_ref):
    c = c_ref[...]
    o_ref[...] = _dot(_silu(c), w_ref[...], HIGHEST) + b_ref[...]


def _ada(c_pad, w, b):
    n = w.shape[1]
    bn = 1024
    return pl.pallas_call(
        _ada_kernel,
        grid=(n // bn,),
        in_specs=[pl.BlockSpec(c_pad.shape, lambda j: (0, 0)),
                  pl.BlockSpec((D_MODEL, bn), lambda j: (0, j)),
                  pl.BlockSpec((1, bn), lambda j: (0, j))],
        out_specs=pl.BlockSpec((c_pad.shape[0], bn), lambda j: (0, j)),
        out_shape=jax.ShapeDtypeStruct((c_pad.shape[0], n), F32),
        compiler_params=_params(("arbitrary",)),
        name="ada",
    )(c_pad, w, b)


def _inproj_kernel(x_ref, sc_ref, sh_ref, g_ref, wf_ref, wg_ref, wgate_ref, ws_ref, bgate_ref,
                   bsm_ref, alog_ref, fqkv_ref, gqkv_ref, gz_ref, gates_ref, small_ref):
    x = x_ref[...]
    h = _rms(x, g_ref[...]) * (1.0 + sc_ref[...]) + sh_ref[...]
    hb = h.astype(BF16)
    pf = _dot(hb, wf_ref[...])
    lane = lax.broadcasted_iota(jnp.int32, (1, 3 * FOX_W), 1)
    pf = jnp.where(lane < FOX_W, pf * (FOX_DH ** -0.5), pf)
    fqkv_ref[...] = pf.astype(BF16)
    pg = _dot(hb, wg_ref[...])
    gqkv_ref[...] = pg[:, :3 * GDN_KW]
    gz_ref[...] = pg[:, 3 * GDN_KW:]
    gates_ref[...] = _sigmoid(_dot(hb, wgate_ref[...]) + bgate_ref[...])
    z = _dot(hb, ws_ref[...]) + bsm_ref[...]
    sl = lax.broadcasted_iota(jnp.int32, (1, LANES), 1)
    neg_a = -jnp.exp(alog_ref[...])
    small_ref[...] = jnp.where(sl < FOX_HEADS, _log_sigmoid(z),
                               jnp.where(sl < FOX_HEADS + GDN_HEADS, _sigmoid(z), neg_a * _softplus(z)))


def _inproj(x, sc1, sh1, g_pre, wf, wg, wgate, ws, bgate, bsm, alog_row, tm):
    B, S, D = x.shape
    full = lambda a: pl.BlockSpec(a.shape, lambda b, i: (0,) * a.ndim)
    mod = pl.BlockSpec((None, 1, D), lambda b, i: (b, 0, 0))
    tile = lambda n: pl.BlockSpec((None, tm, n), lambda b, i: (b, i, 0))
    return pl.pallas_call(
        _inproj_kernel,
        grid=(B, S // tm),
        in_specs=[tile(D), mod, mod, full(g_pre), full(wf), full(wg), full(wgate), full(ws),
                  full(bgate), full(bsm), full(alog_row)],
        out_specs=[tile(3 * FOX_W), tile(3 * GDN_KW), tile(GDN_VW), tile(2 * D), tile(LANES)],
        out_shape=[jax.ShapeDtypeStruct((B, S, 3 * FOX_W), BF16),
                   jax.ShapeDtypeStruct((B, S, 3 * GDN_KW), F32),
                   jax.ShapeDtypeStruct((B, S, GDN_VW), F32),
                   jax.ShapeDtypeStruct((B, S, 2 * D), F32),
                   jax.ShapeDtypeStruct((B, S, LANES), F32)],
        compiler_params=_params(("arbitrary", "arbitrary")),
        name="inproj",
    )(x, sc1, sh1, g_pre, wf, wg, wgate, ws, bgate, bsm, alog_row)


def _cumsum_kernel(s_ref, o_ref, carry_ref):
    @pl.when(pl.program_id(1) == 0)
    def _():
        carry_ref[...] = jnp.zeros_like(carry_ref)

    v = s_ref[...]
    n = v.shape[0]
    r = lax.broadcasted_iota(jnp.int32, (n, n), 0)
    c = lax.broadcasted_iota(jnp.int32, (n, n), 1)
    tri = jnp.where(r >= c, 1.0, 0.0).astype(F32)
    cs = _dot(tri, v, HIGHEST) + carry_ref[...]
    o_ref[...] = cs
    carry_ref[...] = cs[n - 1:n, :]


def _cumsum(small, tc):
    B, S, L = small.shape
    return pl.pallas_call(
        _cumsum_kernel,
        grid=(B, S // tc),
        in_specs=[pl.BlockSpec((None, tc, L), lambda b, i: (b, i, 0))],
        out_specs=pl.BlockSpec((None, tc, L), lambda b, i: (b, i, 0)),
        out_shape=jax.ShapeDtypeStruct((B, S, L), F32),
        scratch_shapes=[pltpu.VMEM((1, L), F32)],
        compiler_params=_params(("arbitrary", "arbitrary")),
        name="cumsum",
    )(small)


def _fox_kernel(q_ref, k_ref, v_ref, ccol_ref, crow_ref, o_ref, m_ref, l_ref, acc_ref, *, tq):
    hp = pl.program_id(1)
    i = pl.program_id(2)
    q2 = q_ref[...]
    lane = lax.broadcasted_iota(jnp.int32, (1, LANES), 1)
    ccol = ccol_ref[...]
    rr = lax.broadcasted_iota(jnp.int32, (tq, tq), 0)
    cc = lax.broadcasted_iota(jnp.int32, (tq, tq), 1)
    outs = []
    for hh in range(2):
        h = hp * 2 + hh
        in_head = (lane >= hh * FOX_DH) & (lane < (hh + 1) * FOX_DH)
        qh = jnp.where(in_head, q2, jnp.zeros_like(q2))
        cq = jnp.sum(jnp.where(lane == h, ccol, 0.0), axis=1, keepdims=True)
        m_ref[...] = jnp.full(m_ref.shape, NEG_BIG, F32)
        l_ref[...] = jnp.zeros(l_ref.shape, F32)
        acc_ref[...] = jnp.zeros(acc_ref.shape, F32)

        def step(j, masked):
            start = pl.multiple_of(j * tq, tq)
            kt = k_ref[pl.ds(start, tq), :]
            vt = v_ref[pl.ds(start, tq), :]
            ck = crow_ref[pl.ds(h, 1), pl.ds(start, tq)]
            s = _dot_nt(qh, kt) + (cq - ck)
            if masked:
                s = jnp.where(rr >= cc, s, NEG_BIG)
            m_prev = m_ref[...]
            m_new = jnp.maximum(m_prev, jnp.max(s, axis=1, keepdims=True))
            alpha = jnp.exp(m_prev - m_new)
            p = jnp.exp(s - m_new)
            l_ref[...] = alpha * l_ref[...] + jnp.sum(p, axis=1, keepdims=True)
            acc_ref[...] = alpha * acc_ref[...] + _dot(p.astype(BF16), vt)
            m_ref[...] = m_new

        def body(j, carry):
            step(j, False)
            return carry

        lax.fori_loop(0, i, body, 0)
        step(i, True)
        outs.append(acc_ref[...] / l_ref[...])
    o_ref[...] = jnp.where(lane < FOX_DH, outs[0], outs[1]).astype(o_ref.dtype)


def _fox(fqkv, cum, cum_t, tq):
    B, S, _ = fqkv.shape
    nhp = FOX_HEADS // 2
    return pl.pallas_call(
        functools.partial(_fox_kernel, tq=tq),
        grid=(B, nhp, S // tq),
        in_specs=[pl.BlockSpec((None, tq, LANES), lambda b, hp, i: (b, i, hp)),
                  pl.BlockSpec((None, S, LANES), lambda b, hp, i: (b, 0, nhp + hp)),
                  pl.BlockSpec((None, S, LANES), lambda b, hp, i: (b, 0, 2 * nhp + hp)),
                  pl.BlockSpec((None, tq, LANES), lambda b, hp, i: (b, i, 0)),
                  pl.BlockSpec((None, FOX_HEADS, S), lambda b, hp, i: (b, 0, 0))],
        out_specs=pl.BlockSpec((None, tq, LANES), lambda b, hp, i: (b, i, hp)),
        out_shape=jax.ShapeDtypeStruct((B, S, FOX_W), BF16),
        scratch_shapes=[pltpu.VMEM((tq, 1), F32), pltpu.VMEM((tq, 1), F32),
                        pltpu.VMEM((tq, LANES), F32)],
        compiler_params=_params(("arbitrary", "arbitrary", "arbitrary")),
        name="fox",
    )(fqkv, fqkv, fqkv, cum, cum_t)


def _unit_lower_inverse(low):
    n = low.shape[0]
    r = lax.broadcasted_iota(jnp.int32, (n, n), 0)
    c = lax.broadcasted_iota(jnp.int32, (n, n), 1)
    x = jnp.where(r == c, 1.0, 0.0).astype(F32)
    s = 1
    while s < n:
        sel = ((r // (2 * s)) == (c // (2 * s))) & (((r // s) % 2) == 1) & (((c // s) % 2) == 0)
        cs = jnp.where(sel, low, 0.0)
        x = x - _dot(_dot(x, cs, HIGHEST), x, HIGHEST)
        s *= 2
    return x


def _gdn_kernel(x_ref, gz_ref, sm_ref, cw_ref, nw_ref, o_ref, state_ref, carry_ref):
    ci = pl.program_id(0)
    b = pl.program_id(1)
    C = GDN_CHUNK

    @pl.when(ci == 0)
    def _():
        state_ref[b] = jnp.zeros(state_ref.shape[1:], F32)
        carry_ref[b] = jnp.zeros(carry_ref.shape[1:], F32)

    x = x_ref[...]
    prev = carry_ref[b]
    row8 = lax.broadcasted_iota(jnp.int32, (8, 1), 0)
    y = cw_ref[CONV_W - 1:CONV_W, :] * x
    for j in range(1, CONV_W):
        xr = pltpu.roll(x, j, 0)
        pr = pltpu.roll(prev, j, 0)
        first = jnp.where(row8 < j, pr, xr[0:8, :])
        y = y + cw_ref[CONV_W - 1 - j:CONV_W - j, :] * jnp.concatenate([first, xr[8:, :]], axis=0)
    carry_ref[b] = x[C - 8:C, :]
    y = _silu(y)

    sm = sm_ref[...]
    r = lax.broadcasted_iota(jnp.int32, (C, C), 0)
    c = lax.broadcasted_iota(jnp.int32, (C, C), 1)
    causal = r >= c
    strict = r > c
    tri = jnp.where(causal, 1.0, 0.0).astype(F32)
    gcum = _dot(tri, sm, HIGHEST)
    gcum_t = gcum.T
    nw = nw_ref[...]
    gz = gz_ref[...]

    for h in range(GDN_HEADS):
        q = y[:, h * GDN_DK:(h + 1) * GDN_DK]
        k = y[:, GDN_KW + h * GDN_DK:GDN_KW + (h + 1) * GDN_DK]
        v = y[:, 2 * GDN_KW + h * GDN_DV:2 * GDN_KW + (h + 1) * GDN_DV]
        q = q * lax.rsqrt(jnp.sum(q * q, axis=1, keepdims=True) + EPS) * (GDN_DK ** -0.5)
        k = k * lax.rsqrt(jnp.sum(k * k, axis=1, keepdims=True) + EPS)
        beta = sm[:, FOX_HEADS + h:FOX_HEADS + h + 1]
        gl = FOX_HEADS + GDN_HEADS + h
        g_col = gcum[:, gl:gl + 1]
        g_row = gcum_t[gl:gl + 1, :]
        g_last = gcum[C - 1:C, gl:gl + 1]
        decay = jnp.where(causal, jnp.exp(jnp.where(causal, g_col - g_row, 0.0)), 0.0)
        kb = k * beta
        low = jnp.where(strict, _dot_nt(kb, k, HIGHEST) * decay, 0.0)
        tinv = _unit_lower_inverse(low)
        eg = jnp.exp(g_col)
        u = _dot(tinv, v * beta, HIGHEST)
        w = _dot(tinv, kb * eg, HIGHEST)
        attn = _dot_nt(q, k, HIGHEST) * decay
        q_dec = q * eg
        k_dec = k * jnp.exp(g_last - g_col)
        st = state_ref[b, h]
        v_new = u - _dot(w, st, HIGHEST)
        o = _dot(q_dec, st, HIGHEST) + _dot(attn, v_new, HIGHEST)
        state_ref[b, h] = st * jnp.exp(g_last) + _dot(k_dec.T, v_new, HIGHEST)
        o = _rms(o, nw) * _silu(gz[:, h * GDN_DV:(h + 1) * GDN_DV])
        o_ref[:, h * GDN_DV:(h + 1) * GDN_DV] = o.astype(o_ref.dtype)


def _gdn(gqkv, gz, small, conv_w, norm_w):
    B, S, _ = gqkv.shape
    C = GDN_CHUNK
    tile = lambda n: pl.BlockSpec((None, C, n), lambda ci, b: (b, ci, 0))
    full = lambda a: pl.BlockSpec(a.shape, lambda ci, b: (0,) * a.ndim)
    return pl.pallas_call(
        _gdn_kernel,
        grid=(S // C, B),
        in_specs=[tile(3 * GDN_KW), tile(GDN_VW), tile(LANES), full(conv_w), full(norm_w)],
        out_specs=tile(GDN_VW),
        out_shape=jax.ShapeDtypeStruct((B, S, GDN_VW), BF16),
        scratch_shapes=[pltpu.VMEM((B, GDN_HEADS, GDN_DK, GDN_DV), F32),
                        pltpu.VMEM((B, 8, 3 * GDN_KW), F32)],
        compiler_params=_params(("arbitrary", "arbitrary")),
        name="gdn",
    )(gqkv, gz, small, conv_w, norm_w)


def _outproj_kernel(ya_ref, yb_ref, gates_ref, x_ref, gt1_ref, sc2_ref, sh2_ref, gpm_ref, gpf_ref,
                    wa_ref, wb_ref, wo_ref, wq_ref, x1_ref, h2_ref, qy_ref):
    ya = _dot(ya_ref[...], wa_ref[...])
    yb = _dot(yb_ref[...], wb_ref[...])
    gates = gates_ref[...]
    mix_in = gates[:, :D_MODEL] * ya + gates[:, D_MODEL:] * yb
    mix = _dot(mix_in.astype(BF16), wo_ref[...])
    x1 = x_ref[...] + gt1_ref[...] * _rms(mix, gpm_ref[...])
    x1_ref[...] = x1
    h2 = (_rms(x1, gpf_ref[...]) * (1.0 + sc2_ref[...]) + sh2_ref[...]).astype(BF16)
    h2_ref[...] = h2
    qy_ref[...] = _dot(h2, wq_ref[...])


def _outproj(ya, yb, gates, x, gt1, sc2, sh2, gpm, gpf, wa, wb, wo, wq, tm):
    B, S, D = x.shape
    full = lambda a: pl.BlockSpec(a.shape, lambda b, i: (0,) * a.ndim)
    mod = pl.BlockSpec((None, 1, D), lambda b, i: (b, 0, 0))
    tile = lambda n: pl.BlockSpec((None, tm, n), lambda b, i: (b, i, 0))
    nq = wq.shape[1]
    return pl.pallas_call(
        _outproj_kernel,
        grid=(B, S // tm),
        in_specs=[tile(FOX_W), tile(GDN_VW), tile(2 * D), tile(D), mod, mod, mod, full(gpm), full(gpf),
                  full(wa), full(wb), full(wo), full(wq)],
        out_specs=[tile(D), tile(D), tile(nq)],
        out_shape=[jax.ShapeDtypeStruct((B, S, D), F32),
                   jax.ShapeDtypeStruct((B, S, D), BF16),
                   jax.ShapeDtypeStruct((B, S, nq), F32)],
        compiler_params=_params(("arbitrary", "arbitrary")),
        name="outproj",
    )(ya, yb, gates, x, gt1, sc2, sh2, gpm, gpf, wa, wb, wo, wq)


def _topk_kernel(qy_ref, k1_ref, k2_ref, s1_ref, s2_ref, tau_ref, m1_ref, m2_ref, zinv_ref,
                 d1_ref, c1_ref, d2_ref, c2_ref, cand_ref, mult_ref):
    K = PEER_TOPK
    half = PEER_DQ // 2
    neg_inf = -jnp.inf

    def extract(v, d_ref, c_ref):
        def it(t, v):
            m = jnp.max(v, axis=0, keepdims=True)
            eq = v == m
            d_ref[pl.ds(t, 1), :] = m
            c_ref[pl.ds(t, 1), :] = jnp.sum(jnp.where(eq, 1.0, 0.0), axis=0, keepdims=True)
            return jnp.where(eq, neg_inf, v)
        lax.fori_loop(0, K, it, v)

    def head(h, carry):
        col = pl.multiple_of(h * PEER_DQ, PEER_DQ)
        q1 = qy_ref[:, pl.ds(col, half)]
        q2 = qy_ref[:, pl.ds(col + half, half)]
        s1 = _dot_nt(k1_ref[...], q1, HIGHEST)
        s2 = _dot_nt(k2_ref[...], q2, HIGHEST)
        s1_ref[h] = s1
        s2_ref[h] = s2
        extract(s1, d1_ref, c1_ref)
        extract(s2, d2_ref, c2_ref)
        d2 = d2_ref[...]
        c2 = c2_ref[...]
        for a in range(K):
            cand_ref[a * K:(a + 1) * K, :] = d1_ref[a:a + 1, :] + d2
            mult_ref[a * K:(a + 1) * K, :] = c1_ref[a:a + 1, :] * c2
        m1 = d1_ref[0:1, :]
        m2 = d2_ref[0:1, :]
        top = m1 + m2
        mult = mult_ref[...]

        def it(t, st):
            cand, cum, zacc, tau = st
            m = jnp.max(cand, axis=0, keepdims=True)
            eq = cand == m
            w = jnp.sum(jnp.where(eq, mult, 0.0), axis=0, keepdims=True)
            take = jnp.minimum(jnp.maximum(K - cum, 0.0), w)
            zacc = zacc + take * jnp.exp(m - top)
            tau = jnp.where(cum < K, m, tau)
            return jnp.where(eq, neg_inf, cand), cum + w, zacc, tau

        zero = jnp.zeros_like(m1)
        _, _, zacc, tau = lax.fori_loop(0, K, it, (cand_ref[...], zero, zero, top))
        tau_ref[pl.ds(h, 1), :] = tau
        m1_ref[pl.ds(h, 1), :] = m1
        m2_ref[pl.ds(h, 1), :] = m2
        zinv_ref[pl.ds(h, 1), :] = 1.0 / zacc
        return carry

    lax.fori_loop(0, PEER_HEADS, head, 0)


def _topk(qy, keys1, keys2, tt):
    T, nq = qy.shape
    H, K = PEER_HEADS, PEER_TOPK
    full = lambda a: pl.BlockSpec(a.shape, lambda i: (0,) * a.ndim)
    big = pl.BlockSpec((H, N_KEYS, tt), lambda i: (0, 0, i))
    stat = pl.BlockSpec((H, tt), lambda i: (0, i))
    return pl.pallas_call(
        _topk_kernel,
        grid=(T // tt,),
        in_specs=[pl.BlockSpec((tt, nq), lambda i: (i, 0)), full(keys1), full(keys2)],
        out_specs=[big, big, stat, stat, stat, stat],
        out_shape=[jax.ShapeDtypeStruct((H, N_KEYS, T), F32)] * 2 + [jax.ShapeDtypeStruct((H, T), F32)] * 4,
        scratch_shapes=[pltpu.VMEM((K, tt), F32)] * 4 + [pltpu.VMEM((K * K, tt), F32)] * 2,
        compiler_params=_params(("arbitrary",)),
        name="topk",
    )(qy, keys1, keys2)


def _peer_kernel(h2_ref, u_ref, vt_ref, s1_ref, s2_ref, tau_ref, m1_ref, m2_ref, zinv_ref, x1_ref,
                 gt2_ref, g_ref, o_ref, acc_ref, e2_ref, wt_ref, *, na):
    j = pl.program_id(1)
    H = PEER_HEADS

    @pl.when(j == 0)
    def _():
        acc_ref[...] = jnp.zeros(acc_ref.shape, F32)
        for h in range(H):
            e2_ref[h] = jnp.exp(s2_ref[h] - m2_ref[h:h + 1, :])

    st = _dot_nt(u_ref[...], h2_ref[...])
    act = 0.5 * st * (1.0 + lax.erf(st * (0.5 ** 0.5)))
    for al in range(na):
        wa = jnp.zeros((N_KEYS, st.shape[1]), F32)
        for h in range(H):
            s1row = s1_ref[h, al:al + 1, :]
            e1row = jnp.exp(s1row - m1_ref[h:h + 1, :]) * zinv_ref[h:h + 1, :]
            score = s1row + s2_ref[h]
            wa = wa + jnp.where(score >= tau_ref[h:h + 1, :], e1row * e2_ref[h], 0.0)
        wt_ref[al * N_KEYS:(al + 1) * N_KEYS, :] = (wa * act[al * N_KEYS:(al + 1) * N_KEYS, :]).astype(BF16)
    acc_ref[...] += _dot(vt_ref[...], wt_ref[...])

    @pl.when(j == pl.num_programs(1) - 1)
    def _():
        y = acc_ref[...].T
        o_ref[...] = x1_ref[...] + gt2_ref[...] * _rms(y, g_ref[...])


def _peer(h2, u_b, vt_b, s1t, s2t, tau, m1, m2, zinv, x1, gt2, g_post, tt, ec, tiles_per_batch):
    T, D = h2.shape
    E = u_b.shape[0]
    H = PEER_HEADS
    na = ec // N_KEYS
    stat = pl.BlockSpec((H, tt), lambda i, j: (0, i))
    return pl.pallas_call(
        functools.partial(_peer_kernel, na=na),
        grid=(T // tt, E // ec),
        in_specs=[pl.BlockSpec((tt, D), lambda i, j: (i, 0)),
                  pl.BlockSpec((ec, D), lambda i, j: (j, 0)),
                  pl.BlockSpec((D, ec), lambda i, j: (0, j)),
                  pl.BlockSpec((H, na, tt), lambda i, j: (0, j, i)),
                  pl.BlockSpec((H, N_KEYS, tt), lambda i, j: (0, 0, i)),
                  stat, stat, stat, stat,
                  pl.BlockSpec((tt, D), lambda i, j: (i, 0)),
                  pl.BlockSpec((None, 1, D), lambda i, j: (i // tiles_per_batch, 0, 0)),
                  pl.BlockSpec(g_post.shape, lambda i, j: (0, 0))],
        out_specs=pl.BlockSpec((tt, D), lambda i, j: (i, 0)),
        out_shape=jax.ShapeDtypeStruct((T, D), F32),
        scratch_shapes=[pltpu.VMEM((D, tt), F32), pltpu.VMEM((H, N_KEYS, tt), F32),
                        pltpu.VMEM((ec, tt), BF16)],
        compiler_params=_params(("arbitrary", "arbitrary")),
        name="peer",
    )(h2, u_b, vt_b, s1t, s2t, tau, m1, m2, zinv, x1, gt2, g_post)


def _pick(n, pref):
    t = pref
    while n % t:
        t //= 2
    return t


def _layer(x, c_pad, w_ada, b_ada, g_pre_mix, g_post_mix, w_in, b_fgate, fox_w_o, conv_w, a_log, dt_bias,
           gdn_norm_w, gdn_w_o, b_branch_gate, w_out, g_pre_ffn, g_post_ffn, peer_w_query, peer_keys1,
           peer_keys2, peer_u, peer_v):
    B, S, D = x.shape
    T = B * S
    mod = _ada(c_pad, w_ada, b_ada.reshape(1, -1))[:B]
    sh1, sc1, gt1, sh2, sc2, gt2 = [m.reshape(B, 1, D) for m in jnp.split(mod, 6, axis=-1)]

    off = [0]
    for s in SPLITS:
        off.append(off[-1] + s)
    col = lambda i, j: w_in[:, off[i]:off[j]]
    wf = col(0, 3).astype(BF16)
    wg = jnp.concatenate([col(4, 8)], axis=1).astype(BF16)
    wgate = col(10, 12).astype(BF16)
    pad = LANES - (FOX_HEADS + 2 * GDN_HEADS)
    ws = jnp.concatenate([col(3, 4), col(8, 10), jnp.zeros((D, pad), F32)], axis=1).astype(BF16)
    zpad = jnp.zeros((pad,), F32)
    bsm = jnp.concatenate([b_fgate, jnp.zeros((GDN_HEADS,), F32), dt_bias, zpad]).reshape(1, LANES)
    alog_row = jnp.concatenate([jnp.zeros((FOX_HEADS + GDN_HEADS,), F32), a_log, zpad]).reshape(1, LANES)
    bgate = b_branch_gate.reshape(1, 2 * D)

    tm = _pick(S, 256)
    fqkv, gqkv, gz, gates, small = _inproj(x, sc1, sh1, g_pre_mix.reshape(1, D), wf, wg, wgate, ws, bgate,
                                           bsm, alog_row, tm)
    cum = _cumsum(small, _pick(S, 512))
    cum_t = jnp.swapaxes(cum[:, :, :FOX_HEADS], 1, 2)
    ya = _fox(fqkv, cum, cum_t, _pick(S, 512))
    yb = _gdn(gqkv, gz, small, conv_w, gdn_norm_w.reshape(1, GDN_DV))
    x1, h2, qy = _outproj(ya, yb, gates, x, gt1, sc2, sh2, g_post_mix.reshape(1, D), g_pre_ffn.reshape(1, D),
                          fox_w_o.astype(BF16), gdn_w_o.astype(BF16), w_out.astype(BF16),
                          peer_w_query.astype(BF16), tm)
    qy = qy.reshape(T, -1)
    s1t, s2t, tau, m1, m2, zinv = _topk(qy, peer_keys1, peer_keys2, _pick(T, 256))
    tt = _pick(S, 512)
    out = _peer(h2.reshape(T, D), peer_u.astype(BF16), peer_v.T.astype(BF16), s1t, s2t, tau, m1, m2, zinv,
                x1.reshape(T, D), gt2, g_post_ffn.reshape(1, D), tt, 1024, S // tt)
    return out.reshape(B, S, D)


def kernel(x, c, w_ada, b_ada, g_pre_mix, g_post_mix, w_in, b_fgate, fox_w_o, conv_w, a_log, dt_bias,
           gdn_norm_w, gdn_w_o, b_branch_gate, w_out, g_pre_ffn, g_post_ffn, peer_w_query, peer_keys1,
           peer_keys2, peer_u, peer_v):
    B = x.shape[0]
    c_pad = jnp.pad(c, ((0, (-B) % 8), (0, 0)))
    for l in range(w_ada.shape[0]):
        x = _layer(x, c_pad, w_ada[l], b_ada[l], g_pre_mix[l], g_post_mix[l], w_in[l], b_fgate[l],
                   fox_w_o[l], conv_w[l], a_log[l], dt_bias[l], gdn_norm_w[l], gdn_w_o[l], b_branch_gate[l],
                   w_out[l], g_pre_ffn[l], g_post_ffn[l], peer_w_query[l], peer_keys1[l], peer_keys2[l],
                   peer_u[l], peer_v[l])
    return x
```

```python
import functools
import math

import jax
import jax.numpy as jnp
from jax import lax
from jax.experimental import pallas as pl
from jax.experimental.pallas import tpu as pltpu

F32 = jnp.float32
BF16 = jnp.bfloat16
HIGHEST = lax.Precision.HIGHEST

D_MODEL = 1024
EPS = 1e-6
FOX_HEADS = 8
FOX_DH = 64
GDN_HEADS = 4
GDN_DK = 128
GDN_DV = 128
CONV_W = 4
PEER_HEADS = 8
N_KEYS = 128
PEER_DQ = 256
PEER_TOPK = 16
FOX_W = FOX_HEADS * FOX_DH
GDN_KW = GDN_HEADS * GDN_DK
GDN_VW = GDN_HEADS * GDN_DV
SPLITS = (FOX_W, FOX_W, FOX_W, FOX_HEADS, GDN_KW, GDN_KW, GDN_VW, GDN_VW, GDN_HEADS, GDN_HEADS,
          D_MODEL, D_MODEL)

LANES = 128
GDN_CHUNK = 128
NEG_BIG = -1e30
LOG2E = 1.4426950408889634
VMEM_LIMIT = 56 * 1024 * 1024


def _params(sem):
    return pltpu.CompilerParams(dimension_semantics=sem, vmem_limit_bytes=VMEM_LIMIT)


def _dot(a, b, precision=None):
    return jnp.dot(a, b, preferred_element_type=F32, precision=precision)


def _dot_nt(a, b, precision=None):
    return lax.dot_general(a, b, (((1,), (1,)), ((), ())), preferred_element_type=F32,
                           precision=precision)


def _sigmoid(x):
    return 1.0 / (1.0 + jnp.exp(-x))


def _silu(x):
    return x * _sigmoid(x)


def _softplus(x):
    return jnp.maximum(x, 0.0) + jnp.log(1.0 + jnp.exp(-jnp.abs(x)))


def _log_sigmoid(x):
    return -_softplus(-x)


def _rms(x, g):
    return x * lax.rsqrt(jnp.mean(x * x, axis=-1, keepdims=True) + EPS) * g


def _ada_kernel(c_ref, w_ref, b_ref, o_ref):
    c = c_ref[...]
    o_ref[...] = _dot(_silu(c), w_ref[...], HIGHEST) + b_ref[...]


def _ada(c_pad, w, b):
    n = w.shape[1]
    bn = 1024
    return pl.pallas_call(
        _ada_kernel,
        grid=(n // bn,),
        in_specs=[pl.BlockSpec(c_pad.shape, lambda j: (0, 0)),
                  pl.BlockSpec((D_MODEL, bn), lambda j: (0, j)),
                  pl.BlockSpec((1, bn), lambda j: (0, j))],
        out_specs=pl.BlockSpec((c_pad.shape[0], bn), lambda j: (0, j)),
        out_shape=jax.ShapeDtypeStruct((c_pad.shape[0], n), F32),
        compiler_params=_params(("arbitrary",)),
        name="ada",
    )(c_pad, w, b)


def _inproj_kernel(x_ref, sc_ref, sh_ref, g_ref, wf_ref, wg_ref, wgate_ref, ws_ref, bgate_ref,
                   bsm_ref, alog_ref, fqkv_ref, gqkv_ref, gz_ref, gates_ref, small_ref):
    x = x_ref[...]
    h = _rms(x, g_ref[...]) * (1.0 + sc_ref[...]) + sh_ref[...]
    hb = h.astype(BF16)
    pf = _dot(hb, wf_ref[...])
    lane = lax.broadcasted_iota(jnp.int32, (1, 3 * FOX_W), 1)
    pf = jnp.where(lane < FOX_W, pf * (FOX_DH ** -0.5 * LOG2E), pf)
    fqkv_ref[...] = pf.astype(BF16)
    pg = _dot(hb, wg_ref[...])
    gqkv_ref[...] = pg[:, :3 * GDN_KW]
    gz_ref[...] = pg[:, 3 * GDN_KW:]
    gates_ref[...] = _sigmoid(_dot(hb, wgate_ref[...]) + bgate_ref[...])
    z = _dot(hb, ws_ref[...]) + bsm_ref[...]
    sl = lax.broadcasted_iota(jnp.int32, (1, LANES), 1)
    neg_a = -jnp.exp(alog_ref[...])
    small_ref[...] = jnp.where(sl < FOX_HEADS, _log_sigmoid(z),
                               jnp.where(sl < FOX_HEADS + GDN_HEADS, _sigmoid(z), neg_a * _softplus(z)))


def _inproj(x, sc1, sh1, g_pre, wf, wg, wgate, ws, bgate, bsm, alog_row, tm):
    B, S, D = x.shape
    full = lambda a: pl.BlockSpec(a.shape, lambda b, i: (0,) * a.ndim)
    mod = pl.BlockSpec((None, 1, D), lambda b, i: (b, 0, 0))
    tile = lambda n: pl.BlockSpec((None, tm, n), lambda b, i: (b, i, 0))
    return pl.pallas_call(
        _inproj_kernel,
        grid=(B, S // tm),
        in_specs=[tile(D), mod, mod, full(g_pre), full(wf), full(wg), full(wgate), full(ws),
                  full(bgate), full(bsm), full(alog_row)],
        out_specs=[tile(3 * FOX_W), tile(3 * GDN_KW), tile(GDN_VW), tile(2 * D), tile(LANES)],
        out_shape=[jax.ShapeDtypeStruct((B, S, 3 * FOX_W), BF16),
                   jax.ShapeDtypeStruct((B, S, 3 * GDN_KW), F32),
                   jax.ShapeDtypeStruct((B, S, GDN_VW), F32),
                   jax.ShapeDtypeStruct((B, S, 2 * D), F32),
                   jax.ShapeDtypeStruct((B, S, LANES), F32)],
        compiler_params=_params(("arbitrary", "arbitrary")),
        name="inproj",
    )(x, sc1, sh1, g_pre, wf, wg, wgate, ws, bgate, bsm, alog_row)


def _cumsum_kernel(s_ref, o_ref, carry_ref):
    @pl.when(pl.program_id(1) == 0)
    def _():
        carry_ref[...] = jnp.zeros_like(carry_ref)

    v = s_ref[...]
    n = v.shape[0]
    r = lax.broadcasted_iota(jnp.int32, (n, n), 0)
    c = lax.broadcasted_iota(jnp.int32, (n, n), 1)
    tri = jnp.where(r >= c, 1.0, 0.0).astype(F32)
    cs = _dot(tri, v, HIGHEST) + carry_ref[...]
    o_ref[...] = cs
    carry_ref[...] = cs[n - 1:n, :]


def _cumsum(small, tc):
    B, S, L = small.shape
    return pl.pallas_call(
        _cumsum_kernel,
        grid=(B, S // tc),
        in_specs=[pl.BlockSpec((None, tc, L), lambda b, i: (b, i, 0))],
        out_specs=pl.BlockSpec((None, tc, L), lambda b, i: (b, i, 0)),
        out_shape=jax.ShapeDtypeStruct((B, S, L), F32),
        scratch_shapes=[pltpu.VMEM((1, L), F32)],
        compiler_params=_params(("arbitrary", "arbitrary")),
        name="cumsum",
    )(small)


def _split3(x):
    p1 = x.astype(BF16).astype(F32)
    r = x - p1
    p2 = r.astype(BF16).astype(F32)
    p3 = (r - p2).astype(BF16).astype(F32)
    return p1, p2, p3


def _fox_kernel(q_ref, k_ref, v_ref, cum_ref, o_ref, kmod_ref, vmod_ref, m_ref, acc_ref, *, tq):
    hp = pl.program_id(1)
    i = pl.program_id(2)
    S = k_ref.shape[0]
    lane = lax.broadcasted_iota(jnp.int32, (1, LANES), 1)

    def head_col(tile, h):
        return jnp.sum(jnp.where(lane == h, tile, 0.0), axis=1, keepdims=True)

    def with_bias(base, hh, pieces, ones_first):
        sp = (1 - hh) * FOX_DH
        in_head = (lane >= hh * FOX_DH) & (lane < (hh + 1) * FOX_DH)
        out = jnp.where(in_head, base, 0.0)
        po, oo = (3, 0) if ones_first else (0, 3)
        for t in range(3):
            out = jnp.where(lane == sp + po + t, pieces[t], out)
            out = jnp.where(lane == sp + oo + t, 1.0, out)
        return out.astype(BF16)

    @pl.when(i == 0)
    def _():
        def chunk(ci, carry):
            start = pl.multiple_of(ci * tq, tq)
            k2 = k_ref[pl.ds(start, tq), :].astype(F32)
            v2 = v_ref[pl.ds(start, tq), :].astype(F32)
            cum = cum_ref[pl.ds(start, tq), :]
            for hh in range(2):
                ck = head_col(cum, hp * 2 + hh)
                kmod_ref[hh, pl.ds(start, tq), :] = with_bias(k2, hh, _split3(-ck * LOG2E), True)
                in_head = (lane >= hh * FOX_DH) & (lane < (hh + 1) * FOX_DH)
                vm = jnp.where(in_head, v2, jnp.where(lane == (1 - hh) * FOX_DH, 1.0, 0.0))
                vmod_ref[hh, pl.ds(start, tq), :] = vm.astype(BF16)
            return carry
        lax.fori_loop(0, S // tq, chunk, 0)

    q2 = q_ref[...].astype(F32)
    cumq = cum_ref[pl.ds(pl.multiple_of(i * tq, tq), tq), :]
    qmods = [with_bias(q2, hh, _split3(head_col(cumq, hp * 2 + hh) * LOG2E), False) for hh in range(2)]
    rr = lax.broadcasted_iota(jnp.int32, (tq, tq), 0)
    cc = lax.broadcasted_iota(jnp.int32, (tq, tq), 1)
    m_ref[...] = jnp.full(m_ref.shape, NEG_BIG, F32)
    acc_ref[...] = jnp.zeros(acc_ref.shape, F32)

    def step(j, masked):
        start = pl.multiple_of(j * tq, tq)
        scores = [_dot_nt(qmods[hh], kmod_ref[hh, pl.ds(start, tq), :]) for hh in range(2)]
        for hh in range(2):
            s = scores[hh]
            if masked:
                s = jnp.where(rr >= cc, s, NEG_BIG)
            m_prev = m_ref[hh]
            m_new = jnp.maximum(m_prev, jnp.max(s, axis=1, keepdims=True))
            p = jnp.exp2(s - jnp.concatenate([m_new] * (tq // LANES), axis=1)).astype(BF16)
            acc_ref[hh] = jnp.exp2(m_prev - m_new) * acc_ref[hh] + _dot(p, vmod_ref[hh, pl.ds(start, tq), :])
            m_ref[hh] = m_new

    def body(j, carry):
        step(j, False)
        return carry

    lax.fori_loop(0, i, body, 0)
    step(i, True)
    outs = []
    for hh in range(2):
        acc = acc_ref[hh]
        outs.append(acc / head_col(acc, (1 - hh) * FOX_DH))
    o_ref[...] = jnp.where(lane < FOX_DH, outs[0], outs[1]).astype(o_ref.dtype)


def _fox(fqkv, cum, tq):
    B, S, _ = fqkv.shape
    nhp = FOX_HEADS // 2
    return pl.pallas_call(
        functools.partial(_fox_kernel, tq=tq),
        grid=(B, nhp, S // tq),
        in_specs=[pl.BlockSpec((None, tq, LANES), lambda b, hp, i: (b, i, hp)),
                  pl.BlockSpec((None, S, LANES), lambda b, hp, i: (b, 0, nhp + hp)),
                  pl.BlockSpec((None, S, LANES), lambda b, hp, i: (b, 0, 2 * nhp + hp)),
                  pl.BlockSpec((None, S, LANES), lambda b, hp, i: (b, 0, 0))],
        out_specs=pl.BlockSpec((None, tq, LANES), lambda b, hp, i: (b, i, hp)),
        out_shape=jax.ShapeDtypeStruct((B, S, FOX_W), BF16),
        scratch_shapes=[pltpu.VMEM((2, S, LANES), BF16), pltpu.VMEM((2, S, LANES), BF16),
                        pltpu.VMEM((2, tq, LANES), F32), pltpu.VMEM((2, tq, LANES), F32)],
        compiler_params=_params(("arbitrary", "arbitrary", "arbitrary")),
        name="fox",
    )(fqkv, fqkv, fqkv, cum)


def _level_masks(n):
    r = lax.broadcasted_iota(jnp.int32, (n, n), 0)
    c = lax.broadcasted_iota(jnp.int32, (n, n), 1)
    masks = []
    s = 1
    while s < n:
        sel = ((r // (2 * s)) == (c // (2 * s))) & (((r // s) % 2) == 1) & (((c // s) % 2) == 0)
        masks.append(jnp.where(sel, 1.0, 0.0).astype(F32))
        s *= 2
    return masks


def _unit_lower_inverses(lows, masks):
    n = lows[0].shape[0]
    r = lax.broadcasted_iota(jnp.int32, (n, n), 0)
    c = lax.broadcasted_iota(jnp.int32, (n, n), 1)
    xs = [jnp.where(r == c, 1.0, 0.0).astype(F32)] * len(lows)
    for mask in masks:
        xbs = [x.astype(BF16) for x in xs]
        xcs = [_dot(xb, (low * mask).astype(BF16)) for xb, low in zip(xbs, lows)]
        xs = [x - _dot(xc.astype(BF16), xb) for x, xc, xb in zip(xs, xcs, xbs)]
    return xs


def _gdn_kernel(x_ref, gz_ref, sm_ref, cw_ref, nw_ref, o_ref, state_ref, carry_ref):
    ci = pl.program_id(0)
    b = pl.program_id(1)
    C = GDN_CHUNK

    @pl.when(ci == 0)
    def _():
        state_ref[b] = jnp.zeros(state_ref.shape[1:], F32)
        carry_ref[b] = jnp.zeros(carry_ref.shape[1:], F32)

    x = x_ref[...]
    prev = carry_ref[b]
    row8 = lax.broadcasted_iota(jnp.int32, (8, 1), 0)
    y = cw_ref[CONV_W - 1:CONV_W, :] * x
    for j in range(1, CONV_W):
        xr = pltpu.roll(x, j, 0)
        pr = pltpu.roll(prev, j, 0)
        first = jnp.where(row8 < j, pr, xr[0:8, :])
        y = y + cw_ref[CONV_W - 1 - j:CONV_W - j, :] * jnp.concatenate([first, xr[8:, :]], axis=0)
    carry_ref[b] = x[C - 8:C, :]
    y = _silu(y)

    sm = sm_ref[...]
    r = lax.broadcasted_iota(jnp.int32, (C, C), 0)
    c = lax.broadcasted_iota(jnp.int32, (C, C), 1)
    causal = r >= c
    strict = r > c
    tri = jnp.where(causal, 1.0, 0.0).astype(F32)
    gcum = _dot(tri, sm, HIGHEST)
    gcum_t = gcum.T
    nw = nw_ref[...]
    gz = gz_ref[...]
    masks = _level_masks(C)

    heads = range(GDN_HEADS)
    qs, ks, vs, betas, g_cols, g_lasts, decays = [], [], [], [], [], [], []
    for h in heads:
        q = y[:, h * GDN_DK:(h + 1) * GDN_DK]
        k = y[:, GDN_KW + h * GDN_DK:GDN_KW + (h + 1) * GDN_DK]
        qs.append(q * lax.rsqrt(jnp.sum(q * q, axis=1, keepdims=True) + EPS) * (GDN_DK ** -0.5))
        ks.append(k * lax.rsqrt(jnp.sum(k * k, axis=1, keepdims=True) + EPS))
        vs.append(y[:, 2 * GDN_KW + h * GDN_DV:2 * GDN_KW + (h + 1) * GDN_DV])
        betas.append(sm[:, FOX_HEADS + h:FOX_HEADS + h + 1])
        gl = FOX_HEADS + GDN_HEADS + h
        g_cols.append(gcum[:, gl:gl + 1])
        g_lasts.append(gcum[C - 1:C, gl:gl + 1])
        diff = g_cols[h] - gcum_t[gl:gl + 1, :]
        decays.append(jnp.where(causal, jnp.exp(jnp.where(causal, diff, 0.0)), 0.0))
    kbs = [ks[h] * betas[h] for h in heads]
    egs = [jnp.exp(g_cols[h]) for h in heads]
    kqs = [_dot_nt(jnp.concatenate([kbs[h], qs[h]], axis=0).astype(BF16), ks[h].astype(BF16)) for h in heads]
    lows = [jnp.where(strict, kqs[h][:C] * decays[h], 0.0) for h in heads]
    attns = [(kqs[h][C:] * decays[h]).astype(BF16) for h in heads]
    tinvs = _unit_lower_inverses(lows, masks)
    uws = [_dot(tinvs[h].astype(BF16),
                jnp.concatenate([vs[h] * betas[h], kbs[h] * egs[h]], axis=1).astype(BF16)) for h in heads]
    sts = [state_ref[b, h] for h in heads]
    wqs = [_dot(jnp.concatenate([uws[h][:, GDN_DV:], qs[h] * egs[h]], axis=0).astype(BF16),
                sts[h].astype(BF16)) for h in heads]
    v_news = [(uws[h][:, :GDN_DV] - wqs[h][:C]).astype(BF16) for h in heads]
    k_dec_ts = [(ks[h] * jnp.exp(g_lasts[h] - g_cols[h])).T.astype(BF16) for h in heads]
    os_ = [wqs[h][C:] + _dot(attns[h], v_news[h]) for h in heads]
    new_sts = [sts[h] * jnp.exp(g_lasts[h]) + _dot(k_dec_ts[h], v_news[h]) for h in heads]
    for h in heads:
        state_ref[b, h] = new_sts[h]
        o = _rms(os_[h], nw) * _silu(gz[:, h * GDN_DV:(h + 1) * GDN_DV])
        o_ref[:, h * GDN_DV:(h + 1) * GDN_DV] = o.astype(o_ref.dtype)


def _gdn(gqkv, gz, small, conv_w, norm_w):
    B, S, _ = gqkv.shape
    C = GDN_CHUNK
    tile = lambda n: pl.BlockSpec((None, C, n), lambda ci, b: (b, ci, 0))
    full = lambda a: pl.BlockSpec(a.shape, lambda ci, b: (0,) * a.ndim)
    return pl.pallas_call(
        _gdn_kernel,
        grid=(S // C, B),
        in_specs=[tile(3 * GDN_KW), tile(GDN_VW), tile(LANES), full(conv_w), full(norm_w)],
        out_specs=tile(GDN_VW),
        out_shape=jax.ShapeDtypeStruct((B, S, GDN_VW), BF16),
        scratch_shapes=[pltpu.VMEM((B, GDN_HEADS, GDN_DK, GDN_DV), F32),
                        pltpu.VMEM((B, 8, 3 * GDN_KW), F32)],
        compiler_params=_params(("arbitrary", "arbitrary")),
        name="gdn",
    )(gqkv, gz, small, conv_w, norm_w)


def _outproj_kernel(ya_ref, yb_ref, gates_ref, x_ref, gt1_ref, sc2_ref, sh2_ref, gpm_ref, gpf_ref,
                    wa_ref, wb_ref, wo_ref, wq_ref, x1_ref, h2_ref, qy_ref):
    ya = _dot(ya_ref[...], wa_ref[...])
    yb = _dot(yb_ref[...], wb_ref[...])
    gates = gates_ref[...]
    mix_in = gates[:, :D_MODEL] * ya + gates[:, D_MODEL:] * yb
    mix = _dot(mix_in.astype(BF16), wo_ref[...])
    x1 = x_ref[...] + gt1_ref[...] * _rms(mix, gpm_ref[...])
    x1_ref[...] = x1
    h2 = (_rms(x1, gpf_ref[...]) * (1.0 + sc2_ref[...]) + sh2_ref[...]).astype(BF16)
    h2_ref[...] = h2
    qy_ref[...] = _dot(h2, wq_ref[...])


def _outproj(ya, yb, gates, x, gt1, sc2, sh2, gpm, gpf, wa, wb, wo, wq, tm):
    B, S, D = x.shape
    full = lambda a: pl.BlockSpec(a.shape, lambda b, i: (0,) * a.ndim)
    mod = pl.BlockSpec((None, 1, D), lambda b, i: (b, 0, 0))
    tile = lambda n: pl.BlockSpec((None, tm, n), lambda b, i: (b, i, 0))
    nq = wq.shape[1]
    return pl.pallas_call(
        _outproj_kernel,
        grid=(B, S // tm),
        in_specs=[tile(FOX_W), tile(GDN_VW), tile(2 * D), tile(D), mod, mod, mod, full(gpm), full(gpf),
                  full(wa), full(wb), full(wo), full(wq)],
        out_specs=[tile(D), tile(D), tile(nq)],
        out_shape=[jax.ShapeDtypeStruct((B, S, D), F32),
                   jax.ShapeDtypeStruct((B, S, D), BF16),
                   jax.ShapeDtypeStruct((B, S, nq), F32)],
        compiler_params=_params(("arbitrary", "arbitrary")),
        name="outproj",
    )(ya, yb, gates, x, gt1, sc2, sh2, gpm, gpf, wa, wb, wo, wq)


SUBLANES = 8


def _cmpx(a, i, l, desc):
    hi, lo = jnp.maximum(a[i], a[l]), jnp.minimum(a[i], a[l])
    a[i], a[l] = (hi, lo) if desc else (lo, hi)


def _top16_desc(a):
    a = list(a)
    n = len(a)
    k = 2
    while k <= n:
        j = k // 2
        while j >= 1:
            for i in range(n):
                l = i ^ j
                if l > i:
                    _cmpx(a, i, l, (i & k) == 0)
            j //= 2
        k *= 2
    shift = SUBLANES // 2
    while shift >= 1:
        other = [pltpu.roll(v, shift, 0) for v in a]
        a = [jnp.maximum(a[i], other[n - 1 - i]) for i in range(n)]
        j = n // 2
        while j >= 1:
            for i in range(n):
                l = i ^ j
                if l > i:
                    _cmpx(a, i, l, True)
            j //= 2
        shift //= 2
    return a


def _topk_kernel(qy_ref, k1_ref, k2_ref, s1_ref, s2_ref, tau_ref, m1_ref, m2_ref, zinv_ref):
    K = PEER_TOPK
    half = PEER_DQ // 2
    tt = qy_ref.shape[0]
    sub = lax.broadcasted_iota(jnp.int32, (SUBLANES, LANES), 0)

    def stagger(d, base):
        out = d[base]
        for r in range(1, SUBLANES):
            out = jnp.where(sub == r, d[base + r], out)
        return out

    def head(h, carry):
        col = pl.multiple_of(h * PEER_DQ, PEER_DQ)
        q1 = qy_ref[:, pl.ds(col, half)]
        q2 = qy_ref[:, pl.ds(col + half, half)]
        s1_ref[h] = _dot_nt(k1_ref[...], q1, HIGHEST)
        s2_ref[h] = _dot_nt(k2_ref[...], q2, HIGHEST)
        for c in range(tt // LANES):
            cs = slice(c * LANES, (c + 1) * LANES)
            rows = lambda ref: [ref[h, SUBLANES * i:SUBLANES * (i + 1), cs] for i in range(N_KEYS // SUBLANES)]
            d1 = _top16_desc(rows(s1_ref))
            d2 = _top16_desc(rows(s2_ref))
            d2_lo, d2_hi, d1_hi = stagger(d2, 0), stagger(d2, SUBLANES), stagger(d1, SUBLANES)
            cand = [d1[0] + d2_lo, d1[0] + d2_hi, d1_hi + d2[0]] + [d1[i] + d2_lo for i in range(1, SUBLANES)]
            cand += [jnp.full((SUBLANES, LANES), -jnp.inf, F32)] * (K - len(cand))
            top = _top16_desc(cand)
            z = jnp.ones_like(top[0])
            for i in range(1, K):
                z = z + jnp.exp(top[i] - top[0])
            tau_ref[h, :, cs] = top[K - 1][0:1, :]
            m1_ref[h, :, cs] = d1[0][0:1, :]
            m2_ref[h, :, cs] = d2[0][0:1, :]
            zinv_ref[h, :, cs] = (1.0 / z)[0:1, :]
        return carry

    lax.fori_loop(0, PEER_HEADS, head, 0)


def _topk(qy, keys1, keys2, tt):
    T, nq = qy.shape
    H = PEER_HEADS
    full = lambda a: pl.BlockSpec(a.shape, lambda i: (0,) * a.ndim)
    big = pl.BlockSpec((H, N_KEYS, tt), lambda i: (0, 0, i))
    stat = pl.BlockSpec((H, 1, tt), lambda i: (0, 0, i))
    return pl.pallas_call(
        _topk_kernel,
        grid=(T // tt,),
        in_specs=[pl.BlockSpec((tt, nq), lambda i: (i, 0)), full(keys1), full(keys2)],
        out_specs=[big, big, stat, stat, stat, stat],
        out_shape=[jax.ShapeDtypeStruct((H, N_KEYS, T), F32)] * 2 + [jax.ShapeDtypeStruct((H, 1, T), F32)] * 4,
        compiler_params=_params(("arbitrary",)),
        name="topk",
    )(qy, keys1, keys2)


def _peer_kernel(h2_ref, u_ref, vt_ref, s1_ref, s2_ref, tau_ref, m1_ref, m2_ref, zinv_ref, x1_ref,
                 gt2_ref, g_ref, o_ref, acc_ref, e2_ref, wt_ref, *, na):
    j = pl.program_id(1)
    H = PEER_HEADS

    @pl.when(j == 0)
    def _():
        acc_ref[...] = jnp.zeros(acc_ref.shape, F32)
        for h in range(H):
            e2_ref[h] = jnp.exp(s2_ref[h] - m2_ref[h])

    st = _dot_nt(u_ref[...], h2_ref[...])
    act = 0.5 * st * (1.0 + lax.erf(st * (0.5 ** 0.5)))
    for al in range(na):
        wa = jnp.zeros((N_KEYS, st.shape[1]), F32)
        for h in range(H):
            s1row = s1_ref[h, al:al + 1, :]
            e1row = jnp.exp(s1row - m1_ref[h]) * zinv_ref[h]
            score = s1row + s2_ref[h]
            wa = wa + jnp.where(score >= tau_ref[h], e1row * e2_ref[h], 0.0)
        wt_ref[al * N_KEYS:(al + 1) * N_KEYS, :] = (wa * act[al * N_KEYS:(al + 1) * N_KEYS, :]).astype(BF16)
    acc_ref[...] += _dot(vt_ref[...], wt_ref[...])

    @pl.when(j == pl.num_programs(1) - 1)
    def _():
        y = acc_ref[...].T
        o_ref[...] = x1_ref[...] + gt2_ref[...] * _rms(y, g_ref[...])


def _peer(h2, u_b, vt_b, s1t, s2t, tau, m1, m2, zinv, x1, gt2, g_post, tt, ec, tiles_per_batch):
    T, D = h2.shape
    E = u_b.shape[0]
    H = PEER_HEADS
    na = ec // N_KEYS
    stat = pl.BlockSpec((H, 1, tt), lambda i, j: (0, 0, i))
    return pl.pallas_call(
        functools.partial(_peer_kernel, na=na),
        grid=(T // tt, E // ec),
        in_specs=[pl.BlockSpec((tt, D), lambda i, j: (i, 0)),
                  pl.BlockSpec((ec, D), lambda i, j: (j, 0)),
                  pl.BlockSpec((D, ec), lambda i, j: (0, j)),
                  pl.BlockSpec((H, na, tt), lambda i, j: (0, j, i)),
                  pl.BlockSpec((H, N_KEYS, tt), lambda i, j: (0, 0, i)),
                  stat, stat, stat, stat,
                  pl.BlockSpec((tt, D), lambda i, j: (i, 0)),
                  pl.BlockSpec((None, 1, D), lambda i, j: (i // tiles_per_batch, 0, 0)),
                  pl.BlockSpec(g_post.shape, lambda i, j: (0, 0))],
        out_specs=pl.BlockSpec((tt, D), lambda i, j: (i, 0)),
        out_shape=jax.ShapeDtypeStruct((T, D), F32),
        scratch_shapes=[pltpu.VMEM((D, tt), F32), pltpu.VMEM((H, N_KEYS, tt), F32),
                        pltpu.VMEM((ec, tt), BF16)],
        compiler_params=_params(("arbitrary", "arbitrary")),
        name="peer",
    )(h2, u_b, vt_b, s1t, s2t, tau, m1, m2, zinv, x1, gt2, g_post)


def _pick(n, pref):
    t = pref
    while n % t:
        t //= 2
    return t


def _layer(x, c_pad, w_ada, b_ada, g_pre_mix, g_post_mix, w_in, b_fgate, fox_w_o, conv_w, a_log, dt_bias,
           gdn_norm_w, gdn_w_o, b_branch_gate, w_out, g_pre_ffn, g_post_ffn, peer_w_query, peer_keys1,
           peer_keys2, peer_u, peer_v):
    B, S, D = x.shape
    T = B * S
    mod = _ada(c_pad, w_ada, b_ada.reshape(1, -1))[:B]
    sh1, sc1, gt1, sh2, sc2, gt2 = [m.reshape(B, 1, D) for m in jnp.split(mod, 6, axis=-1)]

    off = [0]
    for s in SPLITS:
        off.append(off[-1] + s)
    col = lambda i, j: w_in[:, off[i]:off[j]]
    wf = col(0, 3).astype(BF16)
    wg = jnp.concatenate([col(4, 8)], axis=1).astype(BF16)
    wgate = col(10, 12).astype(BF16)
    pad = LANES - (FOX_HEADS + 2 * GDN_HEADS)
    ws = jnp.concatenate([col(3, 4), col(8, 10), jnp.zeros((D, pad), F32)], axis=1).astype(BF16)
    zpad = jnp.zeros((pad,), F32)
    bsm = jnp.concatenate([b_fgate, jnp.zeros((GDN_HEADS,), F32), dt_bias, zpad]).reshape(1, LANES)
    alog_row = jnp.concatenate([jnp.zeros((FOX_HEADS + GDN_HEADS,), F32), a_log, zpad]).reshape(1, LANES)
    bgate = b_branch_gate.reshape(1, 2 * D)

    tm = _pick(S, 256)
    fqkv, gqkv, gz, gates, small = _inproj(x, sc1, sh1, g_pre_mix.reshape(1, D), wf, wg, wgate, ws, bgate,
                                           bsm, alog_row, tm)
    cum = _cumsum(small, _pick(S, 512))
    ya = _fox(fqkv, cum, _pick(S, 512))
    yb = _gdn(gqkv, gz, small, conv_w, gdn_norm_w.reshape(1, GDN_DV))
    x1, h2, qy = _outproj(ya, yb, gates, x, gt1, sc2, sh2, g_post_mix.reshape(1, D), g_pre_ffn.reshape(1, D),
                          fox_w_o.astype(BF16), gdn_w_o.astype(BF16), w_out.astype(BF16),
                          peer_w_query.astype(BF16), tm)
    qy = qy.reshape(T, -1)
    s1t, s2t, tau, m1, m2, zinv = _topk(qy, peer_keys1, peer_keys2, _pick(T, 256))
    tt = _pick(S, 512)
    out = _peer(h2.reshape(T, D), peer_u.astype(BF16), peer_v.T.astype(BF16), s1t, s2t, tau, m1, m2, zinv,
                x1.reshape(T, D), gt2, g_post_ffn.reshape(1, D), tt, 1024, S // tt)
    return out.reshape(B, S, D)


def kernel(x, c, w_ada, b_ada, g_pre_mix, g_post_mix, w_in, b_fgate, fox_w_o, conv_w, a_log, dt_bias,
           gdn_norm_w, gdn_w_o, b_branch_gate, w_out, g_pre_ffn, g_post_ffn, peer_w_query, peer_keys1,
           peer_keys2, peer_u, peer_v):
    B = x.shape[0]
    c_pad = jnp.pad(c, ((0, (-B) % 8), (0, 0)))
    for l in range(w_ada.shape[0]):
        x = _layer(x, c_pad, w_ada[l], b_ada[l], g_pre_mix[l], g_post_mix[l], w_in[l], b_fgate[l],
                   fox_w_o[l], conv_w[l], a_log[l], dt_bias[l], gdn_norm_w[l], gdn_w_o[l], b_branch_gate[l],
                   w_out[l], g_pre_ffn[l], g_post_ffn[l], peer_w_query[l], peer_keys1[l], peer_keys2[l],
                   peer_u[l], peer_v[l])
    return x
```

```python
import functools
import math

import jax
import jax.numpy as jnp
from jax import lax
from jax.experimental import pallas as pl
from jax.experimental.pallas import tpu as pltpu

F32 = jnp.float32
BF16 = jnp.bfloat16
HIGHEST = lax.Precision.HIGHEST

D_MODEL = 1024
EPS = 1e-6
FOX_HEADS = 8
FOX_DH = 64
GDN_HEADS = 4
GDN_DK = 128
GDN_DV = 128
CONV_W = 4
PEER_HEADS = 8
N_KEYS = 128
PEER_DQ = 256
PEER_TOPK = 16
FOX_W = FOX_HEADS * FOX_DH
GDN_KW = GDN_HEADS * GDN_DK
GDN_VW = GDN_HEADS * GDN_DV
SPLITS = (FOX_W, FOX_W, FOX_W, FOX_HEADS, GDN_KW, GDN_KW, GDN_VW, GDN_VW, GDN_HEADS, GDN_HEADS,
          D_MODEL, D_MODEL)

LANES = 128
GDN_CHUNK = 128
NEG_BIG = -1e30
LOG2E = 1.4426950408889634
VMEM_LIMIT = 56 * 1024 * 1024


def _params(sem):
    return pltpu.CompilerParams(dimension_semantics=sem, vmem_limit_bytes=VMEM_LIMIT)


def _dot(a, b, precision=None):
    return jnp.dot(a, b, preferred_element_type=F32, precision=precision)


def _dot_nt(a, b, precision=None):
    return lax.dot_general(a, b, (((1,), (1,)), ((), ())), preferred_element_type=F32,
                           precision=precision)


def _sigmoid(x):
    return 1.0 / (1.0 + jnp.exp(-x))


def _silu(x):
    return x * _sigmoid(x)


def _softplus(x):
    return jnp.maximum(x, 0.0) + jnp.log(1.0 + jnp.exp(-jnp.abs(x)))


def _log_sigmoid(x):
    return -_softplus(-x)


def _rms(x, g):
    return x * lax.rsqrt(jnp.mean(x * x, axis=-1, keepdims=True) + EPS) * g


def _ada_kernel(c_ref, w_ref, b_ref, o_ref):
    c = c_ref[...]
    o_ref[...] = _dot(_silu(c), w_ref[...], HIGHEST) + b_ref[...]


def _ada(c_pad, w, b):
    n = w.shape[1]
    bn = 1024
    return pl.pallas_call(
        _ada_kernel,
        grid=(n // bn,),
        in_specs=[pl.BlockSpec(c_pad.shape, lambda j: (0, 0)),
                  pl.BlockSpec((D_MODEL, bn), lambda j: (0, j)),
                  pl.BlockSpec((1, bn), lambda j: (0, j))],
        out_specs=pl.BlockSpec((c_pad.shape[0], bn), lambda j: (0, j)),
        out_shape=jax.ShapeDtypeStruct((c_pad.shape[0], n), F32),
        compiler_params=_params(("arbitrary",)),
        name="ada",
    )(c_pad, w, b)


def _inproj_kernel(x_ref, sc_ref, sh_ref, g_ref, wf_ref, wg_ref, wgate_ref, ws_ref, bgate_ref,
                   bsm_ref, alog_ref, fqkv_ref, gqkv_ref, gz_ref, gates_ref, small_ref):
    x = x_ref[...]
    h = _rms(x, g_ref[...]) * (1.0 + sc_ref[...]) + sh_ref[...]
    hb = h.astype(BF16)
    pf = _dot(hb, wf_ref[...])
    lane = lax.broadcasted_iota(jnp.int32, (1, 3 * FOX_W), 1)
    pf = jnp.where(lane < FOX_W, pf * (FOX_DH ** -0.5 * LOG2E), pf)
    fqkv_ref[...] = pf.astype(BF16)
    pg = _dot(hb, wg_ref[...])
    gqkv_ref[...] = pg[:, :3 * GDN_KW]
    gz_ref[...] = pg[:, 3 * GDN_KW:]
    gates_ref[...] = _sigmoid(_dot(hb, wgate_ref[...]) + bgate_ref[...])
    z = _dot(hb, ws_ref[...]) + bsm_ref[...]
    sl = lax.broadcasted_iota(jnp.int32, (1, LANES), 1)
    neg_a = -jnp.exp(alog_ref[...])
    small_ref[...] = jnp.where(sl < FOX_HEADS, _log_sigmoid(z),
                               jnp.where(sl < FOX_HEADS + GDN_HEADS, _sigmoid(z), neg_a * _softplus(z)))


def _inproj(x, sc1, sh1, g_pre, wf, wg, wgate, ws, bgate, bsm, alog_row, tm):
    B, S, D = x.shape
    full = lambda a: pl.BlockSpec(a.shape, lambda b, i: (0,) * a.ndim)
    mod = pl.BlockSpec((None, 1, D), lambda b, i: (b, 0, 0))
    tile = lambda n: pl.BlockSpec((None, tm, n), lambda b, i: (b, i, 0))
    return pl.pallas_call(
        _inproj_kernel,
        grid=(B, S // tm),
        in_specs=[tile(D), mod, mod, full(g_pre), full(wf), full(wg), full(wgate), full(ws),
                  full(bgate), full(bsm), full(alog_row)],
        out_specs=[tile(3 * FOX_W), tile(3 * GDN_KW), tile(GDN_VW), tile(2 * D), tile(LANES)],
        out_shape=[jax.ShapeDtypeStruct((B, S, 3 * FOX_W), BF16),
                   jax.ShapeDtypeStruct((B, S, 3 * GDN_KW), F32),
                   jax.ShapeDtypeStruct((B, S, GDN_VW), F32),
                   jax.ShapeDtypeStruct((B, S, 2 * D), F32),
                   jax.ShapeDtypeStruct((B, S, LANES), F32)],
        compiler_params=_params(("arbitrary", "arbitrary")),
        name="inproj",
    )(x, sc1, sh1, g_pre, wf, wg, wgate, ws, bgate, bsm, alog_row)


def _cumsum_kernel(s_ref, o_ref, carry_ref):
    @pl.when(pl.program_id(1) == 0)
    def _():
        carry_ref[...] = jnp.zeros_like(carry_ref)

    v = s_ref[...]
    n = v.shape[0]
    r = lax.broadcasted_iota(jnp.int32, (n, n), 0)
    c = lax.broadcasted_iota(jnp.int32, (n, n), 1)
    tri = jnp.where(r >= c, 1.0, 0.0).astype(F32)
    cs = _dot(tri, v, HIGHEST) + carry_ref[...]
    o_ref[...] = cs
    carry_ref[...] = cs[n - 1:n, :]


def _cumsum(small, tc):
    B, S, L = small.shape
    return pl.pallas_call(
        _cumsum_kernel,
        grid=(B, S // tc),
        in_specs=[pl.BlockSpec((None, tc, L), lambda b, i: (b, i, 0))],
        out_specs=pl.BlockSpec((None, tc, L), lambda b, i: (b, i, 0)),
        out_shape=jax.ShapeDtypeStruct((B, S, L), F32),
        scratch_shapes=[pltpu.VMEM((1, L), F32)],
        compiler_params=_params(("arbitrary", "arbitrary")),
        name="cumsum",
    )(small)


def _split3(x):
    p1 = x.astype(BF16).astype(F32)
    r = x - p1
    p2 = r.astype(BF16).astype(F32)
    p3 = (r - p2).astype(BF16).astype(F32)
    return p1, p2, p3


def _fox_kernel(q_ref, k_ref, v_ref, cum_ref, o_ref, kmod_ref, vmod_ref, m_ref, acc_ref, *, tq):
    hp = pl.program_id(1)
    i = pl.program_id(2)
    S = k_ref.shape[0]
    lane = lax.broadcasted_iota(jnp.int32, (1, LANES), 1)

    def head_col(tile, h):
        return jnp.sum(jnp.where(lane == h, tile, 0.0), axis=1, keepdims=True)

    def with_bias(base, hh, pieces, ones_first):
        sp = (1 - hh) * FOX_DH
        in_head = (lane >= hh * FOX_DH) & (lane < (hh + 1) * FOX_DH)
        out = jnp.where(in_head, base, 0.0)
        po, oo = (3, 0) if ones_first else (0, 3)
        for t in range(3):
            out = jnp.where(lane == sp + po + t, pieces[t], out)
            out = jnp.where(lane == sp + oo + t, 1.0, out)
        return out.astype(BF16)

    @pl.when(i == 0)
    def _():
        def chunk(ci, carry):
            start = pl.multiple_of(ci * tq, tq)
            k2 = k_ref[pl.ds(start, tq), :].astype(F32)
            v2 = v_ref[pl.ds(start, tq), :].astype(F32)
            cum = cum_ref[pl.ds(start, tq), :]
            for hh in range(2):
                ck = head_col(cum, hp * 2 + hh)
                kmod_ref[hh, pl.ds(start, tq), :] = with_bias(k2, hh, _split3(-ck * LOG2E), True)
                in_head = (lane >= hh * FOX_DH) & (lane < (hh + 1) * FOX_DH)
                vm = jnp.where(in_head, v2, jnp.where(lane == (1 - hh) * FOX_DH, 1.0, 0.0))
                vmod_ref[hh, pl.ds(start, tq), :] = vm.astype(BF16)
            return carry
        lax.fori_loop(0, S // tq, chunk, 0)

    q2 = q_ref[...].astype(F32)
    cumq = cum_ref[pl.ds(pl.multiple_of(i * tq, tq), tq), :]
    qmods = [with_bias(q2, hh, _split3(head_col(cumq, hp * 2 + hh) * LOG2E), False) for hh in range(2)]
    rr = lax.broadcasted_iota(jnp.int32, (tq, tq), 0)
    cc = lax.broadcasted_iota(jnp.int32, (tq, tq), 1)
    m_ref[...] = jnp.full(m_ref.shape, NEG_BIG, F32)
    acc_ref[...] = jnp.zeros(acc_ref.shape, F32)

    def step(j, masked):
        start = pl.multiple_of(j * tq, tq)
        scores = [_dot_nt(qmods[hh], kmod_ref[hh, pl.ds(start, tq), :]) for hh in range(2)]
        for hh in range(2):
            s = scores[hh]
            if masked:
                s = jnp.where(rr >= cc, s, NEG_BIG)
            m_prev = m_ref[hh]
            m_new = jnp.maximum(m_prev, jnp.max(s, axis=1, keepdims=True))
            p = jnp.exp2(s - jnp.concatenate([m_new] * (tq // LANES), axis=1)).astype(BF16)
            acc_ref[hh] = jnp.exp2(m_prev - m_new) * acc_ref[hh] + _dot(p, vmod_ref[hh, pl.ds(start, tq), :])
            m_ref[hh] = m_new

    def body(j, carry):
        step(j, False)
        return carry

    lax.fori_loop(0, i, body, 0)
    step(i, True)
    outs = []
    for hh in range(2):
        acc = acc_ref[hh]
        outs.append(acc / head_col(acc, (1 - hh) * FOX_DH))
    o_ref[...] = jnp.where(lane < FOX_DH, outs[0], outs[1]).astype(o_ref.dtype)


def _fox(fqkv, cum, tq):
    B, S, _ = fqkv.shape
    nhp = FOX_HEADS // 2
    return pl.pallas_call(
        functools.partial(_fox_kernel, tq=tq),
        grid=(B, nhp, S // tq),
        in_specs=[pl.BlockSpec((None, tq, LANES), lambda b, hp, i: (b, i, hp)),
                  pl.BlockSpec((None, S, LANES), lambda b, hp, i: (b, 0, nhp + hp)),
                  pl.BlockSpec((None, S, LANES), lambda b, hp, i: (b, 0, 2 * nhp + hp)),
                  pl.BlockSpec((None, S, LANES), lambda b, hp, i: (b, 0, 0))],
        out_specs=pl.BlockSpec((None, tq, LANES), lambda b, hp, i: (b, i, hp)),
        out_shape=jax.ShapeDtypeStruct((B, S, FOX_W), BF16),
        scratch_shapes=[pltpu.VMEM((2, S, LANES), BF16), pltpu.VMEM((2, S, LANES), BF16),
                        pltpu.VMEM((2, tq, LANES), F32), pltpu.VMEM((2, tq, LANES), F32)],
        compiler_params=_params(("arbitrary", "arbitrary", "arbitrary")),
        name="fox",
    )(fqkv, fqkv, fqkv, cum)


def _level_masks(n):
    r = lax.broadcasted_iota(jnp.int32, (n, n), 0)
    c = lax.broadcasted_iota(jnp.int32, (n, n), 1)
    masks = []
    s = 1
    while s < n:
        sel = ((r // (2 * s)) == (c // (2 * s))) & (((r // s) % 2) == 1) & (((c // s) % 2) == 0)
        masks.append(jnp.where(sel, 1.0, 0.0).astype(F32))
        s *= 2
    return masks


def _unit_lower_inverses(lows, masks):
    n = lows[0].shape[0]
    r = lax.broadcasted_iota(jnp.int32, (n, n), 0)
    c = lax.broadcasted_iota(jnp.int32, (n, n), 1)
    xs = [jnp.where(r == c, 1.0, 0.0).astype(F32)] * len(lows)
    for mask in masks:
        xbs = [x.astype(BF16) for x in xs]
        xcs = [_dot(xb, (low * mask).astype(BF16)) for xb, low in zip(xbs, lows)]
        xs = [x - _dot(xc.astype(BF16), xb) for x, xc, xb in zip(xs, xcs, xbs)]
    return xs


def _gdn_kernel(x_ref, gz_ref, sm_ref, cw_ref, nw_ref, o_ref, state_ref, carry_ref):
    ci = pl.program_id(0)
    b = pl.program_id(1)
    C = GDN_CHUNK

    @pl.when(ci == 0)
    def _():
        state_ref[b] = jnp.zeros(state_ref.shape[1:], F32)
        carry_ref[b] = jnp.zeros(carry_ref.shape[1:], F32)

    x = x_ref[...]
    prev = carry_ref[b]
    row8 = lax.broadcasted_iota(jnp.int32, (8, 1), 0)
    y = cw_ref[CONV_W - 1:CONV_W, :] * x
    for j in range(1, CONV_W):
        xr = pltpu.roll(x, j, 0)
        pr = pltpu.roll(prev, j, 0)
        first = jnp.where(row8 < j, pr, xr[0:8, :])
        y = y + cw_ref[CONV_W - 1 - j:CONV_W - j, :] * jnp.concatenate([first, xr[8:, :]], axis=0)
    carry_ref[b] = x[C - 8:C, :]
    y = _silu(y)

    sm = sm_ref[...]
    r = lax.broadcasted_iota(jnp.int32, (C, C), 0)
    c = lax.broadcasted_iota(jnp.int32, (C, C), 1)
    causal = r >= c
    strict = r > c
    tri = jnp.where(causal, 1.0, 0.0).astype(F32)
    gcum = _dot(tri, sm, HIGHEST)
    gcum_t = gcum.T
    nw = nw_ref[...]
    gz = gz_ref[...]
    masks = _level_masks(C)

    heads = range(GDN_HEADS)
    qs, ks, vs, betas, g_cols, g_lasts, decays = [], [], [], [], [], [], []
    for h in heads:
        q = y[:, h * GDN_DK:(h + 1) * GDN_DK]
        k = y[:, GDN_KW + h * GDN_DK:GDN_KW + (h + 1) * GDN_DK]
        qs.append(q * lax.rsqrt(jnp.sum(q * q, axis=1, keepdims=True) + EPS) * (GDN_DK ** -0.5))
        ks.append(k * lax.rsqrt(jnp.sum(k * k, axis=1, keepdims=True) + EPS))
        vs.append(y[:, 2 * GDN_KW + h * GDN_DV:2 * GDN_KW + (h + 1) * GDN_DV])
        betas.append(sm[:, FOX_HEADS + h:FOX_HEADS + h + 1])
        gl = FOX_HEADS + GDN_HEADS + h
        g_cols.append(gcum[:, gl:gl + 1])
        g_lasts.append(gcum[C - 1:C, gl:gl + 1])
        diff = g_cols[h] - gcum_t[gl:gl + 1, :]
        decays.append(jnp.where(causal, jnp.exp(jnp.where(causal, diff, 0.0)), 0.0))
    kbs = [ks[h] * betas[h] for h in heads]
    egs = [jnp.exp(g_cols[h]) for h in heads]
    kqs = [_dot_nt(jnp.concatenate([kbs[h], qs[h]], axis=0).astype(BF16), ks[h].astype(BF16)) for h in heads]
    lows = [jnp.where(strict, kqs[h][:C] * decays[h], 0.0) for h in heads]
    attns = [(kqs[h][C:] * decays[h]).astype(BF16) for h in heads]
    tinvs = _unit_lower_inverses(lows, masks)
    uws = [_dot(tinvs[h].astype(BF16),
                jnp.concatenate([vs[h] * betas[h], kbs[h] * egs[h]], axis=1).astype(BF16)) for h in heads]
    sts = [state_ref[b, h] for h in heads]
    wqs = [_dot(jnp.concatenate([uws[h][:, GDN_DV:], qs[h] * egs[h]], axis=0).astype(BF16),
                sts[h].astype(BF16)) for h in heads]
    v_news = [(uws[h][:, :GDN_DV] - wqs[h][:C]).astype(BF16) for h in heads]
    k_dec_ts = [(ks[h] * jnp.exp(g_lasts[h] - g_cols[h])).T.astype(BF16) for h in heads]
    os_ = [wqs[h][C:] + _dot(attns[h], v_news[h]) for h in heads]
    new_sts = [sts[h] * jnp.exp(g_lasts[h]) + _dot(k_dec_ts[h], v_news[h]) for h in heads]
    for h in heads:
        state_ref[b, h] = new_sts[h]
        o = _rms(os_[h], nw) * _silu(gz[:, h * GDN_DV:(h + 1) * GDN_DV])
        o_ref[:, h * GDN_DV:(h + 1) * GDN_DV] = o.astype(o_ref.dtype)


def _gdn(gqkv, gz, small, conv_w, norm_w):
    B, S, _ = gqkv.shape
    C = GDN_CHUNK
    tile = lambda n: pl.BlockSpec((None, C, n), lambda ci, b: (b, ci, 0))
    full = lambda a: pl.BlockSpec(a.shape, lambda ci, b: (0,) * a.ndim)
    return pl.pallas_call(
        _gdn_kernel,
        grid=(S // C, B),
        in_specs=[tile(3 * GDN_KW), tile(GDN_VW), tile(LANES), full(conv_w), full(norm_w)],
        out_specs=tile(GDN_VW),
        out_shape=jax.ShapeDtypeStruct((B, S, GDN_VW), BF16),
        scratch_shapes=[pltpu.VMEM((B, GDN_HEADS, GDN_DK, GDN_DV), F32),
                        pltpu.VMEM((B, 8, 3 * GDN_KW), F32)],
        compiler_params=_params(("arbitrary", "arbitrary")),
        name="gdn",
    )(gqkv, gz, small, conv_w, norm_w)


def _outproj_kernel(ya_ref, yb_ref, gates_ref, x_ref, gt1_ref, sc2_ref, sh2_ref, gpm_ref, gpf_ref,
                    wa_ref, wb_ref, wo_ref, wq_ref, x1_ref, h2t_ref, qy_ref):
    ya = _dot(ya_ref[...], wa_ref[...])
    yb = _dot(yb_ref[...], wb_ref[...])
    gates = gates_ref[...]
    mix_in = gates[:, :D_MODEL] * ya + gates[:, D_MODEL:] * yb
    mix = _dot(mix_in.astype(BF16), wo_ref[...])
    x1 = x_ref[...] + gt1_ref[...] * _rms(mix, gpm_ref[...])
    x1_ref[...] = x1
    h2 = _rms(x1, gpf_ref[...]) * (1.0 + sc2_ref[...]) + sh2_ref[...]
    h2t_ref[...] = h2.T.astype(BF16)
    qy_ref[...] = _dot(h2.astype(BF16), wq_ref[...])


def _outproj(ya, yb, gates, x, gt1, sc2, sh2, gpm, gpf, wa, wb, wo, wq, tm):
    B, S, D = x.shape
    full = lambda a: pl.BlockSpec(a.shape, lambda b, i: (0,) * a.ndim)
    mod = pl.BlockSpec((None, 1, D), lambda b, i: (b, 0, 0))
    tile = lambda n: pl.BlockSpec((None, tm, n), lambda b, i: (b, i, 0))
    nq = wq.shape[1]
    return pl.pallas_call(
        _outproj_kernel,
        grid=(B, S // tm),
        in_specs=[tile(FOX_W), tile(GDN_VW), tile(2 * D), tile(D), mod, mod, mod, full(gpm), full(gpf),
                  full(wa), full(wb), full(wo), full(wq)],
        out_specs=[tile(D), pl.BlockSpec((D, tm), lambda b, i: (0, b * (S // tm) + i)), tile(nq)],
        out_shape=[jax.ShapeDtypeStruct((B, S, D), F32),
                   jax.ShapeDtypeStruct((D, B * S), BF16),
                   jax.ShapeDtypeStruct((B, S, nq), F32)],
        compiler_params=_params(("arbitrary", "arbitrary")),
        name="outproj",
    )(ya, yb, gates, x, gt1, sc2, sh2, gpm, gpf, wa, wb, wo, wq)


SUBLANES = 8


def _cmpx(a, i, l, desc):
    hi, lo = jnp.maximum(a[i], a[l]), jnp.minimum(a[i], a[l])
    a[i], a[l] = (hi, lo) if desc else (lo, hi)


def _top16_desc(a):
    a = list(a)
    n = len(a)
    k = 2
    while k <= n:
        j = k // 2
        while j >= 1:
            for i in range(n):
                l = i ^ j
                if l > i:
                    _cmpx(a, i, l, (i & k) == 0)
            j //= 2
        k *= 2
    shift = SUBLANES // 2
    while shift >= 1:
        other = [pltpu.roll(v, shift, 0) for v in a]
        a = [jnp.maximum(a[i], other[n - 1 - i]) for i in range(n)]
        j = n // 2
        while j >= 1:
            for i in range(n):
                l = i ^ j
                if l > i:
                    _cmpx(a, i, l, True)
            j //= 2
        shift //= 2
    return a


def _topk_kernel(qy_ref, k1_ref, k2_ref, s1_ref, s2_ref, tau_ref, m1_ref, m2_ref, zinv_ref):
    K = PEER_TOPK
    half = PEER_DQ // 2
    tt = qy_ref.shape[0]
    sub = lax.broadcasted_iota(jnp.int32, (SUBLANES, LANES), 0)

    def stagger(d, base):
        out = d[base]
        for r in range(1, SUBLANES):
            out = jnp.where(sub == r, d[base + r], out)
        return out

    def head(h, carry):
        col = pl.multiple_of(h * PEER_DQ, PEER_DQ)
        q1 = qy_ref[:, pl.ds(col, half)]
        q2 = qy_ref[:, pl.ds(col + half, half)]
        s1_ref[h] = _dot_nt(k1_ref[...], q1, HIGHEST)
        s2_ref[h] = _dot_nt(k2_ref[...], q2, HIGHEST)
        for c in range(tt // LANES):
            cs = slice(c * LANES, (c + 1) * LANES)
            rows = lambda ref: [ref[h, SUBLANES * i:SUBLANES * (i + 1), cs] for i in range(N_KEYS // SUBLANES)]
            d1 = _top16_desc(rows(s1_ref))
            d2 = _top16_desc(rows(s2_ref))
            d2_lo, d2_hi, d1_hi = stagger(d2, 0), stagger(d2, SUBLANES), stagger(d1, SUBLANES)
            cand = [d1[0] + d2_lo, d1[0] + d2_hi, d1_hi + d2[0]] + [d1[i] + d2_lo for i in range(1, SUBLANES)]
            cand += [jnp.full((SUBLANES, LANES), -jnp.inf, F32)] * (K - len(cand))
            top = _top16_desc(cand)
            z = jnp.ones_like(top[0])
            for i in range(1, K):
                z = z + jnp.exp(top[i] - top[0])
            tau_ref[h, :, cs] = top[K - 1][0:1, :]
            m1_ref[h, :, cs] = d1[0][0:1, :]
            m2_ref[h, :, cs] = d2[0][0:1, :]
            zinv_ref[h, :, cs] = (1.0 / z)[0:1, :]
        return carry

    lax.fori_loop(0, PEER_HEADS, head, 0)


def _topk(qy, keys1, keys2, tt):
    T, nq = qy.shape
    H = PEER_HEADS
    full = lambda a: pl.BlockSpec(a.shape, lambda i: (0,) * a.ndim)
    big = pl.BlockSpec((H, N_KEYS, tt), lambda i: (0, 0, i))
    stat = pl.BlockSpec((H, 1, tt), lambda i: (0, 0, i))
    return pl.pallas_call(
        _topk_kernel,
        grid=(T // tt,),
        in_specs=[pl.BlockSpec((tt, nq), lambda i: (i, 0)), full(keys1), full(keys2)],
        out_specs=[big, big, stat, stat, stat, stat],
        out_shape=[jax.ShapeDtypeStruct((H, N_KEYS, T), F32)] * 2 + [jax.ShapeDtypeStruct((H, 1, T), F32)] * 4,
        compiler_params=_params(("arbitrary",)),
        name="topk",
    )(qy, keys1, keys2)


def _peer_kernel(h2t_ref, u0_ref, ua_ref, ub_ref, vt_ref, s1_ref, s2_ref, tau_ref, m1_ref, m2_ref, zinv_ref,
                 x1_ref, gt2_ref, g_ref, o_ref, acc_ref, e2_ref, st0_ref, st1_ref, wt0_ref, wt1_ref, *, na):
    j = pl.program_id(1)
    H = PEER_HEADS
    tt = h2t_ref.shape[1]
    ec = na * N_KEYS
    G = N_KEYS // SUBLANES
    vreg = (SUBLANES, LANES)
    h2t = h2t_ref[...]

    @pl.when(j == 0)
    def _():
        acc_ref[...] = jnp.zeros(acc_ref.shape, F32)
        for h in range(H):
            e2_ref[h] = jnp.exp(s2_ref[h] - m2_ref[h])
        st0_ref[...] = _dot(u0_ref[...], h2t)

    def gate_stage(st_ref, wt_ref, half):
        for c in range(tt // LANES):
            cs = slice(c * LANES, (c + 1) * LANES)
            taus = [jnp.broadcast_to(tau_ref[h, :, cs], vreg) for h in range(H)]
            for al in range(na):
                rows = slice(al * N_KEYS, (al + 1) * N_KEYS)
                a = half * na + al
                wa = jnp.zeros((G,) + vreg, F32)
                for h in range(H):
                    s1row = s1_ref[h, a:a + 1, cs]
                    e1row = jnp.exp(s1row - m1_ref[h, :, cs]) * zinv_ref[h, :, cs]
                    s1b = jnp.broadcast_to(s1row, vreg)
                    e1b = jnp.broadcast_to(e1row, vreg)
                    s2 = s2_ref[h, :, cs].reshape((G,) + vreg)
                    e2 = e2_ref[h, :, cs].reshape((G,) + vreg)
                    wa = wa + jnp.where(s1b + s2 >= taus[h], e1b * e2, 0.0)
                sv = st_ref[rows, cs].reshape((G,) + vreg)
                act = 0.5 * sv * (1.0 + lax.erf(sv * (0.5 ** 0.5)))
                wt_ref[rows, cs] = (wa * act).reshape(N_KEYS, LANES).astype(BF16)

    st1_ref[...] = _dot(ua_ref[...], h2t)
    gate_stage(st0_ref, wt0_ref, 0)
    acc_ref[...] += _dot(vt_ref[:, :ec], wt0_ref[...])
    st0_ref[...] = _dot(ub_ref[...], h2t)
    gate_stage(st1_ref, wt1_ref, 1)
    acc_ref[...] += _dot(vt_ref[:, ec:], wt1_ref[...])

    @pl.when(j == pl.num_programs(1) - 1)
    def _():
        y = acc_ref[...].T
        o_ref[...] = x1_ref[...] + gt2_ref[...] * _rms(y, g_ref[...])


def _peer(h2t, u_b, vt_b, s1t, s2t, tau, m1, m2, zinv, x1, gt2, g_post, tt, ec, tiles_per_batch):
    D, T = h2t.shape
    E = u_b.shape[0]
    H = PEER_HEADS
    na = ec // N_KEYS
    nc = E // ec
    stat = pl.BlockSpec((H, 1, tt), lambda i, j: (0, 0, i))
    return pl.pallas_call(
        functools.partial(_peer_kernel, na=na),
        grid=(T // tt, nc // 2),
        in_specs=[pl.BlockSpec((D, tt), lambda i, j: (0, i)),
                  pl.BlockSpec((ec, D), lambda i, j: (0, 0)),
                  pl.BlockSpec((ec, D), lambda i, j: (2 * j + 1, 0)),
                  pl.BlockSpec((ec, D), lambda i, j: (jnp.minimum(2 * j + 2, nc - 1), 0)),
                  pl.BlockSpec((D, 2 * ec), lambda i, j: (0, j)),
                  pl.BlockSpec((H, 2 * na, tt), lambda i, j: (0, j, i)),
                  pl.BlockSpec((H, N_KEYS, tt), lambda i, j: (0, 0, i)),
                  stat, stat, stat, stat,
                  pl.BlockSpec((tt, D), lambda i, j: (i, 0)),
                  pl.BlockSpec((None, 1, D), lambda i, j: (i // tiles_per_batch, 0, 0)),
                  pl.BlockSpec(g_post.shape, lambda i, j: (0, 0))],
        out_specs=pl.BlockSpec((tt, D), lambda i, j: (i, 0)),
        out_shape=jax.ShapeDtypeStruct((T, D), F32),
        scratch_shapes=[pltpu.VMEM((D, tt), F32), pltpu.VMEM((H, N_KEYS, tt), F32),
                        pltpu.VMEM((ec, tt), F32), pltpu.VMEM((ec, tt), F32),
                        pltpu.VMEM((ec, tt), BF16), pltpu.VMEM((ec, tt), BF16)],
        compiler_params=_params(("arbitrary", "arbitrary")),
        name="peer",
    )(h2t, u_b, u_b, u_b, vt_b, s1t, s2t, tau, m1, m2, zinv, x1, gt2, g_post)


def _pick(n, pref):
    t = pref
    while n % t:
        t //= 2
    return t


def _layer(x, c_pad, w_ada, b_ada, g_pre_mix, g_post_mix, w_in, b_fgate, fox_w_o, conv_w, a_log, dt_bias,
           gdn_norm_w, gdn_w_o, b_branch_gate, w_out, g_pre_ffn, g_post_ffn, peer_w_query, peer_keys1,
           peer_keys2, peer_u, peer_v):
    B, S, D = x.shape
    T = B * S
    mod = _ada(c_pad, w_ada, b_ada.reshape(1, -1))[:B]
    sh1, sc1, gt1, sh2, sc2, gt2 = [m.reshape(B, 1, D) for m in jnp.split(mod, 6, axis=-1)]

    off = [0]
    for s in SPLITS:
        off.append(off[-1] + s)
    col = lambda i, j: w_in[:, off[i]:off[j]]
    wf = col(0, 3).astype(BF16)
    wg = jnp.concatenate([col(4, 8)], axis=1).astype(BF16)
    wgate = col(10, 12).astype(BF16)
    pad = LANES - (FOX_HEADS + 2 * GDN_HEADS)
    ws = jnp.concatenate([col(3, 4), col(8, 10), jnp.zeros((D, pad), F32)], axis=1).astype(BF16)
    zpad = jnp.zeros((pad,), F32)
    bsm = jnp.concatenate([b_fgate, jnp.zeros((GDN_HEADS,), F32), dt_bias, zpad]).reshape(1, LANES)
    alog_row = jnp.concatenate([jnp.zeros((FOX_HEADS + GDN_HEADS,), F32), a_log, zpad]).reshape(1, LANES)
    bgate = b_branch_gate.reshape(1, 2 * D)

    tm = _pick(S, 256)
    fqkv, gqkv, gz, gates, small = _inproj(x, sc1, sh1, g_pre_mix.reshape(1, D), wf, wg, wgate, ws, bgate,
                                           bsm, alog_row, tm)
    cum = _cumsum(small, _pick(S, 512))
    ya = _fox(fqkv, cum, _pick(S, 512))
    yb = _gdn(gqkv, gz, small, conv_w, gdn_norm_w.reshape(1, GDN_DV))
    x1, h2t, qy = _outproj(ya, yb, gates, x, gt1, sc2, sh2, g_post_mix.reshape(1, D), g_pre_ffn.reshape(1, D),
                          fox_w_o.astype(BF16), gdn_w_o.astype(BF16), w_out.astype(BF16),
                          peer_w_query.astype(BF16), tm)
    qy = qy.reshape(T, -1)
    s1t, s2t, tau, m1, m2, zinv = _topk(qy, peer_keys1, peer_keys2, _pick(T, 256))
    tt = _pick(S, 512)
    out = _peer(h2t, peer_u.astype(BF16), peer_v.T.astype(BF16), s1t, s2t, tau, m1, m2, zinv,
                x1.reshape(T, D), gt2, g_post_ffn.reshape(1, D), tt, 1024, S // tt)
    return out.reshape(B, S, D)


def kernel(x, c, w_ada, b_ada, g_pre_mix, g_post_mix, w_in, b_fgate, fox_w_o, conv_w, a_log, dt_bias,
           gdn_norm_w, gdn_w_o, b_branch_gate, w_out, g_pre_ffn, g_post_ffn, peer_w_query, peer_keys1,
           peer_keys2, peer_u, peer_v):
    B = x.shape[0]
    c_pad = jnp.pad(c, ((0, (-B) % 8), (0, 0)))
    for l in range(w_ada.shape[0]):
        x = _layer(x, c_pad, w_ada[l], b_ada[l], g_pre_mix[l], g_post_mix[l], w_in[l], b_fgate[l],
                   fox_w_o[l], conv_w[l], a_log[l], dt_bias[l], gdn_norm_w[l], gdn_w_o[l], b_branch_gate[l],
                   w_out[l], g_pre_ffn[l], g_post_ffn[l], peer_w_query[l], peer_keys1[l], peer_keys2[l],
                   peer_u[l], peer_v[l])
    return x
```

```python
import functools
import math

import jax
import jax.numpy as jnp
from jax import lax
from jax.experimental import pallas as pl
from jax.experimental.pallas import tpu as pltpu

F32 = jnp.float32
BF16 = jnp.bfloat16
HIGHEST = lax.Precision.HIGHEST

D_MODEL = 1024
EPS = 1e-6
FOX_HEADS = 8
FOX_DH = 64
GDN_HEADS = 4
GDN_DK = 128
GDN_DV = 128
CONV_W = 4
PEER_HEADS = 8
N_KEYS = 128
PEER_DQ = 256
PEER_TOPK = 16
FOX_W = FOX_HEADS * FOX_DH
GDN_KW = GDN_HEADS * GDN_DK
GDN_VW = GDN_HEADS * GDN_DV
SPLITS = (FOX_W, FOX_W, FOX_W, FOX_HEADS, GDN_KW, GDN_KW, GDN_VW, GDN_VW, GDN_HEADS, GDN_HEADS,
          D_MODEL, D_MODEL)

LANES = 128
GDN_CHUNK = 128
GATE_KB = 16
NEG_BIG = -1e30
LOG2E = 1.4426950408889634
VMEM_LIMIT = 56 * 1024 * 1024


def _params(sem, flags=None):
    return pltpu.CompilerParams(dimension_semantics=sem, vmem_limit_bytes=VMEM_LIMIT, flags=flags)


def _dot(a, b, precision=None):
    return jnp.dot(a, b, preferred_element_type=F32, precision=precision)


def _dot_nt(a, b, precision=None):
    return lax.dot_general(a, b, (((1,), (1,)), ((), ())), preferred_element_type=F32,
                           precision=precision)


def _sigmoid(x):
    return 1.0 / (1.0 + jnp.exp(-x))


def _silu(x):
    return x * _sigmoid(x)


def _softplus(x):
    return jnp.maximum(x, 0.0) + jnp.log(1.0 + jnp.exp(-jnp.abs(x)))


def _log_sigmoid(x):
    return -_softplus(-x)


def _rms(x, g):
    return x * lax.rsqrt(jnp.mean(x * x, axis=-1, keepdims=True) + EPS) * g


def _ada_kernel(c_ref, w_ref, b_ref, o_ref):
    c = c_ref[...]
    o_ref[...] = _dot(_silu(c), w_ref[...], HIGHEST) + b_ref[...]


def _ada(c_pad, w, b):
    n = w.shape[1]
    bn = 1024
    return pl.pallas_call(
        _ada_kernel,
        grid=(n // bn,),
        in_specs=[pl.BlockSpec(c_pad.shape, lambda j: (0, 0)),
                  pl.BlockSpec((D_MODEL, bn), lambda j: (0, j)),
                  pl.BlockSpec((1, bn), lambda j: (0, j))],
        out_specs=pl.BlockSpec((c_pad.shape[0], bn), lambda j: (0, j)),
        out_shape=jax.ShapeDtypeStruct((c_pad.shape[0], n), F32),
        compiler_params=_params(("arbitrary",)),
        name="ada",
    )(c_pad, w, b)


def _inproj_kernel(x_ref, sc_ref, sh_ref, g_ref, wf_ref, wg_ref, wgate_ref, ws_ref, bgate_ref,
                   bsm_ref, alog_ref, fqkv_ref, gqkv_ref, gz_ref, gates_ref, small_ref):
    x = x_ref[...]
    h = _rms(x, g_ref[...]) * (1.0 + sc_ref[...]) + sh_ref[...]
    hb = h.astype(BF16)
    pf = _dot(hb, wf_ref[...])
    lane = lax.broadcasted_iota(jnp.int32, (1, 3 * FOX_W), 1)
    pf = jnp.where(lane < FOX_W, pf * (FOX_DH ** -0.5 * LOG2E), pf)
    fqkv_ref[...] = pf.astype(BF16)
    pg = _dot(hb, wg_ref[...])
    gqkv_ref[...] = pg[:, :3 * GDN_KW]
    gz_ref[...] = pg[:, 3 * GDN_KW:]
    gates_ref[...] = _sigmoid(_dot(hb, wgate_ref[...]) + bgate_ref[...])
    z = _dot(hb, ws_ref[...]) + bsm_ref[...]
    sl = lax.broadcasted_iota(jnp.int32, (1, LANES), 1)
    neg_a = -jnp.exp(alog_ref[...])
    small_ref[...] = jnp.where(sl < FOX_HEADS, _log_sigmoid(z),
                               jnp.where(sl < FOX_HEADS + GDN_HEADS, _sigmoid(z), neg_a * _softplus(z)))


def _inproj(x, sc1, sh1, g_pre, wf, wg, wgate, ws, bgate, bsm, alog_row, tm):
    B, S, D = x.shape
    full = lambda a: pl.BlockSpec(a.shape, lambda b, i: (0,) * a.ndim)
    mod = pl.BlockSpec((None, 1, D), lambda b, i: (b, 0, 0))
    tile = lambda n: pl.BlockSpec((None, tm, n), lambda b, i: (b, i, 0))
    return pl.pallas_call(
        _inproj_kernel,
        grid=(B, S // tm),
        in_specs=[tile(D), mod, mod, full(g_pre), full(wf), full(wg), full(wgate), full(ws),
                  full(bgate), full(bsm), full(alog_row)],
        out_specs=[tile(3 * FOX_W), tile(3 * GDN_KW), tile(GDN_VW), tile(2 * D), tile(LANES)],
        out_shape=[jax.ShapeDtypeStruct((B, S, 3 * FOX_W), BF16),
                   jax.ShapeDtypeStruct((B, S, 3 * GDN_KW), F32),
                   jax.ShapeDtypeStruct((B, S, GDN_VW), F32),
                   jax.ShapeDtypeStruct((B, S, 2 * D), F32),
                   jax.ShapeDtypeStruct((B, S, LANES), F32)],
        compiler_params=_params(("arbitrary", "arbitrary")),
        name="inproj",
    )(x, sc1, sh1, g_pre, wf, wg, wgate, ws, bgate, bsm, alog_row)


def _cumsum_kernel(s_ref, o_ref, carry_ref):
    @pl.when(pl.program_id(1) == 0)
    def _():
        carry_ref[...] = jnp.zeros_like(carry_ref)

    v = s_ref[...]
    n = v.shape[0]
    r = lax.broadcasted_iota(jnp.int32, (n, n), 0)
    c = lax.broadcasted_iota(jnp.int32, (n, n), 1)
    tri = jnp.where(r >= c, 1.0, 0.0).astype(F32)
    cs = _dot(tri, v, HIGHEST) + carry_ref[...]
    o_ref[...] = cs
    carry_ref[...] = cs[n - 1:n, :]


def _cumsum(small, tc):
    B, S, L = small.shape
    return pl.pallas_call(
        _cumsum_kernel,
        grid=(B, S // tc),
        in_specs=[pl.BlockSpec((None, tc, L), lambda b, i: (b, i, 0))],
        out_specs=pl.BlockSpec((None, tc, L), lambda b, i: (b, i, 0)),
        out_shape=jax.ShapeDtypeStruct((B, S, L), F32),
        scratch_shapes=[pltpu.VMEM((1, L), F32)],
        compiler_params=_params(("arbitrary", "arbitrary")),
        name="cumsum",
    )(small)


def _split3(x):
    p1 = x.astype(BF16).astype(F32)
    r = x - p1
    p2 = r.astype(BF16).astype(F32)
    p3 = (r - p2).astype(BF16).astype(F32)
    return p1, p2, p3


def _fox_kernel(q_ref, k_ref, v_ref, cum_ref, o_ref, kmod_ref, vmod_ref, m_ref, acc_ref, *, tq):
    hp = pl.program_id(1)
    i = pl.program_id(2)
    S = k_ref.shape[0]
    lane = lax.broadcasted_iota(jnp.int32, (1, LANES), 1)

    def head_col(tile, h):
        return jnp.sum(jnp.where(lane == h, tile, 0.0), axis=1, keepdims=True)

    def with_bias(base, hh, pieces, ones_first):
        sp = (1 - hh) * FOX_DH
        in_head = (lane >= hh * FOX_DH) & (lane < (hh + 1) * FOX_DH)
        out = jnp.where(in_head, base, 0.0)
        po, oo = (3, 0) if ones_first else (0, 3)
        for t in range(3):
            out = jnp.where(lane == sp + po + t, pieces[t], out)
            out = jnp.where(lane == sp + oo + t, 1.0, out)
        return out.astype(BF16)

    @pl.when(i == 0)
    def _():
        def chunk(ci, carry):
            start = pl.multiple_of(ci * tq, tq)
            k2 = k_ref[pl.ds(start, tq), :].astype(F32)
            v2 = v_ref[pl.ds(start, tq), :].astype(F32)
            cum = cum_ref[pl.ds(start, tq), :]
            for hh in range(2):
                ck = head_col(cum, hp * 2 + hh)
                kmod_ref[hh, pl.ds(start, tq), :] = with_bias(k2, hh, _split3(-ck * LOG2E), True)
                in_head = (lane >= hh * FOX_DH) & (lane < (hh + 1) * FOX_DH)
                vm = jnp.where(in_head, v2, jnp.where(lane == (1 - hh) * FOX_DH, 1.0, 0.0))
                vmod_ref[hh, pl.ds(start, tq), :] = vm.astype(BF16)
            return carry
        lax.fori_loop(0, S // tq, chunk, 0)

    q2 = q_ref[...].astype(F32)
    cumq = cum_ref[pl.ds(pl.multiple_of(i * tq, tq), tq), :]
    qmods = [with_bias(q2, hh, _split3(head_col(cumq, hp * 2 + hh) * LOG2E), False) for hh in range(2)]
    rr = lax.broadcasted_iota(jnp.int32, (tq, tq), 0)
    cc = lax.broadcasted_iota(jnp.int32, (tq, tq), 1)
    m_ref[...] = jnp.full(m_ref.shape, NEG_BIG, F32)
    acc_ref[...] = jnp.zeros(acc_ref.shape, F32)

    def step(j, masked):
        start = pl.multiple_of(j * tq, tq)
        scores = [_dot_nt(qmods[hh], kmod_ref[hh, pl.ds(start, tq), :]) for hh in range(2)]
        for hh in range(2):
            s = scores[hh]
            if masked:
                s = jnp.where(rr >= cc, s, NEG_BIG)
            m_prev = m_ref[hh]
            m_new = jnp.maximum(m_prev, jnp.max(s, axis=1, keepdims=True))
            p = jnp.exp2(s - jnp.concatenate([m_new] * (tq // LANES), axis=1)).astype(BF16)
            acc_ref[hh] = jnp.exp2(m_prev - m_new) * acc_ref[hh] + _dot(p, vmod_ref[hh, pl.ds(start, tq), :])
            m_ref[hh] = m_new

    def body(j, carry):
        step(j, False)
        return carry

    lax.fori_loop(0, i, body, 0)
    step(i, True)
    outs = []
    for hh in range(2):
        acc = acc_ref[hh]
        outs.append(acc / head_col(acc, (1 - hh) * FOX_DH))
    o_ref[...] = jnp.where(lane < FOX_DH, outs[0], outs[1]).astype(o_ref.dtype)


def _fox(fqkv, cum, tq):
    B, S, _ = fqkv.shape
    nhp = FOX_HEADS // 2
    return pl.pallas_call(
        functools.partial(_fox_kernel, tq=tq),
        grid=(B, nhp, S // tq),
        in_specs=[pl.BlockSpec((None, tq, LANES), lambda b, hp, i: (b, i, hp)),
                  pl.BlockSpec((None, S, LANES), lambda b, hp, i: (b, 0, nhp + hp)),
                  pl.BlockSpec((None, S, LANES), lambda b, hp, i: (b, 0, 2 * nhp + hp)),
                  pl.BlockSpec((None, S, LANES), lambda b, hp, i: (b, 0, 0))],
        out_specs=pl.BlockSpec((None, tq, LANES), lambda b, hp, i: (b, i, hp)),
        out_shape=jax.ShapeDtypeStruct((B, S, FOX_W), BF16),
        scratch_shapes=[pltpu.VMEM((2, S, LANES), BF16), pltpu.VMEM((2, S, LANES), BF16),
                        pltpu.VMEM((2, tq, LANES), F32), pltpu.VMEM((2, tq, LANES), F32)],
        compiler_params=_params(("arbitrary", "arbitrary", "arbitrary")),
        name="fox",
    )(fqkv, fqkv, fqkv, cum)


def _level_masks(n):
    r = lax.broadcasted_iota(jnp.int32, (n, n), 0)
    c = lax.broadcasted_iota(jnp.int32, (n, n), 1)
    masks = []
    s = 1
    while s < n:
        sel = ((r // (2 * s)) == (c // (2 * s))) & (((r // s) % 2) == 1) & (((c // s) % 2) == 0)
        masks.append(jnp.where(sel, 1.0, 0.0).astype(F32))
        s *= 2
    return masks


def _unit_lower_inverses(lows, masks):
    n = lows[0].shape[0]
    r = lax.broadcasted_iota(jnp.int32, (n, n), 0)
    c = lax.broadcasted_iota(jnp.int32, (n, n), 1)
    xs = [jnp.where(r == c, 1.0, 0.0).astype(F32)] * len(lows)
    for mask in masks:
        xbs = [x.astype(BF16) for x in xs]
        xcs = [_dot(xb, (low * mask).astype(BF16)) for xb, low in zip(xbs, lows)]
        xs = [x - _dot(xc.astype(BF16), xb) for x, xc, xb in zip(xs, xcs, xbs)]
    return xs


def _gdn_kernel(x_ref, gz_ref, sm_ref, cw_ref, nw_ref, o_ref, state_ref, carry_ref):
    ci = pl.program_id(0)
    b = pl.program_id(1)
    C = GDN_CHUNK

    @pl.when(ci == 0)
    def _():
        state_ref[b] = jnp.zeros(state_ref.shape[1:], F32)
        carry_ref[b] = jnp.zeros(carry_ref.shape[1:], F32)

    x = x_ref[...]
    prev = carry_ref[b]
    row8 = lax.broadcasted_iota(jnp.int32, (8, 1), 0)
    y = cw_ref[CONV_W - 1:CONV_W, :] * x
    for j in range(1, CONV_W):
        xr = pltpu.roll(x, j, 0)
        pr = pltpu.roll(prev, j, 0)
        first = jnp.where(row8 < j, pr, xr[0:8, :])
        y = y + cw_ref[CONV_W - 1 - j:CONV_W - j, :] * jnp.concatenate([first, xr[8:, :]], axis=0)
    carry_ref[b] = x[C - 8:C, :]
    y = _silu(y)

    sm = sm_ref[...]
    r = lax.broadcasted_iota(jnp.int32, (C, C), 0)
    c = lax.broadcasted_iota(jnp.int32, (C, C), 1)
    causal = r >= c
    strict = r > c
    tri = jnp.where(causal, 1.0, 0.0).astype(F32)
    gcum = _dot(tri, sm, HIGHEST)
    gcum_t = gcum.T
    nw = nw_ref[...]
    gz = gz_ref[...]
    masks = _level_masks(C)

    heads = range(GDN_HEADS)
    qs, ks, vs, betas, g_cols, g_lasts, decays = [], [], [], [], [], [], []
    for h in heads:
        q = y[:, h * GDN_DK:(h + 1) * GDN_DK]
        k = y[:, GDN_KW + h * GDN_DK:GDN_KW + (h + 1) * GDN_DK]
        qs.append(q * lax.rsqrt(jnp.sum(q * q, axis=1, keepdims=True) + EPS) * (GDN_DK ** -0.5))
        ks.append(k * lax.rsqrt(jnp.sum(k * k, axis=1, keepdims=True) + EPS))
        vs.append(y[:, 2 * GDN_KW + h * GDN_DV:2 * GDN_KW + (h + 1) * GDN_DV])
        betas.append(sm[:, FOX_HEADS + h:FOX_HEADS + h + 1])
        gl = FOX_HEADS + GDN_HEADS + h
        g_cols.append(gcum[:, gl:gl + 1])
        g_lasts.append(gcum[C - 1:C, gl:gl + 1])
        diff = g_cols[h] - gcum_t[gl:gl + 1, :]
        decays.append(jnp.where(causal, jnp.exp(jnp.where(causal, diff, 0.0)), 0.0))
    kbs = [ks[h] * betas[h] for h in heads]
    egs = [jnp.exp(g_cols[h]) for h in heads]
    kqs = [_dot_nt(jnp.concatenate([kbs[h], qs[h]], axis=0).astype(BF16), ks[h].astype(BF16)) for h in heads]
    lows = [jnp.where(strict, kqs[h][:C] * decays[h], 0.0) for h in heads]
    attns = [(kqs[h][C:] * decays[h]).astype(BF16) for h in heads]
    tinvs = _unit_lower_inverses(lows, masks)
    uws = [_dot(tinvs[h].astype(BF16),
                jnp.concatenate([vs[h] * betas[h], kbs[h] * egs[h]], axis=1).astype(BF16)) for h in heads]
    sts = [state_ref[b, h] for h in heads]
    wqs = [_dot(jnp.concatenate([uws[h][:, GDN_DV:], qs[h] * egs[h]], axis=0).astype(BF16),
                sts[h].astype(BF16)) for h in heads]
    v_news = [(uws[h][:, :GDN_DV] - wqs[h][:C]).astype(BF16) for h in heads]
    k_dec_ts = [(ks[h] * jnp.exp(g_lasts[h] - g_cols[h])).T.astype(BF16) for h in heads]
    os_ = [wqs[h][C:] + _dot(attns[h], v_news[h]) for h in heads]
    new_sts = [sts[h] * jnp.exp(g_lasts[h]) + _dot(k_dec_ts[h], v_news[h]) for h in heads]
    for h in heads:
        state_ref[b, h] = new_sts[h]
        o = _rms(os_[h], nw) * _silu(gz[:, h * GDN_DV:(h + 1) * GDN_DV])
        o_ref[:, h * GDN_DV:(h + 1) * GDN_DV] = o.astype(o_ref.dtype)


def _gdn(gqkv, gz, small, conv_w, norm_w):
    B, S, _ = gqkv.shape
    C = GDN_CHUNK
    tile = lambda n: pl.BlockSpec((None, C, n), lambda ci, b: (b, ci, 0))
    full = lambda a: pl.BlockSpec(a.shape, lambda ci, b: (0,) * a.ndim)
    return pl.pallas_call(
        _gdn_kernel,
        grid=(S // C, B),
        in_specs=[tile(3 * GDN_KW), tile(GDN_VW), tile(LANES), full(conv_w), full(norm_w)],
        out_specs=tile(GDN_VW),
        out_shape=jax.ShapeDtypeStruct((B, S, GDN_VW), BF16),
        scratch_shapes=[pltpu.VMEM((B, GDN_HEADS, GDN_DK, GDN_DV), F32),
                        pltpu.VMEM((B, 8, 3 * GDN_KW), F32)],
        compiler_params=_params(("arbitrary", "arbitrary")),
        name="gdn",
    )(gqkv, gz, small, conv_w, norm_w)


def _outproj_kernel(ya_ref, yb_ref, gates_ref, x_ref, gt1_ref, sc2_ref, sh2_ref, gpm_ref, gpf_ref,
                    wa_ref, wb_ref, wo_ref, wq_ref, x1_ref, h2t_ref, qy_ref):
    ya = _dot(ya_ref[...], wa_ref[...])
    yb = _dot(yb_ref[...], wb_ref[...])
    gates = gates_ref[...]
    mix_in = gates[:, :D_MODEL] * ya + gates[:, D_MODEL:] * yb
    mix = _dot(mix_in.astype(BF16), wo_ref[...])
    x1 = x_ref[...] + gt1_ref[...] * _rms(mix, gpm_ref[...])
    x1_ref[...] = x1
    h2 = _rms(x1, gpf_ref[...]) * (1.0 + sc2_ref[...]) + sh2_ref[...]
    h2t_ref[...] = h2.T.astype(BF16)
    qy_ref[...] = _dot(h2.astype(BF16), wq_ref[...])


def _outproj(ya, yb, gates, x, gt1, sc2, sh2, gpm, gpf, wa, wb, wo, wq, tm):
    B, S, D = x.shape
    full = lambda a: pl.BlockSpec(a.shape, lambda b, i: (0,) * a.ndim)
    mod = pl.BlockSpec((None, 1, D), lambda b, i: (b, 0, 0))
    tile = lambda n: pl.BlockSpec((None, tm, n), lambda b, i: (b, i, 0))
    nq = wq.shape[1]
    return pl.pallas_call(
        _outproj_kernel,
        grid=(B, S // tm),
        in_specs=[tile(FOX_W), tile(GDN_VW), tile(2 * D), tile(D), mod, mod, mod, full(gpm), full(gpf),
                  full(wa), full(wb), full(wo), full(wq)],
        out_specs=[tile(D), pl.BlockSpec((D, tm), lambda b, i: (0, b * (S // tm) + i)), tile(nq)],
        out_shape=[jax.ShapeDtypeStruct((B, S, D), F32),
                   jax.ShapeDtypeStruct((D, B * S), BF16),
                   jax.ShapeDtypeStruct((B, S, nq), F32)],
        compiler_params=_params(("arbitrary", "arbitrary")),
        name="outproj",
    )(ya, yb, gates, x, gt1, sc2, sh2, gpm, gpf, wa, wb, wo, wq)


SUBLANES = 8


def _cmpx(a, i, l, desc):
    hi, lo = jnp.maximum(a[i], a[l]), jnp.minimum(a[i], a[l])
    a[i], a[l] = (hi, lo) if desc else (lo, hi)


def _top16_desc(a):
    a = list(a)
    n = len(a)
    k = 2
    while k <= n:
        j = k // 2
        while j >= 1:
            for i in range(n):
                l = i ^ j
                if l > i:
                    _cmpx(a, i, l, (i & k) == 0)
            j //= 2
        k *= 2
    shift = SUBLANES // 2
    while shift >= 1:
        other = [pltpu.roll(v, shift, 0) for v in a]
        a = [jnp.maximum(a[i], other[n - 1 - i]) for i in range(n)]
        j = n // 2
        while j >= 1:
            for i in range(n):
                l = i ^ j
                if l > i:
                    _cmpx(a, i, l, True)
            j //= 2
        shift //= 2
    return a


def _topk_kernel(qy_ref, k1_ref, k2_ref, theta_ref, e1_ref, s2_ref, m2_ref):
    K = PEER_TOPK
    half = PEER_DQ // 2
    tt = qy_ref.shape[0]
    G = N_KEYS // SUBLANES
    sub = lax.broadcasted_iota(jnp.int32, (SUBLANES, LANES), 0)

    def stagger(d, base):
        out = d[base]
        for r in range(1, SUBLANES):
            out = jnp.where(sub == r, d[base + r], out)
        return out

    def head(h, carry):
        col = pl.multiple_of(h * PEER_DQ, PEER_DQ)
        q1 = qy_ref[:, pl.ds(col, half)]
        q2 = qy_ref[:, pl.ds(col + half, half)]
        s1 = _dot_nt(k1_ref[...], q1, HIGHEST)
        s2 = _dot_nt(k2_ref[...], q2, HIGHEST)
        for c in range(tt // LANES):
            cs = slice(c * LANES, (c + 1) * LANES)
            s2_ref[h, c] = s2[:, cs]
            s1g = [s1[SUBLANES * g:SUBLANES * (g + 1), cs] for g in range(G)]
            d1 = _top16_desc(s1g)
            d2 = _top16_desc([s2[SUBLANES * g:SUBLANES * (g + 1), cs] for g in range(G)])
            d2_lo, d2_hi, d1_hi = stagger(d2, 0), stagger(d2, SUBLANES), stagger(d1, SUBLANES)
            cand = [d1[0] + d2_lo, d1[0] + d2_hi, d1_hi + d2[0]] + [d1[i] + d2_lo for i in range(1, SUBLANES)]
            cand += [jnp.full((SUBLANES, LANES), -jnp.inf, F32)] * (K - len(cand))
            top = _top16_desc(cand)
            z = jnp.ones_like(top[0])
            for i in range(1, K):
                z = z + jnp.exp(top[i] - top[0])
            zinv = 1.0 / z
            tau = top[K - 1]
            m2_ref[h, :, cs] = d2[0][0:1, :]
            for g in range(G):
                theta = jnp.full((SUBLANES, LANES), jnp.inf, F32)
                for j in range(K):
                    theta = jnp.where(s1g[g] + d2[j] >= tau, d2[j], theta)
                theta_ref[h, c, SUBLANES * g:SUBLANES * (g + 1), :] = theta
                e1_ref[h, c, SUBLANES * g:SUBLANES * (g + 1), :] = jnp.exp(s1g[g] - d1[0]) * zinv
        return carry

    lax.fori_loop(0, PEER_HEADS, head, 0)


def _topk(qy, keys1, keys2, tt):
    T, nq = qy.shape
    H = PEER_HEADS
    full = lambda a: pl.BlockSpec(a.shape, lambda i: (0,) * a.ndim)
    big = pl.BlockSpec((H, tt // LANES, N_KEYS, LANES), lambda i: (0, i, 0, 0))
    return pl.pallas_call(
        _topk_kernel,
        grid=(T // tt,),
        in_specs=[pl.BlockSpec((tt, nq), lambda i: (i, 0)), full(keys1), full(keys2)],
        out_specs=[big, big, big, pl.BlockSpec((H, 1, tt), lambda i: (0, 0, i))],
        out_shape=[jax.ShapeDtypeStruct((H, T // LANES, N_KEYS, LANES), F32)] * 3
        + [jax.ShapeDtypeStruct((H, 1, T), F32)],
        compiler_params=_params(("arbitrary",)),
        name="topk",
    )(qy, keys1, keys2)


def _peer_kernel(h2t_ref, u0_ref, ua_ref, ub_ref, vt_ref, theta_ref, e1_ref, s2_ref, m2_ref,
                 x1_ref, gt2_ref, g_ref, o_ref, acc_ref, e2_ref, st0_ref, st1_ref, wt0_ref, wt1_ref, *, na):
    j = pl.program_id(1)
    H = PEER_HEADS
    tt = h2t_ref.shape[1]
    ec = na * N_KEYS
    G = N_KEYS // SUBLANES
    vreg = (SUBLANES, LANES)
    h2t = h2t_ref[...]

    ncol = tt // LANES
    nkb = N_KEYS // GATE_KB
    blk = (GATE_KB // SUBLANES, SUBLANES, LANES)

    def put_scores(st_ref, st):
        for c in range(ncol):
            st_ref[c] = st[:, c * LANES:(c + 1) * LANES].reshape(na, nkb, GATE_KB, LANES)

    @pl.when(j == 0)
    def _():
        acc_ref[...] = jnp.zeros(acc_ref.shape, F32)
        for h in range(H):
            for c in range(ncol):
                e2_ref[h, c] = jnp.exp(s2_ref[h, c] - m2_ref[h, :, c * LANES:(c + 1) * LANES])
        put_scores(st0_ref, _dot(u0_ref[...], h2t))

    def gate_stage(st_ref, wt_ref, half):
        def block(i, carry):
            c = lax.shift_right_logical(i, nkb.bit_length() - 1)
            kb = lax.bitwise_and(i, nkb - 1)
            was = [jnp.zeros(blk, F32) for _ in range(na)]
            for h in range(H):
                s2 = s2_ref[h, c, kb].reshape(blk)
                e2 = e2_ref[h, c, kb].reshape(blk)
                for al in range(na):
                    a = half * na + al
                    thb = jnp.broadcast_to(theta_ref[h, c, a:a + 1, :], vreg)
                    e1b = jnp.broadcast_to(e1_ref[h, c, a:a + 1, :], vreg)
                    was[al] = was[al] + jnp.where(s2 >= thb, e1b * e2, 0.0)
            for al in range(na):
                sv = st_ref[c, al, kb].reshape(blk)
                act = 0.5 * sv * (1.0 + lax.erf(sv * (0.5 ** 0.5)))
                wt_ref[c, al, kb] = (was[al] * act).reshape(GATE_KB, LANES).astype(BF16)
            return carry

        lax.fori_loop(0, ncol * nkb, block, 0)

    def gated(wt_ref):
        return jnp.concatenate([wt_ref[c].reshape(ec, LANES) for c in range(ncol)], axis=1)

    put_scores(st1_ref, _dot(ua_ref[...], h2t))
    gate_stage(st0_ref, wt0_ref, 0)
    acc_ref[...] += _dot(vt_ref[:, :ec], gated(wt0_ref))
    put_scores(st0_ref, _dot(ub_ref[...], h2t))
    gate_stage(st1_ref, wt1_ref, 1)
    acc_ref[...] += _dot(vt_ref[:, ec:], gated(wt1_ref))

    @pl.when(j == pl.num_programs(1) - 1)
    def _():
        y = acc_ref[...].T
        o_ref[...] = x1_ref[...] + gt2_ref[...] * _rms(y, g_ref[...])


def _peer(h2t, u_b, vt_b, theta, e1, s2, m2, x1, gt2, g_post, tt, ec, tiles_per_batch):
    D, T = h2t.shape
    E = u_b.shape[0]
    H = PEER_HEADS
    na = ec // N_KEYS
    nc = E // ec
    ncol = tt // LANES
    nkb = N_KEYS // GATE_KB
    s2 = s2.reshape(H, T // LANES, nkb, GATE_KB, LANES)
    rows = pl.BlockSpec((H, ncol, 2 * na, LANES), lambda i, j: (0, i, j, 0))
    slab = (ncol, na, nkb, GATE_KB, LANES)
    return pl.pallas_call(
        functools.partial(_peer_kernel, na=na),
        grid=(T // tt, nc // 2),
        in_specs=[pl.BlockSpec((D, tt), lambda i, j: (0, i)),
                  pl.BlockSpec((ec, D), lambda i, j: (0, 0)),
                  pl.BlockSpec((ec, D), lambda i, j: (2 * j + 1, 0)),
                  pl.BlockSpec((ec, D), lambda i, j: (jnp.minimum(2 * j + 2, nc - 1), 0)),
                  pl.BlockSpec((D, 2 * ec), lambda i, j: (0, j)),
                  rows, rows,
                  pl.BlockSpec((H, ncol, nkb, GATE_KB, LANES), lambda i, j: (0, i, 0, 0, 0)),
                  pl.BlockSpec((H, 1, tt), lambda i, j: (0, 0, i)),
                  pl.BlockSpec((tt, D), lambda i, j: (i, 0)),
                  pl.BlockSpec((None, 1, D), lambda i, j: (i // tiles_per_batch, 0, 0)),
                  pl.BlockSpec(g_post.shape, lambda i, j: (0, 0))],
        out_specs=pl.BlockSpec((tt, D), lambda i, j: (i, 0)),
        out_shape=jax.ShapeDtypeStruct((T, D), F32),
        scratch_shapes=[pltpu.VMEM((D, tt), F32), pltpu.VMEM((H, ncol, nkb, GATE_KB, LANES), F32),
                        pltpu.VMEM(slab, F32), pltpu.VMEM(slab, F32),
                        pltpu.VMEM(slab, BF16), pltpu.VMEM(slab, BF16)],
        compiler_params=_params(("arbitrary", "arbitrary")),
        name="peer",
    )(h2t, u_b, u_b, u_b, vt_b, theta, e1, s2, m2, x1, gt2, g_post)


def _pick(n, pref):
    t = pref
    while n % t:
        t //= 2
    return t


def _layer(x, c_pad, w_ada, b_ada, g_pre_mix, g_post_mix, w_in, b_fgate, fox_w_o, conv_w, a_log, dt_bias,
           gdn_norm_w, gdn_w_o, b_branch_gate, w_out, g_pre_ffn, g_post_ffn, peer_w_query, peer_keys1,
           peer_keys2, peer_u, peer_v):
    B, S, D = x.shape
    T = B * S
    mod = _ada(c_pad, w_ada, b_ada.reshape(1, -1))[:B]
    sh1, sc1, gt1, sh2, sc2, gt2 = [m.reshape(B, 1, D) for m in jnp.split(mod, 6, axis=-1)]

    off = [0]
    for s in SPLITS:
        off.append(off[-1] + s)
    col = lambda i, j: w_in[:, off[i]:off[j]]
    wf = col(0, 3).astype(BF16)
    wg = jnp.concatenate([col(4, 8)], axis=1).astype(BF16)
    wgate = col(10, 12).astype(BF16)
    pad = LANES - (FOX_HEADS + 2 * GDN_HEADS)
    ws = jnp.concatenate([col(3, 4), col(8, 10), jnp.zeros((D, pad), F32)], axis=1).astype(BF16)
    zpad = jnp.zeros((pad,), F32)
    bsm = jnp.concatenate([b_fgate, jnp.zeros((GDN_HEADS,), F32), dt_bias, zpad]).reshape(1, LANES)
    alog_row = jnp.concatenate([jnp.zeros((FOX_HEADS + GDN_HEADS,), F32), a_log, zpad]).reshape(1, LANES)
    bgate = b_branch_gate.reshape(1, 2 * D)

    tm = _pick(S, 256)
    fqkv, gqkv, gz, gates, small = _inproj(x, sc1, sh1, g_pre_mix.reshape(1, D), wf, wg, wgate, ws, bgate,
                                           bsm, alog_row, tm)
    cum = _cumsum(small, _pick(S, 512))
    ya = _fox(fqkv, cum, _pick(S, 512))
    yb = _gdn(gqkv, gz, small, conv_w, gdn_norm_w.reshape(1, GDN_DV))
    x1, h2t, qy = _outproj(ya, yb, gates, x, gt1, sc2, sh2, g_post_mix.reshape(1, D), g_pre_ffn.reshape(1, D),
                          fox_w_o.astype(BF16), gdn_w_o.astype(BF16), w_out.astype(BF16),
                          peer_w_query.astype(BF16), tm)
    qy = qy.reshape(T, -1)
    theta, e1, s2, m2 = _topk(qy, peer_keys1, peer_keys2, _pick(T, 256))
    tt = _pick(S, 512)
    out = _peer(h2t, peer_u.astype(BF16), peer_v.T.astype(BF16), theta, e1, s2, m2,
                x1.reshape(T, D), gt2, g_post_ffn.reshape(1, D), tt, 1024, S // tt)
    return out.reshape(B, S, D)


def kernel(x, c, w_ada, b_ada, g_pre_mix, g_post_mix, w_in, b_fgate, fox_w_o, conv_w, a_log, dt_bias,
           gdn_norm_w, gdn_w_o, b_branch_gate, w_out, g_pre_ffn, g_post_ffn, peer_w_query, peer_keys1,
           peer_keys2, peer_u, peer_v):
    B = x.shape[0]
    c_pad = jnp.pad(c, ((0, (-B) % 8), (0, 0)))
    for l in range(w_ada.shape[0]):
        x = _layer(x, c_pad, w_ada[l], b_ada[l], g_pre_mix[l], g_post_mix[l], w_in[l], b_fgate[l],
                   fox_w_o[l], conv_w[l], a_log[l], dt_bias[l], gdn_norm_w[l], gdn_w_o[l], b_branch_gate[l],
                   w_out[l], g_pre_ffn[l], g_post_ffn[l], peer_w_query[l], peer_keys1[l], peer_keys2[l],
                   peer_u[l], peer_v[l])
    return x
```

```python
import functools
import math

import jax
import jax.numpy as jnp
from jax import lax
from jax.experimental import pallas as pl
from jax.experimental.pallas import tpu as pltpu

F32 = jnp.float32
BF16 = jnp.bfloat16
HIGHEST = lax.Precision.HIGHEST

D_MODEL = 1024
EPS = 1e-6
FOX_HEADS = 8
FOX_DH = 64
GDN_HEADS = 4
GDN_DK = 128
GDN_DV = 128
CONV_W = 4
PEER_HEADS = 8
N_KEYS = 128
PEER_DQ = 256
PEER_TOPK = 16
FOX_W = FOX_HEADS * FOX_DH
GDN_KW = GDN_HEADS * GDN_DK
GDN_VW = GDN_HEADS * GDN_DV
SPLITS = (FOX_W, FOX_W, FOX_W, FOX_HEADS, GDN_KW, GDN_KW, GDN_VW, GDN_VW, GDN_HEADS, GDN_HEADS,
          D_MODEL, D_MODEL)

LANES = 128
GDN_CHUNK = 128
FOX_ROW_SPLIT = 1
GATE_KB = 16
GATE_AB = 8
NEG_BIG = -1e30
LOG2E = 1.4426950408889634
VMEM_LIMIT = 56 * 1024 * 1024


def _params(sem, flags=None):
    return pltpu.CompilerParams(dimension_semantics=sem, vmem_limit_bytes=VMEM_LIMIT, flags=flags)


def _dot(a, b, precision=None):
    return jnp.dot(a, b, preferred_element_type=F32, precision=precision)


def _dot_nt(a, b, precision=None):
    return lax.dot_general(a, b, (((1,), (1,)), ((), ())), preferred_element_type=F32,
                           precision=precision)


def _sigmoid(x):
    return 1.0 / (1.0 + jnp.exp(-x))


def _silu(x):
    return x * _sigmoid(x)


def _softplus(x):
    return jnp.maximum(x, 0.0) + jnp.log(1.0 + jnp.exp(-jnp.abs(x)))


def _log_sigmoid(x):
    return -_softplus(-x)


def _rms(x, g):
    return x * lax.rsqrt(jnp.mean(x * x, axis=-1, keepdims=True) + EPS) * g


def _ada_kernel(c_ref, w_ref, b_ref, o_ref):
    c = c_ref[...]
    o_ref[...] = _dot(_silu(c), w_ref[...], HIGHEST) + b_ref[...]


def _ada(c_pad, w, b):
    n = w.shape[1]
    bn = 1024
    return pl.pallas_call(
        _ada_kernel,
        grid=(n // bn,),
        in_specs=[pl.BlockSpec(c_pad.shape, lambda j: (0, 0)),
                  pl.BlockSpec((D_MODEL, bn), lambda j: (0, j)),
                  pl.BlockSpec((1, bn), lambda j: (0, j))],
        out_specs=pl.BlockSpec((c_pad.shape[0], bn), lambda j: (0, j)),
        out_shape=jax.ShapeDtypeStruct((c_pad.shape[0], n), F32),
        compiler_params=_params(("arbitrary",)),
        name="ada",
    )(c_pad, w, b)


def _inproj_kernel(x_ref, sc_ref, sh_ref, g_ref, wf_ref, wg_ref, wgate_ref, ws_ref, bgate_ref,
                   bsm_ref, alog_ref, fqkv_ref, gqkv_ref, gz_ref, gates_ref, small_ref):
    x = x_ref[...]
    h = _rms(x, g_ref[...]) * (1.0 + sc_ref[...]) + sh_ref[...]
    hb = h.astype(BF16)
    pf = _dot(hb, wf_ref[...])
    lane = lax.broadcasted_iota(jnp.int32, (1, 3 * FOX_W), 1)
    pf = jnp.where(lane < FOX_W, pf * (FOX_DH ** -0.5 * LOG2E), pf)
    fqkv_ref[...] = pf.astype(BF16)
    pg = _dot(hb, wg_ref[...])
    gqkv_ref[...] = pg[:, :3 * GDN_KW]
    gz_ref[...] = pg[:, 3 * GDN_KW:]
    gates_ref[...] = _sigmoid(_dot(hb, wgate_ref[...]) + bgate_ref[...])
    z = _dot(hb, ws_ref[...]) + bsm_ref[...]
    sl = lax.broadcasted_iota(jnp.int32, (1, LANES), 1)
    neg_a = -jnp.exp(alog_ref[...])
    small_ref[...] = jnp.where(sl < FOX_HEADS, _log_sigmoid(z),
                               jnp.where(sl < FOX_HEADS + GDN_HEADS, _sigmoid(z), neg_a * _softplus(z)))


def _inproj(x, sc1, sh1, g_pre, wf, wg, wgate, ws, bgate, bsm, alog_row, tm):
    B, S, D = x.shape
    full = lambda a: pl.BlockSpec(a.shape, lambda b, i: (0,) * a.ndim)
    mod = pl.BlockSpec((None, 1, D), lambda b, i: (b, 0, 0))
    tile = lambda n: pl.BlockSpec((None, tm, n), lambda b, i: (b, i, 0))
    return pl.pallas_call(
        _inproj_kernel,
        grid=(B, S // tm),
        in_specs=[tile(D), mod, mod, full(g_pre), full(wf), full(wg), full(wgate), full(ws),
                  full(bgate), full(bsm), full(alog_row)],
        out_specs=[tile(3 * FOX_W), tile(3 * GDN_KW), tile(GDN_VW), tile(2 * D), tile(LANES)],
        out_shape=[jax.ShapeDtypeStruct((B, S, 3 * FOX_W), BF16),
                   jax.ShapeDtypeStruct((B, S, 3 * GDN_KW), F32),
                   jax.ShapeDtypeStruct((B, S, GDN_VW), F32),
                   jax.ShapeDtypeStruct((B, S, 2 * D), F32),
                   jax.ShapeDtypeStruct((B, S, LANES), F32)],
        compiler_params=_params(("arbitrary", "arbitrary")),
        name="inproj",
    )(x, sc1, sh1, g_pre, wf, wg, wgate, ws, bgate, bsm, alog_row)


def _cumsum_kernel(s_ref, o_ref, carry_ref):
    @pl.when(pl.program_id(1) == 0)
    def _():
        carry_ref[...] = jnp.zeros_like(carry_ref)

    v = s_ref[...]
    n = v.shape[0]
    r = lax.broadcasted_iota(jnp.int32, (n, n), 0)
    c = lax.broadcasted_iota(jnp.int32, (n, n), 1)
    tri = jnp.where(r >= c, 1.0, 0.0).astype(F32)
    cs = _dot(tri, v, HIGHEST) + carry_ref[...]
    o_ref[...] = cs
    carry_ref[...] = cs[n - 1:n, :]


def _cumsum(small, tc):
    B, S, L = small.shape
    return pl.pallas_call(
        _cumsum_kernel,
        grid=(B, S // tc),
        in_specs=[pl.BlockSpec((None, tc, L), lambda b, i: (b, i, 0))],
        out_specs=pl.BlockSpec((None, tc, L), lambda b, i: (b, i, 0)),
        out_shape=jax.ShapeDtypeStruct((B, S, L), F32),
        scratch_shapes=[pltpu.VMEM((1, L), F32)],
        compiler_params=_params(("arbitrary", "arbitrary")),
        name="cumsum",
    )(small)


def _split3(x):
    p1 = x.astype(BF16).astype(F32)
    r = x - p1
    p2 = r.astype(BF16).astype(F32)
    p3 = (r - p2).astype(BF16).astype(F32)
    return p1, p2, p3


def _fox_kernel(q_ref, k_ref, v_ref, cum_ref, o_ref, kmod_ref, vmod_ref, m_ref, acc_ref, *, tq):
    hp = pl.program_id(1)
    i = pl.program_id(2)
    S = k_ref.shape[0]
    lane = lax.broadcasted_iota(jnp.int32, (1, LANES), 1)

    def head_col(tile, h):
        return jnp.sum(jnp.where(lane == h, tile, 0.0), axis=1, keepdims=True)

    def with_bias(base, hh, pieces, ones_first):
        sp = (1 - hh) * FOX_DH
        in_head = (lane >= hh * FOX_DH) & (lane < (hh + 1) * FOX_DH)
        out = jnp.where(in_head, base, 0.0)
        po, oo = (3, 0) if ones_first else (0, 3)
        for t in range(3):
            out = jnp.where(lane == sp + po + t, pieces[t], out)
            out = jnp.where(lane == sp + oo + t, 1.0, out)
        return out.astype(BF16)

    @pl.when(i == 0)
    def _():
        def chunk(ci, carry):
            start = pl.multiple_of(ci * tq, tq)
            k2 = k_ref[pl.ds(start, tq), :].astype(F32)
            v2 = v_ref[pl.ds(start, tq), :].astype(F32)
            cum = cum_ref[pl.ds(start, tq), :]
            for hh in range(2):
                ck = head_col(cum, hp * 2 + hh)
                kmod_ref[hh, pl.ds(start, tq), :] = with_bias(k2, hh, _split3(-ck * LOG2E), True)
                in_head = (lane >= hh * FOX_DH) & (lane < (hh + 1) * FOX_DH)
                vm = jnp.where(in_head, v2, jnp.where(lane == (1 - hh) * FOX_DH, 1.0, 0.0))
                vmod_ref[hh, pl.ds(start, tq), :] = vm.astype(BF16)
            return carry
        lax.fori_loop(0, S // tq, chunk, 0)

    q2 = q_ref[...].astype(F32)
    cumq = cum_ref[pl.ds(pl.multiple_of(i * tq, tq), tq), :]
    qmods = [with_bias(q2, hh, _split3(head_col(cumq, hp * 2 + hh) * LOG2E), False) for hh in range(2)]
    rr = lax.broadcasted_iota(jnp.int32, (tq // FOX_ROW_SPLIT, tq), 0)
    cc = lax.broadcasted_iota(jnp.int32, (tq // FOX_ROW_SPLIT, tq), 1)
    m_ref[...] = jnp.full(m_ref.shape, NEG_BIG, F32)
    acc_ref[...] = jnp.zeros(acc_ref.shape, F32)

    RS = FOX_ROW_SPLIT
    tr = tq // RS

    def step(j, masked):
        start = pl.multiple_of(j * tq, tq)
        chains = [(hh, r) for hh in range(2) for r in range(RS)]
        scores = [_dot_nt(qmods[hh][r * tr:(r + 1) * tr], kmod_ref[hh, pl.ds(start, tq), :]) for hh, r in chains]
        for (hh, r), s in zip(chains, scores):
            rows = slice(r * tr, (r + 1) * tr)
            if masked:
                s = jnp.where(rr + r * tr >= cc, s, NEG_BIG)
            m_prev = m_ref[hh, rows, :]
            m_new = jnp.maximum(m_prev, jnp.max(s, axis=1, keepdims=True))
            p = jnp.exp2(s - jnp.concatenate([m_new] * (tq // LANES), axis=1)).astype(BF16)
            acc_ref[hh, rows, :] = (jnp.exp2(m_prev - m_new) * acc_ref[hh, rows, :]
                                    + _dot(p, vmod_ref[hh, pl.ds(start, tq), :]))
            m_ref[hh, rows, :] = m_new

    def body(j, carry):
        step(j, False)
        return carry

    lax.fori_loop(0, i, body, 0)
    step(i, True)
    outs = []
    for hh in range(2):
        acc = acc_ref[hh]
        outs.append(acc / head_col(acc, (1 - hh) * FOX_DH))
    o_ref[...] = jnp.where(lane < FOX_DH, outs[0], outs[1]).astype(o_ref.dtype)


def _fox(fqkv, cum, tq):
    B, S, _ = fqkv.shape
    nhp = FOX_HEADS // 2
    return pl.pallas_call(
        functools.partial(_fox_kernel, tq=tq),
        grid=(B, nhp, S // tq),
        in_specs=[pl.BlockSpec((None, tq, LANES), lambda b, hp, i: (b, i, hp)),
                  pl.BlockSpec((None, S, LANES), lambda b, hp, i: (b, 0, nhp + hp)),
                  pl.BlockSpec((None, S, LANES), lambda b, hp, i: (b, 0, 2 * nhp + hp)),
                  pl.BlockSpec((None, S, LANES), lambda b, hp, i: (b, 0, 0))],
        out_specs=pl.BlockSpec((None, tq, LANES), lambda b, hp, i: (b, i, hp)),
        out_shape=jax.ShapeDtypeStruct((B, S, FOX_W), BF16),
        scratch_shapes=[pltpu.VMEM((2, S, LANES), BF16), pltpu.VMEM((2, S, LANES), BF16),
                        pltpu.VMEM((2, tq, LANES), F32), pltpu.VMEM((2, tq, LANES), F32)],
        compiler_params=_params(("arbitrary", "arbitrary", "arbitrary")),
        name="fox",
    )(fqkv, fqkv, fqkv, cum)


def _level_masks(n):
    r = lax.broadcasted_iota(jnp.int32, (n, n), 0)
    c = lax.broadcasted_iota(jnp.int32, (n, n), 1)
    masks = []
    s = 1
    while s < n:
        sel = ((r // (2 * s)) == (c // (2 * s))) & (((r // s) % 2) == 1) & (((c // s) % 2) == 0)
        masks.append(jnp.where(sel, 1.0, 0.0).astype(F32))
        s *= 2
    return masks


def _unit_lower_inverses(lows, masks):
    n = lows[0].shape[0]
    r = lax.broadcasted_iota(jnp.int32, (n, n), 0)
    c = lax.broadcasted_iota(jnp.int32, (n, n), 1)
    xs = [jnp.where(r == c, 1.0, 0.0).astype(F32)] * len(lows)
    for mask in masks:
        xbs = [x.astype(BF16) for x in xs]
        xcs = [_dot(xb, (low * mask).astype(BF16)) for xb, low in zip(xbs, lows)]
        xs = [x - _dot(xc.astype(BF16), xb) for x, xc, xb in zip(xs, xcs, xbs)]
    return xs


def _gdn_kernel(x_ref, gz_ref, sm_ref, cw_ref, nw_ref, o_ref, state_ref, carry_ref):
    ci = pl.program_id(0)
    b = pl.program_id(1)
    C = GDN_CHUNK

    @pl.when(ci == 0)
    def _():
        state_ref[b] = jnp.zeros(state_ref.shape[1:], F32)
        carry_ref[b] = jnp.zeros(carry_ref.shape[1:], F32)

    x = x_ref[...]
    prev = carry_ref[b]
    row8 = lax.broadcasted_iota(jnp.int32, (8, 1), 0)
    y = cw_ref[CONV_W - 1:CONV_W, :] * x
    for j in range(1, CONV_W):
        xr = pltpu.roll(x, j, 0)
        pr = pltpu.roll(prev, j, 0)
        first = jnp.where(row8 < j, pr, xr[0:8, :])
        y = y + cw_ref[CONV_W - 1 - j:CONV_W - j, :] * jnp.concatenate([first, xr[8:, :]], axis=0)
    carry_ref[b] = x[C - 8:C, :]
    y = _silu(y)

    sm = sm_ref[...]
    r = lax.broadcasted_iota(jnp.int32, (C, C), 0)
    c = lax.broadcasted_iota(jnp.int32, (C, C), 1)
    causal = r >= c
    strict = r > c
    tri = jnp.where(causal, 1.0, 0.0).astype(F32)
    gcum = _dot(tri, sm, HIGHEST)
    gcum_t = gcum.T
    nw = nw_ref[...]
    gz = gz_ref[...]
    masks = _level_masks(C)

    heads = range(GDN_HEADS)
    qs, ks, vs, betas, g_cols, g_lasts, decays = [], [], [], [], [], [], []
    for h in heads:
        q = y[:, h * GDN_DK:(h + 1) * GDN_DK]
        k = y[:, GDN_KW + h * GDN_DK:GDN_KW + (h + 1) * GDN_DK]
        qs.append(q * lax.rsqrt(jnp.sum(q * q, axis=1, keepdims=True) + EPS) * (GDN_DK ** -0.5))
        ks.append(k * lax.rsqrt(jnp.sum(k * k, axis=1, keepdims=True) + EPS))
        vs.append(y[:, 2 * GDN_KW + h * GDN_DV:2 * GDN_KW + (h + 1) * GDN_DV])
        betas.append(sm[:, FOX_HEADS + h:FOX_HEADS + h + 1])
        gl = FOX_HEADS + GDN_HEADS + h
        g_cols.append(gcum[:, gl:gl + 1])
        g_lasts.append(gcum[C - 1:C, gl:gl + 1])
        diff = g_cols[h] - gcum_t[gl:gl + 1, :]
        decays.append(jnp.where(causal, jnp.exp(jnp.where(causal, diff, 0.0)), 0.0))
    kbs = [ks[h] * betas[h] for h in heads]
    egs = [jnp.exp(g_cols[h]) for h in heads]
    kqs = [_dot_nt(jnp.concatenate([kbs[h], qs[h]], axis=0).astype(BF16), ks[h].astype(BF16)) for h in heads]
    lows = [jnp.where(strict, kqs[h][:C] * decays[h], 0.0) for h in heads]
    attns = [(kqs[h][C:] * decays[h]).astype(BF16) for h in heads]
    tinvs = _unit_lower_inverses(lows, masks)
    uws = [_dot(tinvs[h].astype(BF16),
                jnp.concatenate([vs[h] * betas[h], kbs[h] * egs[h]], axis=1).astype(BF16)) for h in heads]
    sts = [state_ref[b, h] for h in heads]
    wqs = [_dot(jnp.concatenate([uws[h][:, GDN_DV:], qs[h] * egs[h]], axis=0).astype(BF16),
                sts[h].astype(BF16)) for h in heads]
    v_news = [(uws[h][:, :GDN_DV] - wqs[h][:C]).astype(BF16) for h in heads]
    k_dec_ts = [(ks[h] * jnp.exp(g_lasts[h] - g_cols[h])).T.astype(BF16) for h in heads]
    os_ = [wqs[h][C:] + _dot(attns[h], v_news[h]) for h in heads]
    new_sts = [sts[h] * jnp.exp(g_lasts[h]) + _dot(k_dec_ts[h], v_news[h]) for h in heads]
    for h in heads:
        state_ref[b, h] = new_sts[h]
        o = _rms(os_[h], nw) * _silu(gz[:, h * GDN_DV:(h + 1) * GDN_DV])
        o_ref[:, h * GDN_DV:(h + 1) * GDN_DV] = o.astype(o_ref.dtype)


def _gdn(gqkv, gz, small, conv_w, norm_w):
    B, S, _ = gqkv.shape
    C = GDN_CHUNK
    tile = lambda n: pl.BlockSpec((None, C, n), lambda ci, b: (b, ci, 0))
    full = lambda a: pl.BlockSpec(a.shape, lambda ci, b: (0,) * a.ndim)
    return pl.pallas_call(
        _gdn_kernel,
        grid=(S // C, B),
        in_specs=[tile(3 * GDN_KW), tile(GDN_VW), tile(LANES), full(conv_w), full(norm_w)],
        out_specs=tile(GDN_VW),
        out_shape=jax.ShapeDtypeStruct((B, S, GDN_VW), BF16),
        scratch_shapes=[pltpu.VMEM((B, GDN_HEADS, GDN_DK, GDN_DV), F32),
                        pltpu.VMEM((B, 8, 3 * GDN_KW), F32)],
        compiler_params=_params(("arbitrary", "arbitrary")),
        name="gdn",
    )(gqkv, gz, small, conv_w, norm_w)


def _outproj_kernel(ya_ref, yb_ref, gates_ref, x_ref, gt1_ref, sc2_ref, sh2_ref, gpm_ref, gpf_ref,
                    wa_ref, wb_ref, wo_ref, wq_ref, x1_ref, h2t_ref, qy_ref):
    ya = _dot(ya_ref[...], wa_ref[...])
    yb = _dot(yb_ref[...], wb_ref[...])
    gates = gates_ref[...]
    mix_in = gates[:, :D_MODEL] * ya + gates[:, D_MODEL:] * yb
    mix = _dot(mix_in.astype(BF16), wo_ref[...])
    x1 = x_ref[...] + gt1_ref[...] * _rms(mix, gpm_ref[...])
    x1_ref[...] = x1
    h2 = _rms(x1, gpf_ref[...]) * (1.0 + sc2_ref[...]) + sh2_ref[...]
    h2t_ref[...] = h2.T.astype(BF16)
    qy_ref[...] = _dot(h2.astype(BF16), wq_ref[...])


def _outproj(ya, yb, gates, x, gt1, sc2, sh2, gpm, gpf, wa, wb, wo, wq, tm):
    B, S, D = x.shape
    full = lambda a: pl.BlockSpec(a.shape, lambda b, i: (0,) * a.ndim)
    mod = pl.BlockSpec((None, 1, D), lambda b, i: (b, 0, 0))
    tile = lambda n: pl.BlockSpec((None, tm, n), lambda b, i: (b, i, 0))
    nq = wq.shape[1]
    return pl.pallas_call(
        _outproj_kernel,
        grid=(B, S // tm),
        in_specs=[tile(FOX_W), tile(GDN_VW), tile(2 * D), tile(D), mod, mod, mod, full(gpm), full(gpf),
                  full(wa), full(wb), full(wo), full(wq)],
        out_specs=[tile(D), pl.BlockSpec((D, tm), lambda b, i: (0, b * (S // tm) + i)), tile(nq)],
        out_shape=[jax.ShapeDtypeStruct((B, S, D), F32),
                   jax.ShapeDtypeStruct((D, B * S), BF16),
                   jax.ShapeDtypeStruct((B, S, nq), F32)],
        compiler_params=_params(("arbitrary", "arbitrary")),
        name="outproj",
    )(ya, yb, gates, x, gt1, sc2, sh2, gpm, gpf, wa, wb, wo, wq)


SUBLANES = 8


def _cmpx(a, i, l, desc):
    hi, lo = jnp.maximum(a[i], a[l]), jnp.minimum(a[i], a[l])
    a[i], a[l] = (hi, lo) if desc else (lo, hi)


def _top16_desc(a):
    a = list(a)
    n = len(a)
    k = 2
    while k <= n:
        j = k // 2
        while j >= 1:
            for i in range(n):
                l = i ^ j
                if l > i:
                    _cmpx(a, i, l, (i & k) == 0)
            j //= 2
        k *= 2
    shift = SUBLANES // 2
    while shift >= 1:
        other = [pltpu.roll(v, shift, 0) for v in a]
        a = [jnp.maximum(a[i], other[n - 1 - i]) for i in range(n)]
        j = n // 2
        while j >= 1:
            for i in range(n):
                l = i ^ j
                if l > i:
                    _cmpx(a, i, l, True)
            j //= 2
        shift //= 2
    return a


def _topk_kernel(qy_ref, k1_ref, k2_ref, theta_ref, e1_ref, s2_ref, m2_ref):
    K = PEER_TOPK
    half = PEER_DQ // 2
    tt = qy_ref.shape[0]
    G = N_KEYS // SUBLANES
    sub = lax.broadcasted_iota(jnp.int32, (SUBLANES, LANES), 0)

    def stagger(d, base):
        out = d[base]
        for r in range(1, SUBLANES):
            out = jnp.where(sub == r, d[base + r], out)
        return out

    def head(h, carry):
        col = pl.multiple_of(h * PEER_DQ, PEER_DQ)
        q1 = qy_ref[:, pl.ds(col, half)]
        q2 = qy_ref[:, pl.ds(col + half, half)]
        s1 = _dot_nt(k1_ref[...], q1, HIGHEST)
        s2 = _dot_nt(k2_ref[...], q2, HIGHEST)
        for c in range(tt // LANES):
            cs = slice(c * LANES, (c + 1) * LANES)
            s2_ref[h, c] = s2[:, cs]
            s1g = [s1[SUBLANES * g:SUBLANES * (g + 1), cs] for g in range(G)]
            d1 = _top16_desc(s1g)
            d2 = _top16_desc([s2[SUBLANES * g:SUBLANES * (g + 1), cs] for g in range(G)])
            d2_lo, d2_hi, d1_hi = stagger(d2, 0), stagger(d2, SUBLANES), stagger(d1, SUBLANES)
            cand = [d1[0] + d2_lo, d1[0] + d2_hi, d1_hi + d2[0]] + [d1[i] + d2_lo for i in range(1, SUBLANES)]
            cand += [jnp.full((SUBLANES, LANES), -jnp.inf, F32)] * (K - len(cand))
            top = _top16_desc(cand)
            z = jnp.ones_like(top[0])
            for i in range(1, K):
                z = z + jnp.exp(top[i] - top[0])
            zinv = 1.0 / z
            tau = top[K - 1]
            m2_ref[h, :, cs] = d2[0][0:1, :]
            for g in range(G):
                theta = jnp.full((SUBLANES, LANES), jnp.inf, F32)
                for j in range(K):
                    theta = jnp.where(s1g[g] + d2[j] >= tau, d2[j], theta)
                theta_ref[h, c, SUBLANES * g:SUBLANES * (g + 1), :] = theta
                e1_ref[h, c, SUBLANES * g:SUBLANES * (g + 1), :] = jnp.exp(s1g[g] - d1[0]) * zinv
        return carry

    lax.fori_loop(0, PEER_HEADS, head, 0)


def _topk(qy, keys1, keys2, tt):
    T, nq = qy.shape
    H = PEER_HEADS
    full = lambda a: pl.BlockSpec(a.shape, lambda i: (0,) * a.ndim)
    big = pl.BlockSpec((H, tt // LANES, N_KEYS, LANES), lambda i: (0, i, 0, 0))
    return pl.pallas_call(
        _topk_kernel,
        grid=(T // tt,),
        in_specs=[pl.BlockSpec((tt, nq), lambda i: (i, 0)), full(keys1), full(keys2)],
        out_specs=[big, big, big, pl.BlockSpec((H, 1, tt), lambda i: (0, 0, i))],
        out_shape=[jax.ShapeDtypeStruct((H, T // LANES, N_KEYS, LANES), F32)] * 3
        + [jax.ShapeDtypeStruct((H, 1, T), F32)],
        compiler_params=_params(("arbitrary",)),
        name="topk",
    )(qy, keys1, keys2)


def _peer_kernel(h2t_ref, u_ref, vt_ref, theta_ref, e1_ref, s2_ref, m2_ref, x1_ref, gt2_ref, g_ref, o_ref,
                 acc_ref, e2_ref, st_ref, wt_ref, *, na):
    j = pl.program_id(1)
    H = PEER_HEADS
    tt = h2t_ref.shape[1]
    vreg = (SUBLANES, LANES)
    ncol = tt // LANES
    nkb = N_KEYS // GATE_KB
    blk = (GATE_KB // SUBLANES, SUBLANES, LANES)

    @pl.when(j == 0)
    def _():
        acc_ref[...] = jnp.zeros(acc_ref.shape, F32)
        for h in range(H):
            for c in range(ncol):
                e2_ref[h, c] = jnp.exp(s2_ref[h, c] - m2_ref[h, :, c * LANES:(c + 1) * LANES])

    st = _dot(u_ref[...], h2t_ref[...])
    for c in range(ncol):
        st_ref[c] = st[:, c * LANES:(c + 1) * LANES].reshape(na, nkb, GATE_KB, LANES)

    def gate_blocks(a0, nab):
        def block(i, carry):
            c = lax.shift_right_logical(i, nkb.bit_length() - 1)
            kb = lax.bitwise_and(i, nkb - 1)
            was = [jnp.zeros(blk, F32) for _ in range(nab)]
            for h in range(H):
                s2 = s2_ref[h, c, kb].reshape(blk)
                e2 = e2_ref[h, c, kb].reshape(blk)
                for t in range(nab):
                    thb = jnp.broadcast_to(theta_ref[h, c, a0 + t:a0 + t + 1, :], vreg)
                    e1b = jnp.broadcast_to(e1_ref[h, c, a0 + t:a0 + t + 1, :], vreg)
                    was[t] = was[t] + jnp.where(s2 >= thb, e1b * e2, 0.0)
            for t in range(nab):
                sv = st_ref[c, a0 + t, kb].reshape(blk)
                act = 0.5 * sv * (1.0 + lax.erf(sv * (0.5 ** 0.5)))
                wt_ref[c, a0 + t, kb] = (was[t] * act).reshape(GATE_KB, LANES).astype(BF16)
            return carry

        lax.fori_loop(0, ncol * nkb, block, 0)

    for a0 in range(0, na, GATE_AB):
        gate_blocks(a0, GATE_AB)
    gated = jnp.concatenate([wt_ref[c].reshape(na * N_KEYS, LANES) for c in range(ncol)], axis=1)
    acc_ref[...] += _dot(vt_ref[...], gated)

    @pl.when(j == pl.num_programs(1) - 1)
    def _():
        y = acc_ref[...].T
        o_ref[...] = x1_ref[...] + gt2_ref[...] * _rms(y, g_ref[...])


def _peer(h2t, u_b, vt_b, theta, e1, s2, m2, x1, gt2, g_post, tt, ec, tiles_per_batch):
    D, T = h2t.shape
    E = u_b.shape[0]
    H = PEER_HEADS
    na = ec // N_KEYS
    ncol = tt // LANES
    nkb = N_KEYS // GATE_KB
    s2 = s2.reshape(H, T // LANES, nkb, GATE_KB, LANES)
    rows = pl.BlockSpec((H, ncol, na, LANES), lambda i, j: (0, i, j, 0))
    slab = (ncol, na, nkb, GATE_KB, LANES)
    return pl.pallas_call(
        functools.partial(_peer_kernel, na=na),
        grid=(T // tt, E // ec),
        in_specs=[pl.BlockSpec((D, tt), lambda i, j: (0, i)),
                  pl.BlockSpec((ec, D), lambda i, j: (j, 0)),
                  pl.BlockSpec((D, ec), lambda i, j: (0, j)),
                  rows, rows,
                  pl.BlockSpec((H, ncol, nkb, GATE_KB, LANES), lambda i, j: (0, i, 0, 0, 0)),
                  pl.BlockSpec((H, 1, tt), lambda i, j: (0, 0, i)),
                  pl.BlockSpec((tt, D), lambda i, j: (i, 0)),
                  pl.BlockSpec((None, 1, D), lambda i, j: (i // tiles_per_batch, 0, 0)),
                  pl.BlockSpec(g_post.shape, lambda i, j: (0, 0))],
        out_specs=pl.BlockSpec((tt, D), lambda i, j: (i, 0)),
        out_shape=jax.ShapeDtypeStruct((T, D), F32),
        scratch_shapes=[pltpu.VMEM((D, tt), F32), pltpu.VMEM((H, ncol, nkb, GATE_KB, LANES), F32),
                        pltpu.VMEM(slab, F32), pltpu.VMEM(slab, BF16)],
        compiler_params=_params(("arbitrary", "arbitrary")),
        name="peer",
    )(h2t, u_b, vt_b, theta, e1, s2, m2, x1, gt2, g_post)


def _pick(n, pref):
    t = pref
    while n % t:
        t //= 2
    return t


def _layer(x, c_pad, w_ada, b_ada, g_pre_mix, g_post_mix, w_in, b_fgate, fox_w_o, conv_w, a_log, dt_bias,
           gdn_norm_w, gdn_w_o, b_branch_gate, w_out, g_pre_ffn, g_post_ffn, peer_w_query, peer_keys1,
           peer_keys2, peer_u, peer_v):
    B, S, D = x.shape
    T = B * S
    mod = _ada(c_pad, w_ada, b_ada.reshape(1, -1))[:B]
    sh1, sc1, gt1, sh2, sc2, gt2 = [m.reshape(B, 1, D) for m in jnp.split(mod, 6, axis=-1)]

    off = [0]
    for s in SPLITS:
        off.append(off[-1] + s)
    col = lambda i, j: w_in[:, off[i]:off[j]]
    wf = col(0, 3).astype(BF16)
    wg = jnp.concatenate([col(4, 8)], axis=1).astype(BF16)
    wgate = col(10, 12).astype(BF16)
    pad = LANES - (FOX_HEADS + 2 * GDN_HEADS)
    ws = jnp.concatenate([col(3, 4), col(8, 10), jnp.zeros((D, pad), F32)], axis=1).astype(BF16)
    zpad = jnp.zeros((pad,), F32)
    bsm = jnp.concatenate([b_fgate, jnp.zeros((GDN_HEADS,), F32), dt_bias, zpad]).reshape(1, LANES)
    alog_row = jnp.concatenate([jnp.zeros((FOX_HEADS + GDN_HEADS,), F32), a_log, zpad]).reshape(1, LANES)
    bgate = b_branch_gate.reshape(1, 2 * D)

    tm = _pick(S, 256)
    fqkv, gqkv, gz, gates, small = _inproj(x, sc1, sh1, g_pre_mix.reshape(1, D), wf, wg, wgate, ws, bgate,
                                           bsm, alog_row, tm)
    cum = _cumsum(small, _pick(S, 512))
    ya = _fox(fqkv, cum, _pick(S, 512))
    yb = _gdn(gqkv, gz, small, conv_w, gdn_norm_w.reshape(1, GDN_DV))
    x1, h2t, qy = _outproj(ya, yb, gates, x, gt1, sc2, sh2, g_post_mix.reshape(1, D), g_pre_ffn.reshape(1, D),
                          fox_w_o.astype(BF16), gdn_w_o.astype(BF16), w_out.astype(BF16),
                          peer_w_query.astype(BF16), tm)
    qy = qy.reshape(T, -1)
    theta, e1, s2, m2 = _topk(qy, peer_keys1, peer_keys2, _pick(T, 256))
    tt = _pick(S, 512)
    out = _peer(h2t, peer_u.astype(BF16), peer_v.T.astype(BF16), theta, e1, s2, m2,
                x1.reshape(T, D), gt2, g_post_ffn.reshape(1, D), tt, 2048, S // tt)
    return out.reshape(B, S, D)


def kernel(x, c, w_ada, b_ada, g_pre_mix, g_post_mix, w_in, b_fgate, fox_w_o, conv_w, a_log, dt_bias,
           gdn_norm_w, gdn_w_o, b_branch_gate, w_out, g_pre_ffn, g_post_ffn, peer_w_query, peer_keys1,
           peer_keys2, peer_u, peer_v):
    B = x.shape[0]
    c_pad = jnp.pad(c, ((0, (-B) % 8), (0, 0)))
    for l in range(w_ada.shape[0]):
        x = _layer(x, c_pad, w_ada[l], b_ada[l], g_pre_mix[l], g_post_mix[l], w_in[l], b_fgate[l],
                   fox_w_o[l], conv_w[l], a_log[l], dt_bias[l], gdn_norm_w[l], gdn_w_o[l], b_branch_gate[l],
                   w_out[l], g_pre_ffn[l], g_post_ffn[l], peer_w_query[l], peer_keys1[l], peer_keys2[l],
                   peer_u[l], peer_v[l])
    return x
```

```python
import functools
import math

import jax
import jax.numpy as jnp
from jax import lax
from jax.experimental import pallas as pl
from jax.experimental.pallas import tpu as pltpu

F32 = jnp.float32
BF16 = jnp.bfloat16
HIGHEST = lax.Precision.HIGHEST

D_MODEL = 1024
EPS = 1e-6
FOX_HEADS = 8
FOX_DH = 64
GDN_HEADS = 4
GDN_DK = 128
GDN_DV = 128
CONV_W = 4
PEER_HEADS = 8
N_KEYS = 128
PEER_DQ = 256
PEER_TOPK = 16
FOX_W = FOX_HEADS * FOX_DH
GDN_KW = GDN_HEADS * GDN_DK
GDN_VW = GDN_HEADS * GDN_DV
SPLITS = (FOX_W, FOX_W, FOX_W, FOX_HEADS, GDN_KW, GDN_KW, GDN_VW, GDN_VW, GDN_HEADS, GDN_HEADS,
          D_MODEL, D_MODEL)

LANES = 128
GDN_CHUNK = 128
FOX_ROW_SPLIT = 1
GATE_KB = 16
GATE_AB = 8
NEG_BIG = -1e30
LOG2E = 1.4426950408889634
VMEM_LIMIT = 56 * 1024 * 1024


def _params(sem, flags=None):
    return pltpu.CompilerParams(dimension_semantics=sem, vmem_limit_bytes=VMEM_LIMIT, flags=flags)


def _dot(a, b, precision=None):
    return jnp.dot(a, b, preferred_element_type=F32, precision=precision)


def _dot_nt(a, b, precision=None):
    return lax.dot_general(a, b, (((1,), (1,)), ((), ())), preferred_element_type=F32,
                           precision=precision)


def _sigmoid(x):
    return 1.0 / (1.0 + jnp.exp(-x))


def _silu(x):
    return x * _sigmoid(x)


def _softplus(x):
    return jnp.maximum(x, 0.0) + jnp.log(1.0 + jnp.exp(-jnp.abs(x)))


def _log_sigmoid(x):
    return -_softplus(-x)


def _rms(x, g):
    return x * lax.rsqrt(jnp.mean(x * x, axis=-1, keepdims=True) + EPS) * g


def _ada_kernel(c_ref, w_ref, b_ref, o_ref):
    c = c_ref[...]
    o_ref[...] = _dot(_silu(c), w_ref[...], HIGHEST) + b_ref[...]


def _ada(c_pad, w, b):
    n = w.shape[1]
    bn = 1024
    return pl.pallas_call(
        _ada_kernel,
        grid=(n // bn,),
        in_specs=[pl.BlockSpec(c_pad.shape, lambda j: (0, 0)),
                  pl.BlockSpec((D_MODEL, bn), lambda j: (0, j)),
                  pl.BlockSpec((1, bn), lambda j: (0, j))],
        out_specs=pl.BlockSpec((c_pad.shape[0], bn), lambda j: (0, j)),
        out_shape=jax.ShapeDtypeStruct((c_pad.shape[0], n), F32),
        compiler_params=_params(("arbitrary",)),
        name="ada",
    )(c_pad, w, b)


def _inproj_kernel(x_ref, sc_ref, sh_ref, g_ref, wf_ref, wg_ref, wgate_ref, ws_ref, bgate_ref,
                   bsm_ref, alog_ref, fqkv_ref, gqkv_ref, gz_ref, gates_ref, small_ref):
    x = x_ref[...]
    h = _rms(x, g_ref[...]) * (1.0 + sc_ref[...]) + sh_ref[...]
    hb = h.astype(BF16)
    pf = _dot(hb, wf_ref[...])
    lane = lax.broadcasted_iota(jnp.int32, (1, 3 * FOX_W), 1)
    pf = jnp.where(lane < FOX_W, pf * (FOX_DH ** -0.5 * LOG2E), pf)
    fqkv_ref[...] = pf.astype(BF16)
    pg = _dot(hb, wg_ref[...])
    gqkv_ref[...] = pg[:, :3 * GDN_KW]
    gz_ref[...] = pg[:, 3 * GDN_KW:]
    gates_ref[...] = _sigmoid(_dot(hb, wgate_ref[...]) + bgate_ref[...])
    z = _dot(hb, ws_ref[...]) + bsm_ref[...]
    sl = lax.broadcasted_iota(jnp.int32, (1, LANES), 1)
    neg_a = -jnp.exp(alog_ref[...])
    small_ref[...] = jnp.where(sl < FOX_HEADS, _log_sigmoid(z),
                               jnp.where(sl < FOX_HEADS + GDN_HEADS, _sigmoid(z), neg_a * _softplus(z)))


def _inproj(x, sc1, sh1, g_pre, wf, wg, wgate, ws, bgate, bsm, alog_row, tm):
    B, S, D = x.shape
    full = lambda a: pl.BlockSpec(a.shape, lambda b, i: (0,) * a.ndim)
    mod = pl.BlockSpec((None, 1, D), lambda b, i: (b, 0, 0))
    tile = lambda n: pl.BlockSpec((None, tm, n), lambda b, i: (b, i, 0))
    return pl.pallas_call(
        _inproj_kernel,
        grid=(B, S // tm),
        in_specs=[tile(D), mod, mod, full(g_pre), full(wf), full(wg), full(wgate), full(ws),
                  full(bgate), full(bsm), full(alog_row)],
        out_specs=[tile(3 * FOX_W), tile(3 * GDN_KW), tile(GDN_VW), tile(2 * D), tile(LANES)],
        out_shape=[jax.ShapeDtypeStruct((B, S, 3 * FOX_W), BF16),
                   jax.ShapeDtypeStruct((B, S, 3 * GDN_KW), F32),
                   jax.ShapeDtypeStruct((B, S, GDN_VW), F32),
                   jax.ShapeDtypeStruct((B, S, 2 * D), F32),
                   jax.ShapeDtypeStruct((B, S, LANES), F32)],
        compiler_params=_params(("arbitrary", "arbitrary")),
        name="inproj",
    )(x, sc1, sh1, g_pre, wf, wg, wgate, ws, bgate, bsm, alog_row)


def _cumsum_kernel(s_ref, o_ref, carry_ref):
    @pl.when(pl.program_id(1) == 0)
    def _():
        carry_ref[...] = jnp.zeros_like(carry_ref)

    v = s_ref[...]
    n = v.shape[0]
    r = lax.broadcasted_iota(jnp.int32, (n, n), 0)
    c = lax.broadcasted_iota(jnp.int32, (n, n), 1)
    tri = jnp.where(r >= c, 1.0, 0.0).astype(F32)
    cs = _dot(tri, v, HIGHEST) + carry_ref[...]
    o_ref[...] = cs
    carry_ref[...] = cs[n - 1:n, :]


def _cumsum(small, tc):
    B, S, L = small.shape
    return pl.pallas_call(
        _cumsum_kernel,
        grid=(B, S // tc),
        in_specs=[pl.BlockSpec((None, tc, L), lambda b, i: (b, i, 0))],
        out_specs=pl.BlockSpec((None, tc, L), lambda b, i: (b, i, 0)),
        out_shape=jax.ShapeDtypeStruct((B, S, L), F32),
        scratch_shapes=[pltpu.VMEM((1, L), F32)],
        compiler_params=_params(("arbitrary", "arbitrary")),
        name="cumsum",
    )(small)


def _split3(x):
    p1 = x.astype(BF16).astype(F32)
    r = x - p1
    p2 = r.astype(BF16).astype(F32)
    p3 = (r - p2).astype(BF16).astype(F32)
    return p1, p2, p3


def _fox_kernel(q_ref, k_ref, v_ref, cum_ref, o_ref, kmod_ref, vmod_ref, m_ref, acc_ref, *, tq):
    hp = pl.program_id(1)
    i = pl.program_id(2)
    S = k_ref.shape[0]
    lane = lax.broadcasted_iota(jnp.int32, (1, LANES), 1)

    def head_col(tile, h):
        return jnp.sum(jnp.where(lane == h, tile, 0.0), axis=1, keepdims=True)

    def with_bias(base, hh, pieces, ones_first):
        sp = (1 - hh) * FOX_DH
        in_head = (lane >= hh * FOX_DH) & (lane < (hh + 1) * FOX_DH)
        out = jnp.where(in_head, base, 0.0)
        po, oo = (3, 0) if ones_first else (0, 3)
        for t in range(3):
            out = jnp.where(lane == sp + po + t, pieces[t], out)
            out = jnp.where(lane == sp + oo + t, 1.0, out)
        return out.astype(BF16)

    @pl.when(i == 0)
    def _():
        def chunk(ci, carry):
            start = pl.multiple_of(ci * tq, tq)
            k2 = k_ref[pl.ds(start, tq), :].astype(F32)
            v2 = v_ref[pl.ds(start, tq), :].astype(F32)
            cum = cum_ref[pl.ds(start, tq), :]
            for hh in range(2):
                ck = head_col(cum, hp * 2 + hh)
                kmod_ref[hh, pl.ds(start, tq), :] = with_bias(k2, hh, _split3(-ck * LOG2E), True)
                in_head = (lane >= hh * FOX_DH) & (lane < (hh + 1) * FOX_DH)
                vm = jnp.where(in_head, v2, jnp.where(lane == (1 - hh) * FOX_DH, 1.0, 0.0))
                vmod_ref[hh, pl.ds(start, tq), :] = vm.astype(BF16)
            return carry
        lax.fori_loop(0, S // tq, chunk, 0)

    q2 = q_ref[...].astype(F32)
    cumq = cum_ref[pl.ds(pl.multiple_of(i * tq, tq), tq), :]
    qmods = [with_bias(q2, hh, _split3(head_col(cumq, hp * 2 + hh) * LOG2E), False) for hh in range(2)]
    rr = lax.broadcasted_iota(jnp.int32, (tq // FOX_ROW_SPLIT, tq), 0)
    cc = lax.broadcasted_iota(jnp.int32, (tq // FOX_ROW_SPLIT, tq), 1)
    m_ref[...] = jnp.full(m_ref.shape, NEG_BIG, F32)
    acc_ref[...] = jnp.zeros(acc_ref.shape, F32)

    RS = FOX_ROW_SPLIT
    tr = tq // RS

    def step(j, masked):
        start = pl.multiple_of(j * tq, tq)
        chains = [(hh, r) for hh in range(2) for r in range(RS)]
        scores = [_dot_nt(qmods[hh][r * tr:(r + 1) * tr], kmod_ref[hh, pl.ds(start, tq), :]) for hh, r in chains]
        for (hh, r), s in zip(chains, scores):
            rows = slice(r * tr, (r + 1) * tr)
            if masked:
                s = jnp.where(rr + r * tr >= cc, s, NEG_BIG)
            m_prev = m_ref[hh, rows, :]
            m_new = jnp.maximum(m_prev, jnp.max(s, axis=1, keepdims=True))
            p = jnp.exp2(s - jnp.concatenate([m_new] * (tq // LANES), axis=1)).astype(BF16)
            acc_ref[hh, rows, :] = (jnp.exp2(m_prev - m_new) * acc_ref[hh, rows, :]
                                    + _dot(p, vmod_ref[hh, pl.ds(start, tq), :]))
            m_ref[hh, rows, :] = m_new

    def body(j, carry):
        step(j, False)
        return carry

    lax.fori_loop(0, i, body, 0)
    step(i, True)
    outs = []
    for hh in range(2):
        acc = acc_ref[hh]
        outs.append(acc / head_col(acc, (1 - hh) * FOX_DH))
    o_ref[...] = jnp.where(lane < FOX_DH, outs[0], outs[1]).astype(o_ref.dtype)


def _fox(fqkv, cum, tq):
    B, S, _ = fqkv.shape
    nhp = FOX_HEADS // 2
    return pl.pallas_call(
        functools.partial(_fox_kernel, tq=tq),
        grid=(B, nhp, S // tq),
        in_specs=[pl.BlockSpec((None, tq, LANES), lambda b, hp, i: (b, i, hp)),
                  pl.BlockSpec((None, S, LANES), lambda b, hp, i: (b, 0, nhp + hp)),
                  pl.BlockSpec((None, S, LANES), lambda b, hp, i: (b, 0, 2 * nhp + hp)),
                  pl.BlockSpec((None, S, LANES), lambda b, hp, i: (b, 0, 0))],
        out_specs=pl.BlockSpec((None, tq, LANES), lambda b, hp, i: (b, i, hp)),
        out_shape=jax.ShapeDtypeStruct((B, S, FOX_W), BF16),
        scratch_shapes=[pltpu.VMEM((2, S, LANES), BF16), pltpu.VMEM((2, S, LANES), BF16),
                        pltpu.VMEM((2, tq, LANES), F32), pltpu.VMEM((2, tq, LANES), F32)],
        compiler_params=_params(("arbitrary", "arbitrary", "arbitrary")),
        name="fox",
    )(fqkv, fqkv, fqkv, cum)


def _level_masks(n):
    r = lax.broadcasted_iota(jnp.int32, (n, n), 0)
    c = lax.broadcasted_iota(jnp.int32, (n, n), 1)
    masks = []
    s = 1
    while s < n:
        sel = ((r // (2 * s)) == (c // (2 * s))) & (((r // s) % 2) == 1) & (((c // s) % 2) == 0)
        masks.append(jnp.where(sel, 1.0, 0.0).astype(F32))
        s *= 2
    return masks


def _unit_lower_inverses(lows, masks):
    n = lows[0].shape[0]
    r = lax.broadcasted_iota(jnp.int32, (n, n), 0)
    c = lax.broadcasted_iota(jnp.int32, (n, n), 1)
    xs = [jnp.where(r == c, 1.0, 0.0).astype(F32)] * len(lows)
    for mask in masks:
        xbs = [x.astype(BF16) for x in xs]
        xcs = [_dot(xb, (low * mask).astype(BF16)) for xb, low in zip(xbs, lows)]
        xs = [x - _dot(xc.astype(BF16), xb) for x, xc, xb in zip(xs, xcs, xbs)]
    return xs


def _gdn_kernel(x_ref, gz_ref, sm_ref, cw_ref, nw_ref, o_ref, state_ref, carry_ref):
    ci = pl.program_id(0)
    NB = x_ref.shape[0]
    b0 = pl.program_id(1) * NB
    C = GDN_CHUNK

    @pl.when(ci == 0)
    def _():
        for bb in range(NB):
            state_ref[b0 + bb] = jnp.zeros(state_ref.shape[1:], F32)
            carry_ref[b0 + bb] = jnp.zeros(carry_ref.shape[1:], F32)

    row8 = lax.broadcasted_iota(jnp.int32, (8, 1), 0)
    r = lax.broadcasted_iota(jnp.int32, (C, C), 0)
    c = lax.broadcasted_iota(jnp.int32, (C, C), 1)
    causal = r >= c
    strict = r > c
    tri = jnp.where(causal, 1.0, 0.0).astype(F32)
    nw = nw_ref[...]
    masks = _level_masks(C)

    ys, sms, gcums, gcum_ts = [], [], [], []
    for bb in range(NB):
        x = x_ref[bb]
        prev = carry_ref[b0 + bb]
        y = cw_ref[CONV_W - 1:CONV_W, :] * x
        for j in range(1, CONV_W):
            xr = pltpu.roll(x, j, 0)
            pr = pltpu.roll(prev, j, 0)
            first = jnp.where(row8 < j, pr, xr[0:8, :])
            y = y + cw_ref[CONV_W - 1 - j:CONV_W - j, :] * jnp.concatenate([first, xr[8:, :]], axis=0)
        carry_ref[b0 + bb] = x[C - 8:C, :]
        ys.append(_silu(y))
        sms.append(sm_ref[bb])
        gcums.append(_dot(tri, sms[bb], HIGHEST))
        gcum_ts.append(gcums[bb].T)

    streams = [(bb, h) for bb in range(NB) for h in range(GDN_HEADS)]
    idx = range(len(streams))
    qs, ks, vs, betas, g_cols, g_lasts, decays = [], [], [], [], [], [], []
    for bb, h in streams:
        y, sm, gcum = ys[bb], sms[bb], gcums[bb]
        q = y[:, h * GDN_DK:(h + 1) * GDN_DK]
        k = y[:, GDN_KW + h * GDN_DK:GDN_KW + (h + 1) * GDN_DK]
        qs.append(q * lax.rsqrt(jnp.sum(q * q, axis=1, keepdims=True) + EPS) * (GDN_DK ** -0.5))
        ks.append(k * lax.rsqrt(jnp.sum(k * k, axis=1, keepdims=True) + EPS))
        vs.append(y[:, 2 * GDN_KW + h * GDN_DV:2 * GDN_KW + (h + 1) * GDN_DV])
        betas.append(sm[:, FOX_HEADS + h:FOX_HEADS + h + 1])
        gl = FOX_HEADS + GDN_HEADS + h
        g_cols.append(gcum[:, gl:gl + 1])
        g_lasts.append(gcum[C - 1:C, gl:gl + 1])
        diff = g_cols[-1] - gcum_ts[bb][gl:gl + 1, :]
        decays.append(jnp.where(causal, jnp.exp(jnp.where(causal, diff, 0.0)), 0.0))
    kbs = [ks[i] * betas[i] for i in idx]
    egs = [jnp.exp(g_cols[i]) for i in idx]
    kqs = [_dot_nt(jnp.concatenate([kbs[i], qs[i]], axis=0).astype(BF16), ks[i].astype(BF16)) for i in idx]
    lows = [jnp.where(strict, kqs[i][:C] * decays[i], 0.0) for i in idx]
    attns = [(kqs[i][C:] * decays[i]).astype(BF16) for i in idx]
    tinvs = _unit_lower_inverses(lows, masks)
    uws = [_dot(tinvs[i].astype(BF16),
                jnp.concatenate([vs[i] * betas[i], kbs[i] * egs[i]], axis=1).astype(BF16)) for i in idx]
    sts = [state_ref[b0 + bb, h] for bb, h in streams]
    wqs = [_dot(jnp.concatenate([uws[i][:, GDN_DV:], qs[i] * egs[i]], axis=0).astype(BF16),
                sts[i].astype(BF16)) for i in idx]
    v_news = [(uws[i][:, :GDN_DV] - wqs[i][:C]).astype(BF16) for i in idx]
    k_dec_ts = [(ks[i] * jnp.exp(g_lasts[i] - g_cols[i])).T.astype(BF16) for i in idx]
    os_ = [wqs[i][C:] + _dot(attns[i], v_news[i]) for i in idx]
    new_sts = [sts[i] * jnp.exp(g_lasts[i]) + _dot(k_dec_ts[i], v_news[i]) for i in idx]
    for i, (bb, h) in enumerate(streams):
        state_ref[b0 + bb, h] = new_sts[i]
        o = _rms(os_[i], nw) * _silu(gz_ref[bb, :, h * GDN_DV:(h + 1) * GDN_DV])
        o_ref[bb, :, h * GDN_DV:(h + 1) * GDN_DV] = o.astype(o_ref.dtype)


def _gdn(gqkv, gz, small, conv_w, norm_w):
    B, S, _ = gqkv.shape
    C = GDN_CHUNK
    nb = 2 if B % 2 == 0 else 1
    tile = lambda n: pl.BlockSpec((nb, C, n), lambda ci, b: (b, ci, 0))
    full = lambda a: pl.BlockSpec(a.shape, lambda ci, b: (0,) * a.ndim)
    return pl.pallas_call(
        _gdn_kernel,
        grid=(S // C, B // nb),
        in_specs=[tile(3 * GDN_KW), tile(GDN_VW), tile(LANES), full(conv_w), full(norm_w)],
        out_specs=tile(GDN_VW),
        out_shape=jax.ShapeDtypeStruct((B, S, GDN_VW), BF16),
        scratch_shapes=[pltpu.VMEM((B, GDN_HEADS, GDN_DK, GDN_DV), F32),
                        pltpu.VMEM((B, 8, 3 * GDN_KW), F32)],
        compiler_params=_params(("arbitrary", "arbitrary")),
        name="gdn",
    )(gqkv, gz, small, conv_w, norm_w)


def _outproj_kernel(ya_ref, yb_ref, gates_ref, x_ref, gt1_ref, sc2_ref, sh2_ref, gpm_ref, gpf_ref,
                    wa_ref, wb_ref, wo_ref, wq_ref, x1_ref, h2t_ref, qy_ref):
    ya = _dot(ya_ref[...], wa_ref[...])
    yb = _dot(yb_ref[...], wb_ref[...])
    gates = gates_ref[...]
    mix_in = gates[:, :D_MODEL] * ya + gates[:, D_MODEL:] * yb
    mix = _dot(mix_in.astype(BF16), wo_ref[...])
    x1 = x_ref[...] + gt1_ref[...] * _rms(mix, gpm_ref[...])
    x1_ref[...] = x1
    h2 = _rms(x1, gpf_ref[...]) * (1.0 + sc2_ref[...]) + sh2_ref[...]
    h2t_ref[...] = h2.T.astype(BF16)
    qy_ref[...] = _dot(h2.astype(BF16), wq_ref[...])


def _outproj(ya, yb, gates, x, gt1, sc2, sh2, gpm, gpf, wa, wb, wo, wq, tm):
    B, S, D = x.shape
    full = lambda a: pl.BlockSpec(a.shape, lambda b, i: (0,) * a.ndim)
    mod = pl.BlockSpec((None, 1, D), lambda b, i: (b, 0, 0))
    tile = lambda n: pl.BlockSpec((None, tm, n), lambda b, i: (b, i, 0))
    nq = wq.shape[1]
    return pl.pallas_call(
        _outproj_kernel,
        grid=(B, S // tm),
        in_specs=[tile(FOX_W), tile(GDN_VW), tile(2 * D), tile(D), mod, mod, mod, full(gpm), full(gpf),
                  full(wa), full(wb), full(wo), full(wq)],
        out_specs=[tile(D), pl.BlockSpec((D, tm), lambda b, i: (0, b * (S // tm) + i)), tile(nq)],
        out_shape=[jax.ShapeDtypeStruct((B, S, D), F32),
                   jax.ShapeDtypeStruct((D, B * S), BF16),
                   jax.ShapeDtypeStruct((B, S, nq), F32)],
        compiler_params=_params(("arbitrary", "arbitrary")),
        name="outproj",
    )(ya, yb, gates, x, gt1, sc2, sh2, gpm, gpf, wa, wb, wo, wq)


SUBLANES = 8


def _cmpx(a, i, l, desc):
    hi, lo = jnp.maximum(a[i], a[l]), jnp.minimum(a[i], a[l])
    a[i], a[l] = (hi, lo) if desc else (lo, hi)


def _top16_desc(a):
    a = list(a)
    n = len(a)
    k = 2
    while k <= n:
        j = k // 2
        while j >= 1:
            for i in range(n):
                l = i ^ j
                if l > i:
                    _cmpx(a, i, l, (i & k) == 0)
            j //= 2
        k *= 2
    shift = SUBLANES // 2
    while shift >= 1:
        other = [pltpu.roll(v, shift, 0) for v in a]
        a = [jnp.maximum(a[i], other[n - 1 - i]) for i in range(n)]
        j = n // 2
        while j >= 1:
            for i in range(n):
                l = i ^ j
                if l > i:
                    _cmpx(a, i, l, True)
            j //= 2
        shift //= 2
    return a


def _topk_kernel(qy_ref, k1_ref, k2_ref, theta_ref, e1_ref, s2_ref, m2_ref):
    K = PEER_TOPK
    half = PEER_DQ // 2
    tt = qy_ref.shape[0]
    G = N_KEYS // SUBLANES
    sub = lax.broadcasted_iota(jnp.int32, (SUBLANES, LANES), 0)

    def stagger(d, base):
        out = d[base]
        for r in range(1, SUBLANES):
            out = jnp.where(sub == r, d[base + r], out)
        return out

    def head(h, carry):
        col = pl.multiple_of(h * PEER_DQ, PEER_DQ)
        q1 = qy_ref[:, pl.ds(col, half)]
        q2 = qy_ref[:, pl.ds(col + half, half)]
        s1 = _dot_nt(k1_ref[...], q1, HIGHEST)
        s2 = _dot_nt(k2_ref[...], q2, HIGHEST)
        for c in range(tt // LANES):
            cs = slice(c * LANES, (c + 1) * LANES)
            s2_ref[h, c] = s2[:, cs]
            s1g = [s1[SUBLANES * g:SUBLANES * (g + 1), cs] for g in range(G)]
            d1 = _top16_desc(s1g)
            d2 = _top16_desc([s2[SUBLANES * g:SUBLANES * (g + 1), cs] for g in range(G)])
            d2_lo, d2_hi, d1_hi = stagger(d2, 0), stagger(d2, SUBLANES), stagger(d1, SUBLANES)
            cand = [d1[0] + d2_lo, d1[0] + d2_hi, d1_hi + d2[0]] + [d1[i] + d2_lo for i in range(1, SUBLANES)]
            cand += [jnp.full((SUBLANES, LANES), -jnp.inf, F32)] * (K - len(cand))
            top = _top16_desc(cand)
            z = jnp.ones_like(top[0])
            for i in range(1, K):
                z = z + jnp.exp(top[i] - top[0])
            zinv = 1.0 / z
            tau = top[K - 1]
            m2_ref[h, :, cs] = d2[0][0:1, :]
            for g in range(G):
                theta = jnp.full((SUBLANES, LANES), jnp.inf, F32)
                for j in range(K):
                    theta = jnp.where(s1g[g] + d2[j] >= tau, d2[j], theta)
                theta_ref[h, c, SUBLANES * g:SUBLANES * (g + 1), :] = theta
                e1_ref[h, c, SUBLANES * g:SUBLANES * (g + 1), :] = jnp.exp(s1g[g] - d1[0]) * zinv
        return carry

    lax.fori_loop(0, PEER_HEADS, head, 0)


def _topk(qy, keys1, keys2, tt):
    T, nq = qy.shape
    H = PEER_HEADS
    full = lambda a: pl.BlockSpec(a.shape, lambda i: (0,) * a.ndim)
    big = pl.BlockSpec((H, tt // LANES, N_KEYS, LANES), lambda i: (0, i, 0, 0))
    return pl.pallas_call(
        _topk_kernel,
        grid=(T // tt,),
        in_specs=[pl.BlockSpec((tt, nq), lambda i: (i, 0)), full(keys1), full(keys2)],
        out_specs=[big, big, big, pl.BlockSpec((H, 1, tt), lambda i: (0, 0, i))],
        out_shape=[jax.ShapeDtypeStruct((H, T // LANES, N_KEYS, LANES), F32)] * 3
        + [jax.ShapeDtypeStruct((H, 1, T), F32)],
        compiler_params=_params(("arbitrary",)),
        name="topk",
    )(qy, keys1, keys2)


def _peer_kernel(h2t_ref, u_ref, vt_ref, theta_ref, e1_ref, s2_ref, m2_ref, x1_ref, gt2_ref, g_ref, o_ref,
                 acc_ref, e2_ref, st_ref, wt_ref, *, na):
    j = pl.program_id(1)
    H = PEER_HEADS
    tt = h2t_ref.shape[1]
    vreg = (SUBLANES, LANES)
    ncol = tt // LANES
    nkb = N_KEYS // GATE_KB
    blk = (GATE_KB // SUBLANES, SUBLANES, LANES)

    @pl.when(j == 0)
    def _():
        acc_ref[...] = jnp.zeros(acc_ref.shape, F32)
        for h in range(H):
            for c in range(ncol):
                e2_ref[h, c] = jnp.exp(s2_ref[h, c] - m2_ref[h, :, c * LANES:(c + 1) * LANES])

    st = _dot(u_ref[...], h2t_ref[...])
    for c in range(ncol):
        st_ref[c] = st[:, c * LANES:(c + 1) * LANES].reshape(na, nkb, GATE_KB, LANES)

    def gate_blocks(a0, nab):
        def block(i, carry):
            c = lax.shift_right_logical(i, nkb.bit_length() - 1)
            kb = lax.bitwise_and(i, nkb - 1)
            was = [jnp.zeros(blk, F32) for _ in range(nab)]
            for h in range(H):
                s2 = s2_ref[h, c, kb].reshape(blk)
                e2 = e2_ref[h, c, kb].reshape(blk)
                for t in range(nab):
                    thb = jnp.broadcast_to(theta_ref[h, c, a0 + t:a0 + t + 1, :], vreg)
                    e1b = jnp.broadcast_to(e1_ref[h, c, a0 + t:a0 + t + 1, :], vreg)
                    was[t] = was[t] + jnp.where(s2 >= thb, e1b * e2, 0.0)
            for t in range(nab):
                sv = st_ref[c, a0 + t, kb].reshape(blk)
                act = 0.5 * sv * (1.0 + lax.erf(sv * (0.5 ** 0.5)))
                wt_ref[c, a0 + t, kb] = (was[t] * act).reshape(GATE_KB, LANES).astype(BF16)
            return carry

        lax.fori_loop(0, ncol * nkb, block, 0)

    for a0 in range(0, na, GATE_AB):
        gate_blocks(a0, GATE_AB)
    gated = jnp.concatenate([wt_ref[c].reshape(na * N_KEYS, LANES) for c in range(ncol)], axis=1)
    acc_ref[...] += _dot(vt_ref[...], gated)

    @pl.when(j == pl.num_programs(1) - 1)
    def _():
        y = acc_ref[...].T
        o_ref[...] = x1_ref[...] + gt2_ref[...] * _rms(y, g_ref[...])


def _peer(h2t, u_b, vt_b, theta, e1, s2, m2, x1, gt2, g_post, tt, ec, tiles_per_batch):
    D, T = h2t.shape
    E = u_b.shape[0]
    H = PEER_HEADS
    na = ec // N_KEYS
    ncol = tt // LANES
    nkb = N_KEYS // GATE_KB
    s2 = s2.reshape(H, T // LANES, nkb, GATE_KB, LANES)
    rows = pl.BlockSpec((H, ncol, na, LANES), lambda i, j: (0, i, j, 0))
    slab = (ncol, na, nkb, GATE_KB, LANES)
    return pl.pallas_call(
        functools.partial(_peer_kernel, na=na),
        grid=(T // tt, E // ec),
        in_specs=[pl.BlockSpec((D, tt), lambda i, j: (0, i)),
                  pl.BlockSpec((ec, D), lambda i, j: (j, 0)),
                  pl.BlockSpec((D, ec), lambda i, j: (0, j)),
                  rows, rows,
                  pl.BlockSpec((H, ncol, nkb, GATE_KB, LANES), lambda i, j: (0, i, 0, 0, 0)),
                  pl.BlockSpec((H, 1, tt), lambda i, j: (0, 0, i)),
                  pl.BlockSpec((tt, D), lambda i, j: (i, 0)),
                  pl.BlockSpec((None, 1, D), lambda i, j: (i // tiles_per_batch, 0, 0)),
                  pl.BlockSpec(g_post.shape, lambda i, j: (0, 0))],
        out_specs=pl.BlockSpec((tt, D), lambda i, j: (i, 0)),
        out_shape=jax.ShapeDtypeStruct((T, D), F32),
        scratch_shapes=[pltpu.VMEM((D, tt), F32), pltpu.VMEM((H, ncol, nkb, GATE_KB, LANES), F32),
                        pltpu.VMEM(slab, F32), pltpu.VMEM(slab, BF16)],
        compiler_params=_params(("arbitrary", "arbitrary")),
        name="peer",
    )(h2t, u_b, vt_b, theta, e1, s2, m2, x1, gt2, g_post)


def _pick(n, pref):
    t = pref
    while n % t:
        t //= 2
    return t


def _layer(x, c_pad, w_ada, b_ada, g_pre_mix, g_post_mix, w_in, b_fgate, fox_w_o, conv_w, a_log, dt_bias,
           gdn_norm_w, gdn_w_o, b_branch_gate, w_out, g_pre_ffn, g_post_ffn, peer_w_query, peer_keys1,
           peer_keys2, peer_u, peer_v):
    B, S, D = x.shape
    T = B * S
    mod = _ada(c_pad, w_ada, b_ada.reshape(1, -1))[:B]
    sh1, sc1, gt1, sh2, sc2, gt2 = [m.reshape(B, 1, D) for m in jnp.split(mod, 6, axis=-1)]

    off = [0]
    for s in SPLITS:
        off.append(off[-1] + s)
    col = lambda i, j: w_in[:, off[i]:off[j]]
    wf = col(0, 3).astype(BF16)
    wg = jnp.concatenate([col(4, 8)], axis=1).astype(BF16)
    wgate = col(10, 12).astype(BF16)
    pad = LANES - (FOX_HEADS + 2 * GDN_HEADS)
    ws = jnp.concatenate([col(3, 4), col(8, 10), jnp.zeros((D, pad), F32)], axis=1).astype(BF16)
    zpad = jnp.zeros((pad,), F32)
    bsm = jnp.concatenate([b_fgate, jnp.zeros((GDN_HEADS,), F32), dt_bias, zpad]).reshape(1, LANES)
    alog_row = jnp.concatenate([jnp.zeros((FOX_HEADS + GDN_HEADS,), F32), a_log, zpad]).reshape(1, LANES)
    bgate = b_branch_gate.reshape(1, 2 * D)

    tm = _pick(S, 512)
    fqkv, gqkv, gz, gates, small = _inproj(x, sc1, sh1, g_pre_mix.reshape(1, D), wf, wg, wgate, ws, bgate,
                                           bsm, alog_row, tm)
    cum = _cumsum(small, _pick(S, 512))
    ya = _fox(fqkv, cum, _pick(S, 512))
    yb = _gdn(gqkv, gz, small, conv_w, gdn_norm_w.reshape(1, GDN_DV))
    x1, h2t, qy = _outproj(ya, yb, gates, x, gt1, sc2, sh2, g_post_mix.reshape(1, D), g_pre_ffn.reshape(1, D),
                          fox_w_o.astype(BF16), gdn_w_o.astype(BF16), w_out.astype(BF16),
                          peer_w_query.astype(BF16), tm)
    qy = qy.reshape(T, -1)
    theta, e1, s2, m2 = _topk(qy, peer_keys1, peer_keys2, _pick(T, 256))
    tt = _pick(S, 512)
    out = _peer(h2t, peer_u.astype(BF16), peer_v.T.astype(BF16), theta, e1, s2, m2,
                x1.reshape(T, D), gt2, g_post_ffn.reshape(1, D), tt, 2048, S // tt)
    return out.reshape(B, S, D)


def kernel(x, c, w_ada, b_ada, g_pre_mix, g_post_mix, w_in, b_fgate, fox_w_o, conv_w, a_log, dt_bias,
           gdn_norm_w, gdn_w_o, b_branch_gate, w_out, g_pre_ffn, g_post_ffn, peer_w_query, peer_keys1,
           peer_keys2, peer_u, peer_v):
    B = x.shape[0]
    c_pad = jnp.pad(c, ((0, (-B) % 8), (0, 0)))
    for l in range(w_ada.shape[0]):
        x = _layer(x, c_pad, w_ada[l], b_ada[l], g_pre_mix[l], g_post_mix[l], w_in[l], b_fgate[l],
                   fox_w_o[l], conv_w[l], a_log[l], dt_bias[l], gdn_norm_w[l], gdn_w_o[l], b_branch_gate[l],
                   w_out[l], g_pre_ffn[l], g_post_ffn[l], peer_w_query[l], peer_keys1[l], peer_keys2[l],
                   peer_u[l], peer_v[l])
    return x
```

```python
import functools
import math

import jax
import jax.numpy as jnp
from jax import lax
from jax.experimental import pallas as pl
from jax.experimental.pallas import tpu as pltpu

F32 = jnp.float32
BF16 = jnp.bfloat16
HIGHEST = lax.Precision.HIGHEST

D_MODEL = 1024
EPS = 1e-6
FOX_HEADS = 8
FOX_DH = 64
GDN_HEADS = 4
GDN_DK = 128
GDN_DV = 128
CONV_W = 4
PEER_HEADS = 8
N_KEYS = 128
PEER_DQ = 256
PEER_TOPK = 16
FOX_W = FOX_HEADS * FOX_DH
GDN_KW = GDN_HEADS * GDN_DK
GDN_VW = GDN_HEADS * GDN_DV
SPLITS = (FOX_W, FOX_W, FOX_W, FOX_HEADS, GDN_KW, GDN_KW, GDN_VW, GDN_VW, GDN_HEADS, GDN_HEADS,
          D_MODEL, D_MODEL)

LANES = 128
GDN_CHUNK = 128
FOX_ROW_SPLIT = 1
GATE_KB = 16
GATE_AB = 8
NEG_BIG = -1e30
LOG2E = 1.4426950408889634
VMEM_LIMIT = 56 * 1024 * 1024


def _params(sem, flags=None):
    return pltpu.CompilerParams(dimension_semantics=sem, vmem_limit_bytes=VMEM_LIMIT, flags=flags)


def _dot(a, b, precision=None):
    return jnp.dot(a, b, preferred_element_type=F32, precision=precision)


def _dot_nt(a, b, precision=None):
    return lax.dot_general(a, b, (((1,), (1,)), ((), ())), preferred_element_type=F32,
                           precision=precision)


def _sigmoid(x):
    return 1.0 / (1.0 + jnp.exp(-x))


def _silu(x):
    return x * _sigmoid(x)


def _softplus(x):
    return jnp.maximum(x, 0.0) + jnp.log(1.0 + jnp.exp(-jnp.abs(x)))


def _log_sigmoid(x):
    return -_softplus(-x)


def _rms(x, g):
    return x * lax.rsqrt(jnp.mean(x * x, axis=-1, keepdims=True) + EPS) * g


def _ada_kernel(c_ref, w_ref, b_ref, o_ref):
    c = c_ref[...]
    o_ref[...] = _dot(_silu(c), w_ref[...], HIGHEST) + b_ref[...]


def _ada(c_pad, w, b):
    n = w.shape[1]
    bn = 1024
    return pl.pallas_call(
        _ada_kernel,
        grid=(n // bn,),
        in_specs=[pl.BlockSpec(c_pad.shape, lambda j: (0, 0)),
                  pl.BlockSpec((D_MODEL, bn), lambda j: (0, j)),
                  pl.BlockSpec((1, bn), lambda j: (0, j))],
        out_specs=pl.BlockSpec((c_pad.shape[0], bn), lambda j: (0, j)),
        out_shape=jax.ShapeDtypeStruct((c_pad.shape[0], n), F32),
        compiler_params=_params(("arbitrary",)),
        name="ada",
    )(c_pad, w, b)


def _inproj_kernel(x_ref, sc_ref, sh_ref, g_ref, wf_ref, wg_ref, wgate_ref, ws_ref, bgate_ref,
                   bsm_ref, alog_ref, fqkv_ref, gqkv_ref, gz_ref, gates_ref, small_ref):
    x = x_ref[...]
    h = _rms(x, g_ref[...]) * (1.0 + sc_ref[...]) + sh_ref[...]
    hb = h.astype(BF16)
    pf = _dot(hb, wf_ref[...])
    lane = lax.broadcasted_iota(jnp.int32, (1, 3 * FOX_W), 1)
    pf = jnp.where(lane < FOX_W, pf * (FOX_DH ** -0.5 * LOG2E), pf)
    fqkv_ref[...] = pf.astype(BF16)
    pg = _dot(hb, wg_ref[...])
    gqkv_ref[...] = pg[:, :3 * GDN_KW]
    gz_ref[...] = pg[:, 3 * GDN_KW:]
    gates_ref[...] = _sigmoid(_dot(hb, wgate_ref[...]) + bgate_ref[...])
    z = _dot(hb, ws_ref[...]) + bsm_ref[...]
    sl = lax.broadcasted_iota(jnp.int32, (1, LANES), 1)
    neg_a = -jnp.exp(alog_ref[...])
    small_ref[...] = jnp.where(sl < FOX_HEADS, _log_sigmoid(z),
                               jnp.where(sl < FOX_HEADS + GDN_HEADS, _sigmoid(z), neg_a * _softplus(z)))


def _inproj(x, sc1, sh1, g_pre, wf, wg, wgate, ws, bgate, bsm, alog_row, tm):
    B, S, D = x.shape
    full = lambda a: pl.BlockSpec(a.shape, lambda b, i: (0,) * a.ndim)
    mod = pl.BlockSpec((None, 1, D), lambda b, i: (b, 0, 0))
    tile = lambda n: pl.BlockSpec((None, tm, n), lambda b, i: (b, i, 0))
    return pl.pallas_call(
        _inproj_kernel,
        grid=(B, S // tm),
        in_specs=[tile(D), mod, mod, full(g_pre), full(wf), full(wg), full(wgate), full(ws),
                  full(bgate), full(bsm), full(alog_row)],
        out_specs=[tile(3 * FOX_W), tile(3 * GDN_KW), tile(GDN_VW), tile(2 * D), tile(LANES)],
        out_shape=[jax.ShapeDtypeStruct((B, S, 3 * FOX_W), BF16),
                   jax.ShapeDtypeStruct((B, S, 3 * GDN_KW), F32),
                   jax.ShapeDtypeStruct((B, S, GDN_VW), F32),
                   jax.ShapeDtypeStruct((B, S, 2 * D), F32),
                   jax.ShapeDtypeStruct((B, S, LANES), F32)],
        compiler_params=_params(("arbitrary", "arbitrary")),
        name="inproj",
    )(x, sc1, sh1, g_pre, wf, wg, wgate, ws, bgate, bsm, alog_row)


def _cumsum_kernel(s_ref, o_ref, carry_ref):
    @pl.when(pl.program_id(1) == 0)
    def _():
        carry_ref[...] = jnp.zeros_like(carry_ref)

    v = s_ref[...]
    n = v.shape[0]
    r = lax.broadcasted_iota(jnp.int32, (n, n), 0)
    c = lax.broadcasted_iota(jnp.int32, (n, n), 1)
    tri = jnp.where(r >= c, 1.0, 0.0).astype(F32)
    cs = _dot(tri, v, HIGHEST) + carry_ref[...]
    o_ref[...] = cs
    carry_ref[...] = cs[n - 1:n, :]


def _cumsum(small, tc):
    B, S, L = small.shape
    return pl.pallas_call(
        _cumsum_kernel,
        grid=(B, S // tc),
        in_specs=[pl.BlockSpec((None, tc, L), lambda b, i: (b, i, 0))],
        out_specs=pl.BlockSpec((None, tc, L), lambda b, i: (b, i, 0)),
        out_shape=jax.ShapeDtypeStruct((B, S, L), F32),
        scratch_shapes=[pltpu.VMEM((1, L), F32)],
        compiler_params=_params(("arbitrary", "arbitrary")),
        name="cumsum",
    )(small)


def _split3(x):
    p1 = x.astype(BF16).astype(F32)
    r = x - p1
    p2 = r.astype(BF16).astype(F32)
    p3 = (r - p2).astype(BF16).astype(F32)
    return p1, p2, p3


def _fox_kernel(q_ref, k_ref, v_ref, cum_ref, o_ref, kmod_ref, vmod_ref, m_ref, acc_ref, *, tq):
    hp = pl.program_id(1)
    i = pl.program_id(2)
    S = k_ref.shape[0]
    lane = lax.broadcasted_iota(jnp.int32, (1, LANES), 1)

    def head_col(tile, h):
        return jnp.sum(jnp.where(lane == h, tile, 0.0), axis=1, keepdims=True)

    def with_bias(base, hh, pieces, ones_first):
        sp = (1 - hh) * FOX_DH
        in_head = (lane >= hh * FOX_DH) & (lane < (hh + 1) * FOX_DH)
        out = jnp.where(in_head, base, 0.0)
        po, oo = (3, 0) if ones_first else (0, 3)
        for t in range(3):
            out = jnp.where(lane == sp + po + t, pieces[t], out)
            out = jnp.where(lane == sp + oo + t, 1.0, out)
        return out.astype(BF16)

    @pl.when(i == 0)
    def _():
        def chunk(ci, carry):
            start = pl.multiple_of(ci * tq, tq)
            k2 = k_ref[pl.ds(start, tq), :].astype(F32)
            v2 = v_ref[pl.ds(start, tq), :].astype(F32)
            cum = cum_ref[pl.ds(start, tq), :]
            for hh in range(2):
                ck = head_col(cum, hp * 2 + hh)
                kmod_ref[hh, pl.ds(start, tq), :] = with_bias(k2, hh, _split3(-ck * LOG2E), True)
                in_head = (lane >= hh * FOX_DH) & (lane < (hh + 1) * FOX_DH)
                vm = jnp.where(in_head, v2, jnp.where(lane == (1 - hh) * FOX_DH, 1.0, 0.0))
                vmod_ref[hh, pl.ds(start, tq), :] = vm.astype(BF16)
            return carry
        lax.fori_loop(0, S // tq, chunk, 0)

    q2 = q_ref[...].astype(F32)
    cumq = cum_ref[pl.ds(pl.multiple_of(i * tq, tq), tq), :]
    qmods = [with_bias(q2, hh, _split3(head_col(cumq, hp * 2 + hh) * LOG2E), False) for hh in range(2)]
    rr = lax.broadcasted_iota(jnp.int32, (tq // FOX_ROW_SPLIT, tq), 0)
    cc = lax.broadcasted_iota(jnp.int32, (tq // FOX_ROW_SPLIT, tq), 1)
    m_ref[...] = jnp.full(m_ref.shape, NEG_BIG, F32)
    acc_ref[...] = jnp.zeros(acc_ref.shape, F32)

    RS = FOX_ROW_SPLIT
    tr = tq // RS

    def step(j, masked):
        start = pl.multiple_of(j * tq, tq)
        chains = [(hh, r) for hh in range(2) for r in range(RS)]
        scores = [_dot_nt(qmods[hh][r * tr:(r + 1) * tr], kmod_ref[hh, pl.ds(start, tq), :]) for hh, r in chains]
        for (hh, r), s in zip(chains, scores):
            rows = slice(r * tr, (r + 1) * tr)
            if masked:
                s = jnp.where(rr + r * tr >= cc, s, NEG_BIG)
            m_prev = m_ref[hh, rows, :]
            m_new = jnp.maximum(m_prev, jnp.max(s, axis=1, keepdims=True))
            p = jnp.exp2(s - jnp.concatenate([m_new] * (tq // LANES), axis=1)).astype(BF16)
            acc_ref[hh, rows, :] = (jnp.exp2(m_prev - m_new) * acc_ref[hh, rows, :]
                                    + _dot(p, vmod_ref[hh, pl.ds(start, tq), :]))
            m_ref[hh, rows, :] = m_new

    def body(j, carry):
        step(j, False)
        return carry

    lax.fori_loop(0, i, body, 0)
    step(i, True)
    outs = []
    for hh in range(2):
        acc = acc_ref[hh]
        outs.append(acc / head_col(acc, (1 - hh) * FOX_DH))
    o_ref[...] = jnp.where(lane < FOX_DH, outs[0], outs[1]).astype(o_ref.dtype)


def _fox(fqkv, cum, tq):
    B, S, _ = fqkv.shape
    nhp = FOX_HEADS // 2
    return pl.pallas_call(
        functools.partial(_fox_kernel, tq=tq),
        grid=(B, nhp, S // tq),
        in_specs=[pl.BlockSpec((None, tq, LANES), lambda b, hp, i: (b, i, hp)),
                  pl.BlockSpec((None, S, LANES), lambda b, hp, i: (b, 0, nhp + hp)),
                  pl.BlockSpec((None, S, LANES), lambda b, hp, i: (b, 0, 2 * nhp + hp)),
                  pl.BlockSpec((None, S, LANES), lambda b, hp, i: (b, 0, 0))],
        out_specs=pl.BlockSpec((None, tq, LANES), lambda b, hp, i: (b, i, hp)),
        out_shape=jax.ShapeDtypeStruct((B, S, FOX_W), BF16),
        scratch_shapes=[pltpu.VMEM((2, S, LANES), BF16), pltpu.VMEM((2, S, LANES), BF16),
                        pltpu.VMEM((2, tq, LANES), F32), pltpu.VMEM((2, tq, LANES), F32)],
        compiler_params=_params(("arbitrary", "arbitrary", "arbitrary")),
        name="fox",
    )(fqkv, fqkv, fqkv, cum)


def _level_masks(n):
    r = lax.broadcasted_iota(jnp.int32, (n, n), 0)
    c = lax.broadcasted_iota(jnp.int32, (n, n), 1)
    masks = []
    s = 1
    while s < n:
        sel = ((r // (2 * s)) == (c // (2 * s))) & (((r // s) % 2) == 1) & (((c // s) % 2) == 0)
        masks.append(jnp.where(sel, 1.0, 0.0).astype(F32))
        s *= 2
    return masks


def _unit_lower_inverses(lows, masks):
    n = lows[0].shape[0]
    r = lax.broadcasted_iota(jnp.int32, (n, n), 0)
    c = lax.broadcasted_iota(jnp.int32, (n, n), 1)
    xs = [jnp.where(r == c, 1.0, 0.0).astype(F32)] * len(lows)
    for mask in masks:
        xbs = [x.astype(BF16) for x in xs]
        xcs = [_dot(xb, (low * mask).astype(BF16)) for xb, low in zip(xbs, lows)]
        xs = [x - _dot(xc.astype(BF16), xb) for x, xc, xb in zip(xs, xcs, xbs)]
    return xs


def _gdn_kernel(x_ref, gz_ref, sm_ref, cw_ref, nw_ref, o_ref, state_ref, carry_ref):
    ci = pl.program_id(0)
    NB = x_ref.shape[0]
    b0 = pl.program_id(1) * NB
    C = GDN_CHUNK

    @pl.when(ci == 0)
    def _():
        for bb in range(NB):
            state_ref[b0 + bb] = jnp.zeros(state_ref.shape[1:], F32)
            carry_ref[b0 + bb] = jnp.zeros(carry_ref.shape[1:], F32)

    row8 = lax.broadcasted_iota(jnp.int32, (8, 1), 0)
    r = lax.broadcasted_iota(jnp.int32, (C, C), 0)
    c = lax.broadcasted_iota(jnp.int32, (C, C), 1)
    causal = r >= c
    strict = r > c
    tri = jnp.where(causal, 1.0, 0.0).astype(F32)
    nw = nw_ref[...]
    masks = _level_masks(C)

    ys, sms, gcums, gcum_ts = [], [], [], []
    for bb in range(NB):
        x = x_ref[bb]
        prev = carry_ref[b0 + bb]
        y = cw_ref[CONV_W - 1:CONV_W, :] * x
        for j in range(1, CONV_W):
            xr = pltpu.roll(x, j, 0)
            pr = pltpu.roll(prev, j, 0)
            first = jnp.where(row8 < j, pr, xr[0:8, :])
            y = y + cw_ref[CONV_W - 1 - j:CONV_W - j, :] * jnp.concatenate([first, xr[8:, :]], axis=0)
        carry_ref[b0 + bb] = x[C - 8:C, :]
        ys.append(_silu(y))
        sms.append(sm_ref[bb])
        gcums.append(_dot(tri, sms[bb], HIGHEST))
        gcum_ts.append(gcums[bb].T)

    streams = [(bb, h) for bb in range(NB) for h in range(GDN_HEADS)]
    idx = range(len(streams))
    qs, ks, vs, betas, g_cols, g_lasts, decays = [], [], [], [], [], [], []
    for bb, h in streams:
        y, sm, gcum = ys[bb], sms[bb], gcums[bb]
        q = y[:, h * GDN_DK:(h + 1) * GDN_DK]
        k = y[:, GDN_KW + h * GDN_DK:GDN_KW + (h + 1) * GDN_DK]
        qs.append(q * lax.rsqrt(jnp.sum(q * q, axis=1, keepdims=True) + EPS) * (GDN_DK ** -0.5))
        ks.append(k * lax.rsqrt(jnp.sum(k * k, axis=1, keepdims=True) + EPS))
        vs.append(y[:, 2 * GDN_KW + h * GDN_DV:2 * GDN_KW + (h + 1) * GDN_DV])
        betas.append(sm[:, FOX_HEADS + h:FOX_HEADS + h + 1])
        gl = FOX_HEADS + GDN_HEADS + h
        g_cols.append(gcum[:, gl:gl + 1])
        g_lasts.append(gcum[C - 1:C, gl:gl + 1])
        diff = g_cols[-1] - gcum_ts[bb][gl:gl + 1, :]
        decays.append(jnp.where(causal, jnp.exp(jnp.where(causal, diff, 0.0)), 0.0))
    kbs = [ks[i] * betas[i] for i in idx]
    egs = [jnp.exp(g_cols[i]) for i in idx]
    kqs = [_dot_nt(jnp.concatenate([kbs[i], qs[i]], axis=0).astype(BF16), ks[i].astype(BF16)) for i in idx]
    lows = [jnp.where(strict, kqs[i][:C] * decays[i], 0.0) for i in idx]
    attns = [(kqs[i][C:] * decays[i]).astype(BF16) for i in idx]
    tinvs = _unit_lower_inverses(lows, masks)
    uws = [_dot(tinvs[i].astype(BF16),
                jnp.concatenate([vs[i] * betas[i], kbs[i] * egs[i]], axis=1).astype(BF16)) for i in idx]
    sts = [state_ref[b0 + bb, h] for bb, h in streams]
    wqs = [_dot(jnp.concatenate([uws[i][:, GDN_DV:], qs[i] * egs[i]], axis=0).astype(BF16),
                sts[i].astype(BF16)) for i in idx]
    v_news = [(uws[i][:, :GDN_DV] - wqs[i][:C]).astype(BF16) for i in idx]
    k_dec_ts = [(ks[i] * jnp.exp(g_lasts[i] - g_cols[i])).T.astype(BF16) for i in idx]
    os_ = [wqs[i][C:] + _dot(attns[i], v_news[i]) for i in idx]
    new_sts = [sts[i] * jnp.exp(g_lasts[i]) + _dot(k_dec_ts[i], v_news[i]) for i in idx]
    for i, (bb, h) in enumerate(streams):
        state_ref[b0 + bb, h] = new_sts[i]
        o = _rms(os_[i], nw) * _silu(gz_ref[bb, :, h * GDN_DV:(h + 1) * GDN_DV])
        o_ref[bb, :, h * GDN_DV:(h + 1) * GDN_DV] = o.astype(o_ref.dtype)


def _gdn(gqkv, gz, small, conv_w, norm_w):
    B, S, _ = gqkv.shape
    C = GDN_CHUNK
    nb = _pick(B, 4)
    tile = lambda n: pl.BlockSpec((nb, C, n), lambda ci, b: (b, ci, 0))
    full = lambda a: pl.BlockSpec(a.shape, lambda ci, b: (0,) * a.ndim)
    return pl.pallas_call(
        _gdn_kernel,
        grid=(S // C, B // nb),
        in_specs=[tile(3 * GDN_KW), tile(GDN_VW), tile(LANES), full(conv_w), full(norm_w)],
        out_specs=tile(GDN_VW),
        out_shape=jax.ShapeDtypeStruct((B, S, GDN_VW), BF16),
        scratch_shapes=[pltpu.VMEM((B, GDN_HEADS, GDN_DK, GDN_DV), F32),
                        pltpu.VMEM((B, 8, 3 * GDN_KW), F32)],
        compiler_params=_params(("arbitrary", "arbitrary")),
        name="gdn",
    )(gqkv, gz, small, conv_w, norm_w)


def _outproj_kernel(ya_ref, yb_ref, gates_ref, x_ref, gt1_ref, sc2_ref, sh2_ref, gpm_ref, gpf_ref,
                    wa_ref, wb_ref, wo_ref, wq_ref, x1_ref, h2t_ref, qy_ref):
    ya = _dot(ya_ref[...], wa_ref[...])
    yb = _dot(yb_ref[...], wb_ref[...])
    gates = gates_ref[...]
    mix_in = gates[:, :D_MODEL] * ya + gates[:, D_MODEL:] * yb
    mix = _dot(mix_in.astype(BF16), wo_ref[...])
    x1 = x_ref[...] + gt1_ref[...] * _rms(mix, gpm_ref[...])
    x1_ref[...] = x1
    h2 = _rms(x1, gpf_ref[...]) * (1.0 + sc2_ref[...]) + sh2_ref[...]
    h2t_ref[...] = h2.T.astype(BF16)
    qy_ref[...] = _dot(h2.astype(BF16), wq_ref[...])


def _outproj(ya, yb, gates, x, gt1, sc2, sh2, gpm, gpf, wa, wb, wo, wq, tm):
    B, S, D = x.shape
    full = lambda a: pl.BlockSpec(a.shape, lambda b, i: (0,) * a.ndim)
    mod = pl.BlockSpec((None, 1, D), lambda b, i: (b, 0, 0))
    tile = lambda n: pl.BlockSpec((None, tm, n), lambda b, i: (b, i, 0))
    nq = wq.shape[1]
    return pl.pallas_call(
        _outproj_kernel,
        grid=(B, S // tm),
        in_specs=[tile(FOX_W), tile(GDN_VW), tile(2 * D), tile(D), mod, mod, mod, full(gpm), full(gpf),
                  full(wa), full(wb), full(wo), full(wq)],
        out_specs=[tile(D), pl.BlockSpec((D, tm), lambda b, i: (0, b * (S // tm) + i)), tile(nq)],
        out_shape=[jax.ShapeDtypeStruct((B, S, D), F32),
                   jax.ShapeDtypeStruct((D, B * S), BF16),
                   jax.ShapeDtypeStruct((B, S, nq), F32)],
        compiler_params=_params(("arbitrary", "arbitrary")),
        name="outproj",
    )(ya, yb, gates, x, gt1, sc2, sh2, gpm, gpf, wa, wb, wo, wq)


SUBLANES = 8


def _cmpx(a, i, l, desc):
    hi, lo = jnp.maximum(a[i], a[l]), jnp.minimum(a[i], a[l])
    a[i], a[l] = (hi, lo) if desc else (lo, hi)


def _top16_desc(a):
    a = list(a)
    n = len(a)
    k = 2
    while k <= n:
        j = k // 2
        while j >= 1:
            for i in range(n):
                l = i ^ j
                if l > i:
                    _cmpx(a, i, l, (i & k) == 0)
            j //= 2
        k *= 2
    shift = SUBLANES // 2
    while shift >= 1:
        other = [pltpu.roll(v, shift, 0) for v in a]
        a = [jnp.maximum(a[i], other[n - 1 - i]) for i in range(n)]
        j = n // 2
        while j >= 1:
            for i in range(n):
                l = i ^ j
                if l > i:
                    _cmpx(a, i, l, True)
            j //= 2
        shift //= 2
    return a


def _topk_kernel(qy_ref, k1_ref, k2_ref, theta_ref, e1_ref, s2_ref, m2_ref):
    K = PEER_TOPK
    half = PEER_DQ // 2
    tt = qy_ref.shape[0]
    G = N_KEYS // SUBLANES
    sub = lax.broadcasted_iota(jnp.int32, (SUBLANES, LANES), 0)

    def stagger(d, base):
        out = d[base]
        for r in range(1, SUBLANES):
            out = jnp.where(sub == r, d[base + r], out)
        return out

    def head(h, carry):
        col = pl.multiple_of(h * PEER_DQ, PEER_DQ)
        q1 = qy_ref[:, pl.ds(col, half)]
        q2 = qy_ref[:, pl.ds(col + half, half)]
        s1 = _dot_nt(k1_ref[...], q1, HIGHEST)
        s2 = _dot_nt(k2_ref[...], q2, HIGHEST)
        for c in range(tt // LANES):
            cs = slice(c * LANES, (c + 1) * LANES)
            s2_ref[h, c] = s2[:, cs]
            s1g = [s1[SUBLANES * g:SUBLANES * (g + 1), cs] for g in range(G)]
            d1 = _top16_desc(s1g)
            d2 = _top16_desc([s2[SUBLANES * g:SUBLANES * (g + 1), cs] for g in range(G)])
            d2_lo, d2_hi, d1_hi = stagger(d2, 0), stagger(d2, SUBLANES), stagger(d1, SUBLANES)
            cand = [d1[0] + d2_lo, d1[0] + d2_hi, d1_hi + d2[0]] + [d1[i] + d2_lo for i in range(1, SUBLANES)]
            cand += [jnp.full((SUBLANES, LANES), -jnp.inf, F32)] * (K - len(cand))
            top = _top16_desc(cand)
            z = jnp.ones_like(top[0])
            for i in range(1, K):
                z = z + jnp.exp(top[i] - top[0])
            zinv = 1.0 / z
            tau = top[K - 1]
            m2_ref[h, :, cs] = d2[0][0:1, :]
            for g in range(G):
                theta = jnp.full((SUBLANES, LANES), jnp.inf, F32)
                for j in range(K):
                    theta = jnp.where(s1g[g] + d2[j] >= tau, d2[j], theta)
                theta_ref[h, c, SUBLANES * g:SUBLANES * (g + 1), :] = theta
                e1_ref[h, c, SUBLANES * g:SUBLANES * (g + 1), :] = jnp.exp(s1g[g] - d1[0]) * zinv
        return carry

    lax.fori_loop(0, PEER_HEADS, head, 0)


def _topk(qy, keys1, keys2, tt):
    T, nq = qy.shape
    H = PEER_HEADS
    full = lambda a: pl.BlockSpec(a.shape, lambda i: (0,) * a.ndim)
    big = pl.BlockSpec((H, tt // LANES, N_KEYS, LANES), lambda i: (0, i, 0, 0))
    return pl.pallas_call(
        _topk_kernel,
        grid=(T // tt,),
        in_specs=[pl.BlockSpec((tt, nq), lambda i: (i, 0)), full(keys1), full(keys2)],
        out_specs=[big, big, big, pl.BlockSpec((H, 1, tt), lambda i: (0, 0, i))],
        out_shape=[jax.ShapeDtypeStruct((H, T // LANES, N_KEYS, LANES), F32)] * 3
        + [jax.ShapeDtypeStruct((H, 1, T), F32)],
        compiler_params=_params(("arbitrary",)),
        name="topk",
    )(qy, keys1, keys2)


def _peer_kernel(h2t_ref, u_ref, vt_ref, theta_ref, e1_ref, s2_ref, m2_ref, x1_ref, gt2_ref, g_ref, o_ref,
                 acc_ref, e2_ref, st_ref, wt_ref, *, na):
    j = pl.program_id(1)
    H = PEER_HEADS
    tt = h2t_ref.shape[1]
    vreg = (SUBLANES, LANES)
    ncol = tt // LANES
    nkb = N_KEYS // GATE_KB
    blk = (GATE_KB // SUBLANES, SUBLANES, LANES)

    @pl.when(j == 0)
    def _():
        acc_ref[...] = jnp.zeros(acc_ref.shape, F32)
        for h in range(H):
            for c in range(ncol):
                e2_ref[h, c] = jnp.exp(s2_ref[h, c] - m2_ref[h, :, c * LANES:(c + 1) * LANES])

    st = _dot(u_ref[...], h2t_ref[...])
    for c in range(ncol):
        st_ref[c] = st[:, c * LANES:(c + 1) * LANES].reshape(na, nkb, GATE_KB, LANES)

    def gate_blocks(a0, nab):
        def block(i, carry):
            c = lax.shift_right_logical(i, nkb.bit_length() - 1)
            kb = lax.bitwise_and(i, nkb - 1)
            was = [jnp.zeros(blk, F32) for _ in range(nab)]
            for h in range(H):
                s2 = s2_ref[h, c, kb].reshape(blk)
                e2 = e2_ref[h, c, kb].reshape(blk)
                for t in range(nab):
                    thb = jnp.broadcast_to(theta_ref[h, c, a0 + t:a0 + t + 1, :], vreg)
                    e1b = jnp.broadcast_to(e1_ref[h, c, a0 + t:a0 + t + 1, :], vreg)
                    was[t] = was[t] + jnp.where(s2 >= thb, e1b * e2, 0.0)
            for t in range(nab):
                sv = st_ref[c, a0 + t, kb].reshape(blk)
                act = 0.5 * sv * (1.0 + lax.erf(sv * (0.5 ** 0.5)))
                wt_ref[c, a0 + t, kb] = (was[t] * act).reshape(GATE_KB, LANES).astype(BF16)
            return carry

        lax.fori_loop(0, ncol * nkb, block, 0)

    for a0 in range(0, na, GATE_AB):
        gate_blocks(a0, GATE_AB)
    gated = jnp.concatenate([wt_ref[c].reshape(na * N_KEYS, LANES) for c in range(ncol)], axis=1)
    acc_ref[...] += _dot(vt_ref[...], gated)

    @pl.when(j == pl.num_programs(1) - 1)
    def _():
        y = acc_ref[...].T
        o_ref[...] = x1_ref[...] + gt2_ref[...] * _rms(y, g_ref[...])


def _peer(h2t, u_b, vt_b, theta, e1, s2, m2, x1, gt2, g_post, tt, ec, tiles_per_batch):
    D, T = h2t.shape
    E = u_b.shape[0]
    H = PEER_HEADS
    na = ec // N_KEYS
    ncol = tt // LANES
    nkb = N_KEYS // GATE_KB
    s2 = s2.reshape(H, T // LANES, nkb, GATE_KB, LANES)
    rows = pl.BlockSpec((H, ncol, na, LANES), lambda i, j: (0, i, j, 0))
    slab = (ncol, na, nkb, GATE_KB, LANES)
    return pl.pallas_call(
        functools.partial(_peer_kernel, na=na),
        grid=(T // tt, E // ec),
        in_specs=[pl.BlockSpec((D, tt), lambda i, j: (0, i)),
                  pl.BlockSpec((ec, D), lambda i, j: (j, 0)),
                  pl.BlockSpec((D, ec), lambda i, j: (0, j)),
                  rows, rows,
                  pl.BlockSpec((H, ncol, nkb, GATE_KB, LANES), lambda i, j: (0, i, 0, 0, 0)),
                  pl.BlockSpec((H, 1, tt), lambda i, j: (0, 0, i)),
                  pl.BlockSpec((tt, D), lambda i, j: (i, 0)),
                  pl.BlockSpec((None, 1, D), lambda i, j: (i // tiles_per_batch, 0, 0)),
                  pl.BlockSpec(g_post.shape, lambda i, j: (0, 0))],
        out_specs=pl.BlockSpec((tt, D), lambda i, j: (i, 0)),
        out_shape=jax.ShapeDtypeStruct((T, D), F32),
        scratch_shapes=[pltpu.VMEM((D, tt), F32), pltpu.VMEM((H, ncol, nkb, GATE_KB, LANES), F32),
                        pltpu.VMEM(slab, F32), pltpu.VMEM(slab, BF16)],
        compiler_params=_params(("arbitrary", "arbitrary")),
        name="peer",
    )(h2t, u_b, vt_b, theta, e1, s2, m2, x1, gt2, g_post)


def _pick(n, pref):
    t = pref
    while n % t:
        t //= 2
    return t


def _layer(x, c_pad, w_ada, b_ada, g_pre_mix, g_post_mix, w_in, b_fgate, fox_w_o, conv_w, a_log, dt_bias,
           gdn_norm_w, gdn_w_o, b_branch_gate, w_out, g_pre_ffn, g_post_ffn, peer_w_query, peer_keys1,
           peer_keys2, peer_u, peer_v):
    B, S, D = x.shape
    T = B * S
    mod = _ada(c_pad, w_ada, b_ada.reshape(1, -1))[:B]
    sh1, sc1, gt1, sh2, sc2, gt2 = [m.reshape(B, 1, D) for m in jnp.split(mod, 6, axis=-1)]

    off = [0]
    for s in SPLITS:
        off.append(off[-1] + s)
    col = lambda i, j: w_in[:, off[i]:off[j]]
    wf = col(0, 3).astype(BF16)
    wg = jnp.concatenate([col(4, 8)], axis=1).astype(BF16)
    wgate = col(10, 12).astype(BF16)
    pad = LANES - (FOX_HEADS + 2 * GDN_HEADS)
    ws = jnp.concatenate([col(3, 4), col(8, 10), jnp.zeros((D, pad), F32)], axis=1).astype(BF16)
    zpad = jnp.zeros((pad,), F32)
    bsm = jnp.concatenate([b_fgate, jnp.zeros((GDN_HEADS,), F32), dt_bias, zpad]).reshape(1, LANES)
    alog_row = jnp.concatenate([jnp.zeros((FOX_HEADS + GDN_HEADS,), F32), a_log, zpad]).reshape(1, LANES)
    bgate = b_branch_gate.reshape(1, 2 * D)

    tm = _pick(S, 512)
    fqkv, gqkv, gz, gates, small = _inproj(x, sc1, sh1, g_pre_mix.reshape(1, D), wf, wg, wgate, ws, bgate,
                                           bsm, alog_row, tm)
    cum = _cumsum(small, _pick(S, 512))
    ya = _fox(fqkv, cum, _pick(S, 512))
    yb = _gdn(gqkv, gz, small, conv_w, gdn_norm_w.reshape(1, GDN_DV))
    x1, h2t, qy = _outproj(ya, yb, gates, x, gt1, sc2, sh2, g_post_mix.reshape(1, D), g_pre_ffn.reshape(1, D),
                          fox_w_o.astype(BF16), gdn_w_o.astype(BF16), w_out.astype(BF16),
                          peer_w_query.astype(BF16), tm)
    qy = qy.reshape(T, -1)
    theta, e1, s2, m2 = _topk(qy, peer_keys1, peer_keys2, _pick(T, 256))
    tt = _pick(S, 512)
    out = _peer(h2t, peer_u.astype(BF16), peer_v.T.astype(BF16), theta, e1, s2, m2,
                x1.reshape(T, D), gt2, g_post_ffn.reshape(1, D), tt, 2048, S // tt)
    return out.reshape(B, S, D)


def kernel(x, c, w_ada, b_ada, g_pre_mix, g_post_mix, w_in, b_fgate, fox_w_o, conv_w, a_log, dt_bias,
           gdn_norm_w, gdn_w_o, b_branch_gate, w_out, g_pre_ffn, g_post_ffn, peer_w_query, peer_keys1,
           peer_keys2, peer_u, peer_v):
    B = x.shape[0]
    c_pad = jnp.pad(c, ((0, (-B) % 8), (0, 0)))
    for l in range(w_ada.shape[0]):
        x = _layer(x, c_pad, w_ada[l], b_ada[l], g_pre_mix[l], g_post_mix[l], w_in[l], b_fgate[l],
                   fox_w_o[l], conv_w[l], a_log[l], dt_bias[l], gdn_norm_w[l], gdn_w_o[l], b_branch_gate[l],
                   w_out[l], g_pre_ffn[l], g_post_ffn[l], peer_w_query[l], peer_keys1[l], peer_keys2[l],
                   peer_u[l], peer_v[l])
    return x
```

```python
import functools
import math

import jax
import jax.numpy as jnp
from jax import lax
from jax.experimental import pallas as pl
from jax.experimental.pallas import tpu as pltpu

F32 = jnp.float32
BF16 = jnp.bfloat16
HIGHEST = lax.Precision.HIGHEST

D_MODEL = 1024
EPS = 1e-6
FOX_HEADS = 8
FOX_DH = 64
GDN_HEADS = 4
GDN_DK = 128
GDN_DV = 128
CONV_W = 4
PEER_HEADS = 8
N_KEYS = 128
PEER_DQ = 256
PEER_TOPK = 16
FOX_W = FOX_HEADS * FOX_DH
GDN_KW = GDN_HEADS * GDN_DK
GDN_VW = GDN_HEADS * GDN_DV
SPLITS = (FOX_W, FOX_W, FOX_W, FOX_HEADS, GDN_KW, GDN_KW, GDN_VW, GDN_VW, GDN_HEADS, GDN_HEADS,
          D_MODEL, D_MODEL)

LANES = 128
GDN_CHUNK = 128
FOX_ROW_SPLIT = 1
GATE_KB = 16
GATE_AB = 8
NEG_BIG = -1e30
LOG2E = 1.4426950408889634
VMEM_LIMIT = 56 * 1024 * 1024


def _params(sem, flags=None):
    return pltpu.CompilerParams(dimension_semantics=sem, vmem_limit_bytes=VMEM_LIMIT, flags=flags)


def _dot(a, b, precision=None):
    return jnp.dot(a, b, preferred_element_type=F32, precision=precision)


def _dot_nt(a, b, precision=None):
    return lax.dot_general(a, b, (((1,), (1,)), ((), ())), preferred_element_type=F32,
                           precision=precision)


def _sigmoid(x):
    return 1.0 / (1.0 + jnp.exp(-x))


def _silu(x):
    return x * _sigmoid(x)


def _softplus(x):
    return jnp.maximum(x, 0.0) + jnp.log(1.0 + jnp.exp(-jnp.abs(x)))


def _log_sigmoid(x):
    return -_softplus(-x)


def _rms(x, g):
    return x * lax.rsqrt(jnp.mean(x * x, axis=-1, keepdims=True) + EPS) * g


def _ada_kernel(c_ref, w_ref, b_ref, o_ref):
    c = c_ref[...]
    o_ref[...] = _dot(_silu(c), w_ref[...], HIGHEST) + b_ref[...]


def _ada(c_pad, w, b):
    n = w.shape[1]
    bn = 1024
    return pl.pallas_call(
        _ada_kernel,
        grid=(n // bn,),
        in_specs=[pl.BlockSpec(c_pad.shape, lambda j: (0, 0)),
                  pl.BlockSpec((D_MODEL, bn), lambda j: (0, j)),
                  pl.BlockSpec((1, bn), lambda j: (0, j))],
        out_specs=pl.BlockSpec((c_pad.shape[0], bn), lambda j: (0, j)),
        out_shape=jax.ShapeDtypeStruct((c_pad.shape[0], n), F32),
        compiler_params=_params(("arbitrary",)),
        name="ada",
    )(c_pad, w, b)


def _inproj_kernel(x_ref, sc_ref, sh_ref, g_ref, wf_ref, wg_ref, wgate_ref, ws_ref, bgate_ref,
                   bsm_ref, alog_ref, fqkv_ref, gqkv_ref, gz_ref, gates_ref, small_ref):
    x = x_ref[...]
    h = _rms(x, g_ref[...]) * (1.0 + sc_ref[...]) + sh_ref[...]
    hb = h.astype(BF16)
    pf = _dot(hb, wf_ref[...])
    lane = lax.broadcasted_iota(jnp.int32, (1, 3 * FOX_W), 1)
    pf = jnp.where(lane < FOX_W, pf * (FOX_DH ** -0.5 * LOG2E), pf)
    fqkv_ref[...] = pf.astype(BF16)
    pg = _dot(hb, wg_ref[...])
    gqkv_ref[...] = pg[:, :3 * GDN_KW]
    gz_ref[...] = pg[:, 3 * GDN_KW:]
    gates_ref[...] = _sigmoid(_dot(hb, wgate_ref[...]) + bgate_ref[...])
    z = _dot(hb, ws_ref[...]) + bsm_ref[...]
    sl = lax.broadcasted_iota(jnp.int32, (1, LANES), 1)
    neg_a = -jnp.exp(alog_ref[...])
    small_ref[...] = jnp.where(sl < FOX_HEADS, _log_sigmoid(z),
                               jnp.where(sl < FOX_HEADS + GDN_HEADS, _sigmoid(z), neg_a * _softplus(z)))


def _inproj(x, sc1, sh1, g_pre, wf, wg, wgate, ws, bgate, bsm, alog_row, tm):
    B, S, D = x.shape
    full = lambda a: pl.BlockSpec(a.shape, lambda b, i: (0,) * a.ndim)
    mod = pl.BlockSpec((None, 1, D), lambda b, i: (b, 0, 0))
    tile = lambda n: pl.BlockSpec((None, tm, n), lambda b, i: (b, i, 0))
    return pl.pallas_call(
        _inproj_kernel,
        grid=(B, S // tm),
        in_specs=[tile(D), mod, mod, full(g_pre), full(wf), full(wg), full(wgate), full(ws),
                  full(bgate), full(bsm), full(alog_row)],
        out_specs=[tile(3 * FOX_W), tile(3 * GDN_KW), tile(GDN_VW), tile(2 * D), tile(LANES)],
        out_shape=[jax.ShapeDtypeStruct((B, S, 3 * FOX_W), BF16),
                   jax.ShapeDtypeStruct((B, S, 3 * GDN_KW), F32),
                   jax.ShapeDtypeStruct((B, S, GDN_VW), F32),
                   jax.ShapeDtypeStruct((B, S, 2 * D), F32),
                   jax.ShapeDtypeStruct((B, S, LANES), F32)],
        compiler_params=_params(("arbitrary", "arbitrary")),
        name="inproj",
    )(x, sc1, sh1, g_pre, wf, wg, wgate, ws, bgate, bsm, alog_row)


def _cumsum_kernel(s_ref, o_ref, carry_ref):
    @pl.when(pl.program_id(1) == 0)
    def _():
        carry_ref[...] = jnp.zeros_like(carry_ref)

    v = s_ref[...]
    n = v.shape[0]
    r = lax.broadcasted_iota(jnp.int32, (n, n), 0)
    c = lax.broadcasted_iota(jnp.int32, (n, n), 1)
    tri = jnp.where(r >= c, 1.0, 0.0).astype(F32)
    cs = _dot(tri, v, HIGHEST) + carry_ref[...]
    o_ref[...] = cs
    carry_ref[...] = cs[n - 1:n, :]


def _cumsum(small, tc):
    B, S, L = small.shape
    return pl.pallas_call(
        _cumsum_kernel,
        grid=(B, S // tc),
        in_specs=[pl.BlockSpec((None, tc, L), lambda b, i: (b, i, 0))],
        out_specs=pl.BlockSpec((None, tc, L), lambda b, i: (b, i, 0)),
        out_shape=jax.ShapeDtypeStruct((B, S, L), F32),
        scratch_shapes=[pltpu.VMEM((1, L), F32)],
        compiler_params=_params(("arbitrary", "arbitrary")),
        name="cumsum",
    )(small)


def _split3(x):
    p1 = x.astype(BF16).astype(F32)
    r = x - p1
    p2 = r.astype(BF16).astype(F32)
    p3 = (r - p2).astype(BF16).astype(F32)
    return p1, p2, p3


def _fox_kernel(q_ref, k_ref, v_ref, cum_ref, o_ref, kmod_ref, vmod_ref, m_ref, acc_ref, *, tq):
    hp = pl.program_id(1)
    i = pl.program_id(2)
    S = k_ref.shape[0]
    lane = lax.broadcasted_iota(jnp.int32, (1, LANES), 1)

    def head_col(tile, h):
        return jnp.sum(jnp.where(lane == h, tile, 0.0), axis=1, keepdims=True)

    def with_bias(base, hh, pieces, ones_first):
        sp = (1 - hh) * FOX_DH
        in_head = (lane >= hh * FOX_DH) & (lane < (hh + 1) * FOX_DH)
        out = jnp.where(in_head, base, 0.0)
        po, oo = (3, 0) if ones_first else (0, 3)
        for t in range(3):
            out = jnp.where(lane == sp + po + t, pieces[t], out)
            out = jnp.where(lane == sp + oo + t, 1.0, out)
        return out.astype(BF16)

    @pl.when(i == 0)
    def _():
        def chunk(ci, carry):
            start = pl.multiple_of(ci * tq, tq)
            k2 = k_ref[pl.ds(start, tq), :].astype(F32)
            v2 = v_ref[pl.ds(start, tq), :].astype(F32)
            cum = cum_ref[pl.ds(start, tq), :]
            for hh in range(2):
                ck = head_col(cum, hp * 2 + hh)
                kmod_ref[hh, pl.ds(start, tq), :] = with_bias(k2, hh, _split3(-ck * LOG2E), True)
                in_head = (lane >= hh * FOX_DH) & (lane < (hh + 1) * FOX_DH)
                vm = jnp.where(in_head, v2, jnp.where(lane == (1 - hh) * FOX_DH, 1.0, 0.0))
                vmod_ref[hh, pl.ds(start, tq), :] = vm.astype(BF16)
            return carry
        lax.fori_loop(0, S // tq, chunk, 0)

    q2 = q_ref[...].astype(F32)
    cumq = cum_ref[pl.ds(pl.multiple_of(i * tq, tq), tq), :]
    qmods = [with_bias(q2, hh, _split3(head_col(cumq, hp * 2 + hh) * LOG2E), False) for hh in range(2)]
    rr = lax.broadcasted_iota(jnp.int32, (tq // FOX_ROW_SPLIT, tq), 0)
    cc = lax.broadcasted_iota(jnp.int32, (tq // FOX_ROW_SPLIT, tq), 1)
    m_ref[...] = jnp.full(m_ref.shape, NEG_BIG, F32)
    acc_ref[...] = jnp.zeros(acc_ref.shape, F32)

    RS = FOX_ROW_SPLIT
    tr = tq // RS

    def step(j, masked):
        start = pl.multiple_of(j * tq, tq)
        chains = [(hh, r) for hh in range(2) for r in range(RS)]
        scores = [_dot_nt(qmods[hh][r * tr:(r + 1) * tr], kmod_ref[hh, pl.ds(start, tq), :]) for hh, r in chains]
        for (hh, r), s in zip(chains, scores):
            rows = slice(r * tr, (r + 1) * tr)
            if masked:
                s = jnp.where(rr + r * tr >= cc, s, NEG_BIG)
            m_prev = m_ref[hh, rows, :]
            m_new = jnp.maximum(m_prev, jnp.max(s, axis=1, keepdims=True))
            p = jnp.exp2(s - jnp.concatenate([m_new] * (tq // LANES), axis=1)).astype(BF16)
            acc_ref[hh, rows, :] = (jnp.exp2(m_prev - m_new) * acc_ref[hh, rows, :]
                                    + _dot(p, vmod_ref[hh, pl.ds(start, tq), :]))
            m_ref[hh, rows, :] = m_new

    def body(j, carry):
        step(j, False)
        return carry

    lax.fori_loop(0, i, body, 0)
    step(i, True)
    outs = []
    for hh in range(2):
        acc = acc_ref[hh]
        outs.append(acc / head_col(acc, (1 - hh) * FOX_DH))
    o_ref[...] = jnp.where(lane < FOX_DH, outs[0], outs[1]).astype(o_ref.dtype)


def _fox(fqkv, cum, tq):
    B, S, _ = fqkv.shape
    nhp = FOX_HEADS // 2
    return pl.pallas_call(
        functools.partial(_fox_kernel, tq=tq),
        grid=(B, nhp, S // tq),
        in_specs=[pl.BlockSpec((None, tq, LANES), lambda b, hp, i: (b, i, hp)),
                  pl.BlockSpec((None, S, LANES), lambda b, hp, i: (b, 0, nhp + hp)),
                  pl.BlockSpec((None, S, LANES), lambda b, hp, i: (b, 0, 2 * nhp + hp)),
                  pl.BlockSpec((None, S, LANES), lambda b, hp, i: (b, 0, 0))],
        out_specs=pl.BlockSpec((None, tq, LANES), lambda b, hp, i: (b, i, hp)),
        out_shape=jax.ShapeDtypeStruct((B, S, FOX_W), BF16),
        scratch_shapes=[pltpu.VMEM((2, S, LANES), BF16), pltpu.VMEM((2, S, LANES), BF16),
                        pltpu.VMEM((2, tq, LANES), F32), pltpu.VMEM((2, tq, LANES), F32)],
        compiler_params=_params(("arbitrary", "arbitrary", "arbitrary")),
        name="fox",
    )(fqkv, fqkv, fqkv, cum)


def _level_masks(n):
    r = lax.broadcasted_iota(jnp.int32, (n, n), 0)
    c = lax.broadcasted_iota(jnp.int32, (n, n), 1)
    masks = []
    s = 1
    while s < n:
        sel = ((r // (2 * s)) == (c // (2 * s))) & (((r // s) % 2) == 1) & (((c // s) % 2) == 0)
        masks.append(jnp.where(sel, 1.0, 0.0).astype(F32))
        s *= 2
    return masks


def _unit_lower_inverses(lows, masks):
    n = lows[0].shape[0]
    r = lax.broadcasted_iota(jnp.int32, (n, n), 0)
    c = lax.broadcasted_iota(jnp.int32, (n, n), 1)
    xs = [jnp.where(r == c, 1.0, 0.0).astype(F32)] * len(lows)
    for mask in masks:
        xbs = [x.astype(BF16) for x in xs]
        xcs = [_dot(xb, (low * mask).astype(BF16)) for xb, low in zip(xbs, lows)]
        xs = [x - _dot(xc.astype(BF16), xb) for x, xc, xb in zip(xs, xcs, xbs)]
    return xs


def _gdn_kernel(x_ref, gz_ref, sm_ref, cw_ref, nw_ref, o_ref, state_ref, carry_ref):
    ci = pl.program_id(0)
    NB = x_ref.shape[0]
    b0 = pl.program_id(1) * NB
    C = GDN_CHUNK

    @pl.when(ci == 0)
    def _():
        for bb in range(NB):
            state_ref[b0 + bb] = jnp.zeros(state_ref.shape[1:], F32)
            carry_ref[b0 + bb] = jnp.zeros(carry_ref.shape[1:], F32)

    row8 = lax.broadcasted_iota(jnp.int32, (8, 1), 0)
    r = lax.broadcasted_iota(jnp.int32, (C, C), 0)
    c = lax.broadcasted_iota(jnp.int32, (C, C), 1)
    causal = r >= c
    strict = r > c
    tri = jnp.where(causal, 1.0, 0.0).astype(F32)
    nw = nw_ref[...]
    masks = _level_masks(C)

    ys, sms, gcums, gcum_ts = [], [], [], []
    for bb in range(NB):
        x = x_ref[bb]
        prev = carry_ref[b0 + bb]
        y = cw_ref[CONV_W - 1:CONV_W, :] * x
        for j in range(1, CONV_W):
            xr = pltpu.roll(x, j, 0)
            pr = pltpu.roll(prev, j, 0)
            first = jnp.where(row8 < j, pr, xr[0:8, :])
            y = y + cw_ref[CONV_W - 1 - j:CONV_W - j, :] * jnp.concatenate([first, xr[8:, :]], axis=0)
        carry_ref[b0 + bb] = x[C - 8:C, :]
        ys.append(_silu(y))
        sms.append(sm_ref[bb])
        gcums.append(_dot(tri, sms[bb], HIGHEST))
        gcum_ts.append(gcums[bb].T)

    streams = [(bb, h) for bb in range(NB) for h in range(GDN_HEADS)]
    idx = range(len(streams))
    qs, ks, vs, betas, g_cols, g_lasts, decays = [], [], [], [], [], [], []
    for bb, h in streams:
        y, sm, gcum = ys[bb], sms[bb], gcums[bb]
        q = y[:, h * GDN_DK:(h + 1) * GDN_DK]
        k = y[:, GDN_KW + h * GDN_DK:GDN_KW + (h + 1) * GDN_DK]
        qs.append(q * lax.rsqrt(jnp.sum(q * q, axis=1, keepdims=True) + EPS) * (GDN_DK ** -0.5))
        ks.append(k * lax.rsqrt(jnp.sum(k * k, axis=1, keepdims=True) + EPS))
        vs.append(y[:, 2 * GDN_KW + h * GDN_DV:2 * GDN_KW + (h + 1) * GDN_DV])
        betas.append(sm[:, FOX_HEADS + h:FOX_HEADS + h + 1])
        gl = FOX_HEADS + GDN_HEADS + h
        g_cols.append(gcum[:, gl:gl + 1])
        g_lasts.append(gcum[C - 1:C, gl:gl + 1])
        diff = g_cols[-1] - gcum_ts[bb][gl:gl + 1, :]
        decays.append(jnp.where(causal, jnp.exp(jnp.where(causal, diff, 0.0)), 0.0))
    kbs = [ks[i] * betas[i] for i in idx]
    egs = [jnp.exp(g_cols[i]) for i in idx]
    kqs = [_dot_nt(jnp.concatenate([kbs[i], qs[i]], axis=0).astype(BF16), ks[i].astype(BF16)) for i in idx]
    lows = [jnp.where(strict, kqs[i][:C] * decays[i], 0.0) for i in idx]
    attns = [(kqs[i][C:] * decays[i]).astype(BF16) for i in idx]
    tinvs = _unit_lower_inverses(lows, masks)
    uws = [_dot(tinvs[i].astype(BF16),
                jnp.concatenate([vs[i] * betas[i], kbs[i] * egs[i]], axis=1).astype(BF16)) for i in idx]
    sts = [state_ref[b0 + bb, h] for bb, h in streams]
    wqs = [_dot(jnp.concatenate([uws[i][:, GDN_DV:], qs[i] * egs[i]], axis=0).astype(BF16),
                sts[i].astype(BF16)) for i in idx]
    v_news = [(uws[i][:, :GDN_DV] - wqs[i][:C]).astype(BF16) for i in idx]
    k_dec_ts = [(ks[i] * jnp.exp(g_lasts[i] - g_cols[i])).T.astype(BF16) for i in idx]
    os_ = [wqs[i][C:] + _dot(attns[i], v_news[i]) for i in idx]
    new_sts = [sts[i] * jnp.exp(g_lasts[i]) + _dot(k_dec_ts[i], v_news[i]) for i in idx]
    for i, (bb, h) in enumerate(streams):
        state_ref[b0 + bb, h] = new_sts[i]
        o = _rms(os_[i], nw) * _silu(gz_ref[bb, :, h * GDN_DV:(h + 1) * GDN_DV])
        o_ref[bb, :, h * GDN_DV:(h + 1) * GDN_DV] = o.astype(o_ref.dtype)


def _gdn(gqkv, gz, small, conv_w, norm_w):
    B, S, _ = gqkv.shape
    C = GDN_CHUNK
    nb = _pick(B, 4)
    tile = lambda n: pl.BlockSpec((nb, C, n), lambda ci, b: (b, ci, 0))
    full = lambda a: pl.BlockSpec(a.shape, lambda ci, b: (0,) * a.ndim)
    return pl.pallas_call(
        _gdn_kernel,
        grid=(S // C, B // nb),
        in_specs=[tile(3 * GDN_KW), tile(GDN_VW), tile(LANES), full(conv_w), full(norm_w)],
        out_specs=tile(GDN_VW),
        out_shape=jax.ShapeDtypeStruct((B, S, GDN_VW), BF16),
        scratch_shapes=[pltpu.VMEM((B, GDN_HEADS, GDN_DK, GDN_DV), F32),
                        pltpu.VMEM((B, 8, 3 * GDN_KW), F32)],
        compiler_params=_params(("arbitrary", "arbitrary")),
        name="gdn",
    )(gqkv, gz, small, conv_w, norm_w)


def _outproj_kernel(ya_ref, yb_ref, gates_ref, x_ref, gt1_ref, sc2_ref, sh2_ref, gpm_ref, gpf_ref,
                    wa_ref, wb_ref, wo_ref, wq_ref, x1_ref, h2t_ref, qy_ref):
    ya = _dot(ya_ref[...], wa_ref[...])
    yb = _dot(yb_ref[...], wb_ref[...])
    gates = gates_ref[...]
    mix_in = gates[:, :D_MODEL] * ya + gates[:, D_MODEL:] * yb
    mix = _dot(mix_in.astype(BF16), wo_ref[...])
    x1 = x_ref[...] + gt1_ref[...] * _rms(mix, gpm_ref[...])
    x1_ref[...] = x1
    h2 = _rms(x1, gpf_ref[...]) * (1.0 + sc2_ref[...]) + sh2_ref[...]
    h2t_ref[...] = h2.T.astype(BF16)
    qy_ref[...] = _dot(h2.astype(BF16), wq_ref[...])


def _outproj(ya, yb, gates, x, gt1, sc2, sh2, gpm, gpf, wa, wb, wo, wq, tm):
    B, S, D = x.shape
    full = lambda a: pl.BlockSpec(a.shape, lambda b, i: (0,) * a.ndim)
    mod = pl.BlockSpec((None, 1, D), lambda b, i: (b, 0, 0))
    tile = lambda n: pl.BlockSpec((None, tm, n), lambda b, i: (b, i, 0))
    nq = wq.shape[1]
    return pl.pallas_call(
        _outproj_kernel,
        grid=(B, S // tm),
        in_specs=[tile(FOX_W), tile(GDN_VW), tile(2 * D), tile(D), mod, mod, mod, full(gpm), full(gpf),
                  full(wa), full(wb), full(wo), full(wq)],
        out_specs=[tile(D), pl.BlockSpec((D, tm), lambda b, i: (0, b * (S // tm) + i)), tile(nq)],
        out_shape=[jax.ShapeDtypeStruct((B, S, D), F32),
                   jax.ShapeDtypeStruct((D, B * S), BF16),
                   jax.ShapeDtypeStruct((B, S, nq), F32)],
        compiler_params=_params(("arbitrary", "arbitrary")),
        name="outproj",
    )(ya, yb, gates, x, gt1, sc2, sh2, gpm, gpf, wa, wb, wo, wq)


SUBLANES = 8


def _cmpx(a, i, l, desc):
    hi, lo = jnp.maximum(a[i], a[l]), jnp.minimum(a[i], a[l])
    a[i], a[l] = (hi, lo) if desc else (lo, hi)


def _top16_desc(a):
    a = list(a)
    n = len(a)
    k = 2
    while k <= n:
        j = k // 2
        while j >= 1:
            for i in range(n):
                l = i ^ j
                if l > i:
                    _cmpx(a, i, l, (i & k) == 0)
            j //= 2
        k *= 2
    shift = SUBLANES // 2
    while shift >= 1:
        other = [pltpu.roll(v, shift, 0) for v in a]
        a = [jnp.maximum(a[i], other[n - 1 - i]) for i in range(n)]
        j = n // 2
        while j >= 1:
            for i in range(n):
                l = i ^ j
                if l > i:
                    _cmpx(a, i, l, True)
            j //= 2
        shift //= 2
    return a


def _topk_kernel(qy_ref, k1_ref, k2_ref, theta_ref, e1_ref, s2_ref, m2_ref):
    K = PEER_TOPK
    half = PEER_DQ // 2
    tt = qy_ref.shape[0]
    G = N_KEYS // SUBLANES
    sub = lax.broadcasted_iota(jnp.int32, (SUBLANES, LANES), 0)

    def stagger(d, base):
        out = d[base]
        for r in range(1, SUBLANES):
            out = jnp.where(sub == r, d[base + r], out)
        return out

    def head(h, carry):
        col = pl.multiple_of(h * PEER_DQ, PEER_DQ)
        q1 = qy_ref[:, pl.ds(col, half)]
        q2 = qy_ref[:, pl.ds(col + half, half)]
        s1 = _dot_nt(k1_ref[...], q1, HIGHEST)
        s2 = _dot_nt(k2_ref[...], q2, HIGHEST)
        for c in range(tt // LANES):
            cs = slice(c * LANES, (c + 1) * LANES)
            s2_ref[h, c] = s2[:, cs]
            s1g = [s1[SUBLANES * g:SUBLANES * (g + 1), cs] for g in range(G)]
            d1 = _top16_desc(s1g)
            d2 = _top16_desc([s2[SUBLANES * g:SUBLANES * (g + 1), cs] for g in range(G)])
            d2_lo, d2_hi, d1_hi = stagger(d2, 0), stagger(d2, SUBLANES), stagger(d1, SUBLANES)
            cand = [d1[0] + d2_lo, d1[0] + d2_hi, d1_hi + d2[0]] + [d1[i] + d2_lo for i in range(1, SUBLANES)]
            cand += [jnp.full((SUBLANES, LANES), -jnp.inf, F32)] * (K - len(cand))
            top = _top16_desc(cand)
            z = jnp.ones_like(top[0])
            for i in range(1, K):
                z = z + jnp.exp(top[i] - top[0])
            zinv = 1.0 / z
            tau = top[K - 1]
            m2_ref[h, :, cs] = d2[0][0:1, :]
            for g in range(G):
                theta = jnp.full((SUBLANES, LANES), jnp.inf, F32)
                for j in range(K):
                    theta = jnp.where(s1g[g] + d2[j] >= tau, d2[j], theta)
                theta_ref[h, c, SUBLANES * g:SUBLANES * (g + 1), :] = theta
                e1_ref[h, c, SUBLANES * g:SUBLANES * (g + 1), :] = jnp.exp(s1g[g] - d1[0]) * zinv
        return carry

    lax.fori_loop(0, PEER_HEADS, head, 0)


def _topk(qy, keys1, keys2, tt):
    T, nq = qy.shape
    H = PEER_HEADS
    full = lambda a: pl.BlockSpec(a.shape, lambda i: (0,) * a.ndim)
    big = pl.BlockSpec((H, tt // LANES, N_KEYS, LANES), lambda i: (0, i, 0, 0))
    return pl.pallas_call(
        _topk_kernel,
        grid=(T // tt,),
        in_specs=[pl.BlockSpec((tt, nq), lambda i: (i, 0)), full(keys1), full(keys2)],
        out_specs=[big, big, big, pl.BlockSpec((H, 1, tt), lambda i: (0, 0, i))],
        out_shape=[jax.ShapeDtypeStruct((H, T // LANES, N_KEYS, LANES), F32)] * 3
        + [jax.ShapeDtypeStruct((H, 1, T), F32)],
        compiler_params=_params(("arbitrary",)),
        name="topk",
    )(qy, keys1, keys2)


def _peer_kernel(h2t_ref, u_ref, vt_ref, theta_ref, e1_ref, s2_ref, m2_ref, x1_ref, gt2_ref, g_ref, o_ref,
                 acc_ref, e2_ref, st_ref, wt_ref, *, na):
    j = pl.program_id(1)
    H = PEER_HEADS
    tt = h2t_ref.shape[1]
    vreg = (SUBLANES, LANES)
    ncol = tt // LANES
    nkb = N_KEYS // GATE_KB
    blk = (GATE_KB // SUBLANES, SUBLANES, LANES)

    @pl.when(j == 0)
    def _():
        acc_ref[...] = jnp.zeros(acc_ref.shape, F32)
        for h in range(H):
            for c in range(ncol):
                e2_ref[h, c] = jnp.exp(s2_ref[h, c] - m2_ref[h, :, c * LANES:(c + 1) * LANES])

    st = _dot(u_ref[...], h2t_ref[...])
    for c in range(ncol):
        st_ref[c] = st[:, c * LANES:(c + 1) * LANES].reshape(na, nkb, GATE_KB, LANES)

    def gate_blocks(a0, nab):
        def block(i, carry):
            c = lax.shift_right_logical(i, nkb.bit_length() - 1)
            kb = lax.bitwise_and(i, nkb - 1)
            was = [jnp.zeros(blk, F32) for _ in range(nab)]
            for h in range(H):
                s2 = s2_ref[h, c, kb].reshape(blk)
                e2 = e2_ref[h, c, kb].reshape(blk)
                for t in range(nab):
                    thb = jnp.broadcast_to(theta_ref[h, c, a0 + t:a0 + t + 1, :], vreg)
                    e1b = jnp.broadcast_to(e1_ref[h, c, a0 + t:a0 + t + 1, :], vreg)
                    was[t] = was[t] + jnp.where(s2 >= thb, e1b * e2, 0.0)
            for t in range(nab):
                sv = st_ref[c, a0 + t, kb].reshape(blk)
                act = 0.5 * sv * (1.0 + lax.erf(sv * (0.5 ** 0.5)))
                wt_ref[c, a0 + t, kb] = (was[t] * act).reshape(GATE_KB, LANES).astype(BF16)
            return carry

        lax.fori_loop(0, ncol * nkb, block, 0, unroll=4)

    for a0 in range(0, na, GATE_AB):
        gate_blocks(a0, GATE_AB)
    gated = jnp.concatenate([wt_ref[c].reshape(na * N_KEYS, LANES) for c in range(ncol)], axis=1)
    acc_ref[...] += _dot(vt_ref[...], gated)

    @pl.when(j == pl.num_programs(1) - 1)
    def _():
        y = acc_ref[...].T
        o_ref[...] = x1_ref[...] + gt2_ref[...] * _rms(y, g_ref[...])


def _peer(h2t, u_b, vt_b, theta, e1, s2, m2, x1, gt2, g_post, tt, ec, tiles_per_batch):
    D, T = h2t.shape
    E = u_b.shape[0]
    H = PEER_HEADS
    na = ec // N_KEYS
    ncol = tt // LANES
    nkb = N_KEYS // GATE_KB
    s2 = s2.reshape(H, T // LANES, nkb, GATE_KB, LANES)
    rows = pl.BlockSpec((H, ncol, na, LANES), lambda i, j: (0, i, j, 0))
    slab = (ncol, na, nkb, GATE_KB, LANES)
    return pl.pallas_call(
        functools.partial(_peer_kernel, na=na),
        grid=(T // tt, E // ec),
        in_specs=[pl.BlockSpec((D, tt), lambda i, j: (0, i)),
                  pl.BlockSpec((ec, D), lambda i, j: (j, 0)),
                  pl.BlockSpec((D, ec), lambda i, j: (0, j)),
                  rows, rows,
                  pl.BlockSpec((H, ncol, nkb, GATE_KB, LANES), lambda i, j: (0, i, 0, 0, 0)),
                  pl.BlockSpec((H, 1, tt), lambda i, j: (0, 0, i)),
                  pl.BlockSpec((tt, D), lambda i, j: (i, 0)),
                  pl.BlockSpec((None, 1, D), lambda i, j: (i // tiles_per_batch, 0, 0)),
                  pl.BlockSpec(g_post.shape, lambda i, j: (0, 0))],
        out_specs=pl.BlockSpec((tt, D), lambda i, j: (i, 0)),
        out_shape=jax.ShapeDtypeStruct((T, D), F32),
        scratch_shapes=[pltpu.VMEM((D, tt), F32), pltpu.VMEM((H, ncol, nkb, GATE_KB, LANES), F32),
                        pltpu.VMEM(slab, F32), pltpu.VMEM(slab, BF16)],
        compiler_params=_params(("arbitrary", "arbitrary")),
        name="peer",
    )(h2t, u_b, vt_b, theta, e1, s2, m2, x1, gt2, g_post)


def _pick(n, pref):
    t = pref
    while n % t:
        t //= 2
    return t


def _layer(x, c_pad, w_ada, b_ada, g_pre_mix, g_post_mix, w_in, b_fgate, fox_w_o, conv_w, a_log, dt_bias,
           gdn_norm_w, gdn_w_o, b_branch_gate, w_out, g_pre_ffn, g_post_ffn, peer_w_query, peer_keys1,
           peer_keys2, peer_u, peer_v):
    B, S, D = x.shape
    T = B * S
    mod = _ada(c_pad, w_ada, b_ada.reshape(1, -1))[:B]
    sh1, sc1, gt1, sh2, sc2, gt2 = [m.reshape(B, 1, D) for m in jnp.split(mod, 6, axis=-1)]

    off = [0]
    for s in SPLITS:
        off.append(off[-1] + s)
    col = lambda i, j: w_in[:, off[i]:off[j]]
    wf = col(0, 3).astype(BF16)
    wg = jnp.concatenate([col(4, 8)], axis=1).astype(BF16)
    wgate = col(10, 12).astype(BF16)
    pad = LANES - (FOX_HEADS + 2 * GDN_HEADS)
    ws = jnp.concatenate([col(3, 4), col(8, 10), jnp.zeros((D, pad), F32)], axis=1).astype(BF16)
    zpad = jnp.zeros((pad,), F32)
    bsm = jnp.concatenate([b_fgate, jnp.zeros((GDN_HEADS,), F32), dt_bias, zpad]).reshape(1, LANES)
    alog_row = jnp.concatenate([jnp.zeros((FOX_HEADS + GDN_HEADS,), F32), a_log, zpad]).reshape(1, LANES)
    bgate = b_branch_gate.reshape(1, 2 * D)

    tm = _pick(S, 512)
    fqkv, gqkv, gz, gates, small = _inproj(x, sc1, sh1, g_pre_mix.reshape(1, D), wf, wg, wgate, ws, bgate,
                                           bsm, alog_row, tm)
    cum = _cumsum(small, _pick(S, 512))
    ya = _fox(fqkv, cum, _pick(S, 512))
    yb = _gdn(gqkv, gz, small, conv_w, gdn_norm_w.reshape(1, GDN_DV))
    x1, h2t, qy = _outproj(ya, yb, gates, x, gt1, sc2, sh2, g_post_mix.reshape(1, D), g_pre_ffn.reshape(1, D),
                          fox_w_o.astype(BF16), gdn_w_o.astype(BF16), w_out.astype(BF16),
                          peer_w_query.astype(BF16), tm)
    qy = qy.reshape(T, -1)
    theta, e1, s2, m2 = _topk(qy, peer_keys1, peer_keys2, _pick(T, 256))
    tt = _pick(S, 512)
    out = _peer(h2t, peer_u.astype(BF16), peer_v.T.astype(BF16), theta, e1, s2, m2,
                x1.reshape(T, D), gt2, g_post_ffn.reshape(1, D), tt, 2048, S // tt)
    return out.reshape(B, S, D)


def kernel(x, c, w_ada, b_ada, g_pre_mix, g_post_mix, w_in, b_fgate, fox_w_o, conv_w, a_log, dt_bias,
           gdn_norm_w, gdn_w_o, b_branch_gate, w_out, g_pre_ffn, g_post_ffn, peer_w_query, peer_keys1,
           peer_keys2, peer_u, peer_v):
    B = x.shape[0]
    c_pad = jnp.pad(c, ((0, (-B) % 8), (0, 0)))
    for l in range(w_ada.shape[0]):
        x = _layer(x, c_pad, w_ada[l], b_ada[l], g_pre_mix[l], g_post_mix[l], w_in[l], b_fgate[l],
                   fox_w_o[l], conv_w[l], a_log[l], dt_bias[l], gdn_norm_w[l], gdn_w_o[l], b_branch_gate[l],
                   w_out[l], g_pre_ffn[l], g_post_ffn[l], peer_w_query[l], peer_keys1[l], peer_keys2[l],
                   peer_u[l], peer_v[l])
    return x
```

```python
import functools
import math

import jax
import jax.numpy as jnp
from jax import lax
from jax.experimental import pallas as pl
from jax.experimental.pallas import tpu as pltpu

F32 = jnp.float32
BF16 = jnp.bfloat16
HIGHEST = lax.Precision.HIGHEST

D_MODEL = 1024
EPS = 1e-6
FOX_HEADS = 8
FOX_DH = 64
GDN_HEADS = 4
GDN_DK = 128
GDN_DV = 128
CONV_W = 4
PEER_HEADS = 8
N_KEYS = 128
PEER_DQ = 256
PEER_TOPK = 16
FOX_W = FOX_HEADS * FOX_DH
GDN_KW = GDN_HEADS * GDN_DK
GDN_VW = GDN_HEADS * GDN_DV
SPLITS = (FOX_W, FOX_W, FOX_W, FOX_HEADS, GDN_KW, GDN_KW, GDN_VW, GDN_VW, GDN_HEADS, GDN_HEADS,
          D_MODEL, D_MODEL)

LANES = 128
GDN_CHUNK = 128
FOX_ROW_SPLIT = 1
GATE_KB = 16
GATE_AB = 8
NEG_BIG = -1e30
LOG2E = 1.4426950408889634
VMEM_LIMIT = 56 * 1024 * 1024


def _params(sem, flags=None):
    return pltpu.CompilerParams(dimension_semantics=sem, vmem_limit_bytes=VMEM_LIMIT, flags=flags)


def _dot(a, b, precision=None):
    return jnp.dot(a, b, preferred_element_type=F32, precision=precision)


def _dot_nt(a, b, precision=None):
    return lax.dot_general(a, b, (((1,), (1,)), ((), ())), preferred_element_type=F32,
                           precision=precision)


def _sigmoid(x):
    return 1.0 / (1.0 + jnp.exp(-x))


def _silu(x):
    return x * _sigmoid(x)


def _softplus(x):
    return jnp.maximum(x, 0.0) + jnp.log(1.0 + jnp.exp(-jnp.abs(x)))


def _log_sigmoid(x):
    return -_softplus(-x)


def _rms(x, g):
    return x * lax.rsqrt(jnp.mean(x * x, axis=-1, keepdims=True) + EPS) * g


def _ada_kernel(c_ref, w_ref, b_ref, o_ref):
    c = c_ref[...]
    o_ref[...] = _dot(_silu(c), w_ref[...], HIGHEST) + b_ref[...]


def _ada(c_pad, w, b):
    n = w.shape[1]
    bn = 1024
    return pl.pallas_call(
        _ada_kernel,
        grid=(n // bn,),
        in_specs=[pl.BlockSpec(c_pad.shape, lambda j: (0, 0)),
                  pl.BlockSpec((D_MODEL, bn), lambda j: (0, j)),
                  pl.BlockSpec((1, bn), lambda j: (0, j))],
        out_specs=pl.BlockSpec((c_pad.shape[0], bn), lambda j: (0, j)),
        out_shape=jax.ShapeDtypeStruct((c_pad.shape[0], n), F32),
        compiler_params=_params(("arbitrary",)),
        name="ada",
    )(c_pad, w, b)


def _inproj_kernel(x_ref, sc_ref, sh_ref, g_ref, wf_ref, wg_ref, wgate_ref, ws_ref, bgate_ref,
                   bsm_ref, alog_ref, fqkv_ref, gqkv_ref, gz_ref, gates_ref, small_ref):
    x = x_ref[...]
    h = _rms(x, g_ref[...]) * (1.0 + sc_ref[...]) + sh_ref[...]
    hb = h.astype(BF16)
    pf = _dot(hb, wf_ref[...])
    lane = lax.broadcasted_iota(jnp.int32, (1, 3 * FOX_W), 1)
    pf = jnp.where(lane < FOX_W, pf * (FOX_DH ** -0.5 * LOG2E), pf)
    fqkv_ref[...] = pf.astype(BF16)
    pg = _dot(hb, wg_ref[...])
    gqkv_ref[...] = pg[:, :3 * GDN_KW]
    gz_ref[...] = pg[:, 3 * GDN_KW:]
    gates_ref[...] = _sigmoid(_dot(hb, wgate_ref[...]) + bgate_ref[...])
    z = _dot(hb, ws_ref[...]) + bsm_ref[...]
    sl = lax.broadcasted_iota(jnp.int32, (1, LANES), 1)
    neg_a = -jnp.exp(alog_ref[...])
    small_ref[...] = jnp.where(sl < FOX_HEADS, _log_sigmoid(z),
                               jnp.where(sl < FOX_HEADS + GDN_HEADS, _sigmoid(z), neg_a * _softplus(z)))


def _inproj(x, sc1, sh1, g_pre, wf, wg, wgate, ws, bgate, bsm, alog_row, tm):
    B, S, D = x.shape
    full = lambda a: pl.BlockSpec(a.shape, lambda b, i: (0,) * a.ndim)
    mod = pl.BlockSpec((None, 1, D), lambda b, i: (b, 0, 0))
    tile = lambda n: pl.BlockSpec((None, tm, n), lambda b, i: (b, i, 0))
    return pl.pallas_call(
        _inproj_kernel,
        grid=(B, S // tm),
        in_specs=[tile(D), mod, mod, full(g_pre), full(wf), full(wg), full(wgate), full(ws),
                  full(bgate), full(bsm), full(alog_row)],
        out_specs=[tile(3 * FOX_W), tile(3 * GDN_KW), tile(GDN_VW), tile(2 * D), tile(LANES)],
        out_shape=[jax.ShapeDtypeStruct((B, S, 3 * FOX_W), BF16),
                   jax.ShapeDtypeStruct((B, S, 3 * GDN_KW), F32),
                   jax.ShapeDtypeStruct((B, S, GDN_VW), F32),
                   jax.ShapeDtypeStruct((B, S, 2 * D), F32),
                   jax.ShapeDtypeStruct((B, S, LANES), F32)],
        compiler_params=_params(("arbitrary", "arbitrary")),
        name="inproj",
    )(x, sc1, sh1, g_pre, wf, wg, wgate, ws, bgate, bsm, alog_row)


def _cumsum_kernel(s_ref, o_ref, carry_ref):
    @pl.when(pl.program_id(1) == 0)
    def _():
        carry_ref[...] = jnp.zeros_like(carry_ref)

    v = s_ref[...]
    n = v.shape[0]
    r = lax.broadcasted_iota(jnp.int32, (n, n), 0)
    c = lax.broadcasted_iota(jnp.int32, (n, n), 1)
    tri = jnp.where(r >= c, 1.0, 0.0).astype(F32)
    cs = _dot(tri, v, HIGHEST) + carry_ref[...]
    o_ref[...] = cs
    carry_ref[...] = cs[n - 1:n, :]


def _cumsum(small, tc):
    B, S, L = small.shape
    return pl.pallas_call(
        _cumsum_kernel,
        grid=(B, S // tc),
        in_specs=[pl.BlockSpec((None, tc, L), lambda b, i: (b, i, 0))],
        out_specs=pl.BlockSpec((None, tc, L), lambda b, i: (b, i, 0)),
        out_shape=jax.ShapeDtypeStruct((B, S, L), F32),
        scratch_shapes=[pltpu.VMEM((1, L), F32)],
        compiler_params=_params(("arbitrary", "arbitrary")),
        name="cumsum",
    )(small)


def _split3(x):
    p1 = x.astype(BF16).astype(F32)
    r = x - p1
    p2 = r.astype(BF16).astype(F32)
    p3 = (r - p2).astype(BF16).astype(F32)
    return p1, p2, p3


def _fox_kernel(q_ref, k_ref, v_ref, cum_ref, o_ref, kmod_ref, vmod_ref, m_ref, acc_ref, *, tq):
    hp = pl.program_id(1)
    i = pl.program_id(2)
    S = k_ref.shape[0]
    lane = lax.broadcasted_iota(jnp.int32, (1, LANES), 1)

    def head_col(tile, h):
        return jnp.sum(jnp.where(lane == h, tile, 0.0), axis=1, keepdims=True)

    def with_bias(base, hh, pieces, ones_first):
        sp = (1 - hh) * FOX_DH
        in_head = (lane >= hh * FOX_DH) & (lane < (hh + 1) * FOX_DH)
        out = jnp.where(in_head, base, 0.0)
        po, oo = (3, 0) if ones_first else (0, 3)
        for t in range(3):
            out = jnp.where(lane == sp + po + t, pieces[t], out)
            out = jnp.where(lane == sp + oo + t, 1.0, out)
        return out.astype(BF16)

    @pl.when(i == 0)
    def _():
        def chunk(ci, carry):
            start = pl.multiple_of(ci * tq, tq)
            k2 = k_ref[pl.ds(start, tq), :].astype(F32)
            v2 = v_ref[pl.ds(start, tq), :].astype(F32)
            cum = cum_ref[pl.ds(start, tq), :]
            for hh in range(2):
                ck = head_col(cum, hp * 2 + hh)
                kmod_ref[hh, pl.ds(start, tq), :] = with_bias(k2, hh, _split3(-ck * LOG2E), True)
                in_head = (lane >= hh * FOX_DH) & (lane < (hh + 1) * FOX_DH)
                vm = jnp.where(in_head, v2, jnp.where(lane == (1 - hh) * FOX_DH, 1.0, 0.0))
                vmod_ref[hh, pl.ds(start, tq), :] = vm.astype(BF16)
            return carry
        lax.fori_loop(0, S // tq, chunk, 0)

    q2 = q_ref[...].astype(F32)
    cumq = cum_ref[pl.ds(pl.multiple_of(i * tq, tq), tq), :]
    qmods = [with_bias(q2, hh, _split3(head_col(cumq, hp * 2 + hh) * LOG2E), False) for hh in range(2)]
    rr = lax.broadcasted_iota(jnp.int32, (tq // FOX_ROW_SPLIT, tq), 0)
    cc = lax.broadcasted_iota(jnp.int32, (tq // FOX_ROW_SPLIT, tq), 1)
    m_ref[...] = jnp.full(m_ref.shape, NEG_BIG, F32)
    acc_ref[...] = jnp.zeros(acc_ref.shape, F32)

    RS = FOX_ROW_SPLIT
    tr = tq // RS

    def step(j, masked):
        start = pl.multiple_of(j * tq, tq)
        chains = [(hh, r) for hh in range(2) for r in range(RS)]
        scores = [_dot_nt(qmods[hh][r * tr:(r + 1) * tr], kmod_ref[hh, pl.ds(start, tq), :]) for hh, r in chains]
        for (hh, r), s in zip(chains, scores):
            rows = slice(r * tr, (r + 1) * tr)
            if masked:
                s = jnp.where(rr + r * tr >= cc, s, NEG_BIG)
            m_prev = m_ref[hh, rows, :]
            m_new = jnp.maximum(m_prev, jnp.max(s, axis=1, keepdims=True))
            p = jnp.exp2(s - jnp.concatenate([m_new] * (tq // LANES), axis=1)).astype(BF16)
            acc_ref[hh, rows, :] = (jnp.exp2(m_prev - m_new) * acc_ref[hh, rows, :]
                                    + _dot(p, vmod_ref[hh, pl.ds(start, tq), :]))
            m_ref[hh, rows, :] = m_new

    def body(j, carry):
        step(j, False)
        return carry

    lax.fori_loop(0, i, body, 0)
    step(i, True)
    outs = []
    for hh in range(2):
        acc = acc_ref[hh]
        outs.append(acc / head_col(acc, (1 - hh) * FOX_DH))
    o_ref[...] = jnp.where(lane < FOX_DH, outs[0], outs[1]).astype(o_ref.dtype)


def _fox(fqkv, cum, tq):
    B, S, _ = fqkv.shape
    nhp = FOX_HEADS // 2
    return pl.pallas_call(
        functools.partial(_fox_kernel, tq=tq),
        grid=(B, nhp, S // tq),
        in_specs=[pl.BlockSpec((None, tq, LANES), lambda b, hp, i: (b, i, hp)),
                  pl.BlockSpec((None, S, LANES), lambda b, hp, i: (b, 0, nhp + hp)),
                  pl.BlockSpec((None, S, LANES), lambda b, hp, i: (b, 0, 2 * nhp + hp)),
                  pl.BlockSpec((None, S, LANES), lambda b, hp, i: (b, 0, 0))],
        out_specs=pl.BlockSpec((None, tq, LANES), lambda b, hp, i: (b, i, hp)),
        out_shape=jax.ShapeDtypeStruct((B, S, FOX_W), BF16),
        scratch_shapes=[pltpu.VMEM((2, S, LANES), BF16), pltpu.VMEM((2, S, LANES), BF16),
                        pltpu.VMEM((2, tq, LANES), F32), pltpu.VMEM((2, tq, LANES), F32)],
        compiler_params=_params(("arbitrary", "arbitrary", "arbitrary")),
        name="fox",
    )(fqkv, fqkv, fqkv, cum)


def _level_masks(n):
    r = lax.broadcasted_iota(jnp.int32, (n, n), 0)
    c = lax.broadcasted_iota(jnp.int32, (n, n), 1)
    masks = []
    s = 1
    while s < n:
        sel = ((r // (2 * s)) == (c // (2 * s))) & (((r // s) % 2) == 1) & (((c // s) % 2) == 0)
        masks.append(jnp.where(sel, 1.0, 0.0).astype(F32))
        s *= 2
    return masks


def _unit_lower_inverses(lows, masks):
    n = lows[0].shape[0]
    r = lax.broadcasted_iota(jnp.int32, (n, n), 0)
    c = lax.broadcasted_iota(jnp.int32, (n, n), 1)
    xs = [jnp.where(r == c, 1.0, 0.0).astype(F32)] * len(lows)
    for mask in masks:
        xbs = [x.astype(BF16) for x in xs]
        xcs = [_dot(xb, (low * mask).astype(BF16)) for xb, low in zip(xbs, lows)]
        xs = [x - _dot(xc.astype(BF16), xb) for x, xc, xb in zip(xs, xcs, xbs)]
    return xs


def _gdn_kernel(x_ref, gz_ref, sm_ref, cw_ref, nw_ref, o_ref, state_ref, carry_ref):
    ci = pl.program_id(0)
    NB = x_ref.shape[0]
    b0 = pl.program_id(1) * NB
    C = GDN_CHUNK

    @pl.when(ci == 0)
    def _():
        for bb in range(NB):
            state_ref[b0 + bb] = jnp.zeros(state_ref.shape[1:], F32)
            carry_ref[b0 + bb] = jnp.zeros(carry_ref.shape[1:], F32)

    row8 = lax.broadcasted_iota(jnp.int32, (8, 1), 0)
    r = lax.broadcasted_iota(jnp.int32, (C, C), 0)
    c = lax.broadcasted_iota(jnp.int32, (C, C), 1)
    causal = r >= c
    strict = r > c
    tri = jnp.where(causal, 1.0, 0.0).astype(F32)
    nw = nw_ref[...]
    masks = _level_masks(C)

    ys, sms, gcums, gcum_ts = [], [], [], []
    for bb in range(NB):
        x = x_ref[bb]
        prev = carry_ref[b0 + bb]
        y = cw_ref[CONV_W - 1:CONV_W, :] * x
        for j in range(1, CONV_W):
            xr = pltpu.roll(x, j, 0)
            pr = pltpu.roll(prev, j, 0)
            first = jnp.where(row8 < j, pr, xr[0:8, :])
            y = y + cw_ref[CONV_W - 1 - j:CONV_W - j, :] * jnp.concatenate([first, xr[8:, :]], axis=0)
        carry_ref[b0 + bb] = x[C - 8:C, :]
        ys.append(_silu(y))
        sms.append(sm_ref[bb])
        gcums.append(_dot(tri, sms[bb], HIGHEST))
        gcum_ts.append(gcums[bb].T)

    streams = [(bb, h) for bb in range(NB) for h in range(GDN_HEADS)]
    idx = range(len(streams))
    qs, ks, vs, betas, g_cols, g_lasts, decays = [], [], [], [], [], [], []
    for bb, h in streams:
        y, sm, gcum = ys[bb], sms[bb], gcums[bb]
        q = y[:, h * GDN_DK:(h + 1) * GDN_DK]
        k = y[:, GDN_KW + h * GDN_DK:GDN_KW + (h + 1) * GDN_DK]
        qs.append(q * lax.rsqrt(jnp.sum(q * q, axis=1, keepdims=True) + EPS) * (GDN_DK ** -0.5))
        ks.append(k * lax.rsqrt(jnp.sum(k * k, axis=1, keepdims=True) + EPS))
        vs.append(y[:, 2 * GDN_KW + h * GDN_DV:2 * GDN_KW + (h + 1) * GDN_DV])
        betas.append(sm[:, FOX_HEADS + h:FOX_HEADS + h + 1])
        gl = FOX_HEADS + GDN_HEADS + h
        g_cols.append(gcum[:, gl:gl + 1])
        g_lasts.append(gcum[C - 1:C, gl:gl + 1])
        diff = g_cols[-1] - gcum_ts[bb][gl:gl + 1, :]
        decays.append(jnp.where(causal, jnp.exp(jnp.where(causal, diff, 0.0)), 0.0))
    kbs = [ks[i] * betas[i] for i in idx]
    egs = [jnp.exp(g_cols[i]) for i in idx]
    kqs = [_dot_nt(jnp.concatenate([kbs[i], qs[i]], axis=0).astype(BF16), ks[i].astype(BF16)) for i in idx]
    lows = [jnp.where(strict, kqs[i][:C] * decays[i], 0.0) for i in idx]
    attns = [(kqs[i][C:] * decays[i]).astype(BF16) for i in idx]
    tinvs = _unit_lower_inverses(lows, masks)
    uws = [_dot(tinvs[i].astype(BF16),
                jnp.concatenate([vs[i] * betas[i], kbs[i] * egs[i]], axis=1).astype(BF16)) for i in idx]
    sts = [state_ref[b0 + bb, h] for bb, h in streams]
    wqs = [_dot(jnp.concatenate([uws[i][:, GDN_DV:], qs[i] * egs[i]], axis=0).astype(BF16),
                sts[i].astype(BF16)) for i in idx]
    v_news = [(uws[i][:, :GDN_DV] - wqs[i][:C]).astype(BF16) for i in idx]
    k_dec_ts = [(ks[i] * jnp.exp(g_lasts[i] - g_cols[i])).T.astype(BF16) for i in idx]
    os_ = [wqs[i][C:] + _dot(attns[i], v_news[i]) for i in idx]
    new_sts = [sts[i] * jnp.exp(g_lasts[i]) + _dot(k_dec_ts[i], v_news[i]) for i in idx]
    for i, (bb, h) in enumerate(streams):
        state_ref[b0 + bb, h] = new_sts[i]
        o = _rms(os_[i], nw) * _silu(gz_ref[bb, :, h * GDN_DV:(h + 1) * GDN_DV])
        o_ref[bb, :, h * GDN_DV:(h + 1) * GDN_DV] = o.astype(o_ref.dtype)


def _gdn(gqkv, gz, small, conv_w, norm_w):
    B, S, _ = gqkv.shape
    C = GDN_CHUNK
    nb = _pick(B, 4)
    tile = lambda n: pl.BlockSpec((nb, C, n), lambda ci, b: (b, ci, 0))
    full = lambda a: pl.BlockSpec(a.shape, lambda ci, b: (0,) * a.ndim)
    return pl.pallas_call(
        _gdn_kernel,
        grid=(S // C, B // nb),
        in_specs=[tile(3 * GDN_KW), tile(GDN_VW), tile(LANES), full(conv_w), full(norm_w)],
        out_specs=tile(GDN_VW),
        out_shape=jax.ShapeDtypeStruct((B, S, GDN_VW), BF16),
        scratch_shapes=[pltpu.VMEM((B, GDN_HEADS, GDN_DK, GDN_DV), F32),
                        pltpu.VMEM((B, 8, 3 * GDN_KW), F32)],
        compiler_params=_params(("arbitrary", "arbitrary")),
        name="gdn",
    )(gqkv, gz, small, conv_w, norm_w)


def _outproj_kernel(ya_ref, yb_ref, gates_ref, x_ref, gt1_ref, sc2_ref, sh2_ref, gpm_ref, gpf_ref,
                    wa_ref, wb_ref, wo_ref, wq_ref, x1_ref, h2t_ref, qy_ref):
    ya = _dot(ya_ref[...], wa_ref[...])
    yb = _dot(yb_ref[...], wb_ref[...])
    gates = gates_ref[...]
    mix_in = gates[:, :D_MODEL] * ya + gates[:, D_MODEL:] * yb
    mix = _dot(mix_in.astype(BF16), wo_ref[...])
    x1 = x_ref[...] + gt1_ref[...] * _rms(mix, gpm_ref[...])
    x1_ref[...] = x1
    h2 = _rms(x1, gpf_ref[...]) * (1.0 + sc2_ref[...]) + sh2_ref[...]
    h2t_ref[...] = h2.T.astype(BF16)
    qy_ref[...] = _dot(h2.astype(BF16), wq_ref[...])


def _outproj(ya, yb, gates, x, gt1, sc2, sh2, gpm, gpf, wa, wb, wo, wq, tm):
    B, S, D = x.shape
    full = lambda a: pl.BlockSpec(a.shape, lambda b, i: (0,) * a.ndim)
    mod = pl.BlockSpec((None, 1, D), lambda b, i: (b, 0, 0))
    tile = lambda n: pl.BlockSpec((None, tm, n), lambda b, i: (b, i, 0))
    nq = wq.shape[1]
    return pl.pallas_call(
        _outproj_kernel,
        grid=(B, S // tm),
        in_specs=[tile(FOX_W), tile(GDN_VW), tile(2 * D), tile(D), mod, mod, mod, full(gpm), full(gpf),
                  full(wa), full(wb), full(wo), full(wq)],
        out_specs=[tile(D), pl.BlockSpec((D, tm), lambda b, i: (0, b * (S // tm) + i)), tile(nq)],
        out_shape=[jax.ShapeDtypeStruct((B, S, D), F32),
                   jax.ShapeDtypeStruct((D, B * S), BF16),
                   jax.ShapeDtypeStruct((B, S, nq), F32)],
        compiler_params=_params(("arbitrary", "arbitrary")),
        name="outproj",
    )(ya, yb, gates, x, gt1, sc2, sh2, gpm, gpf, wa, wb, wo, wq)


SUBLANES = 8


def _cmpx(a, i, l, desc):
    hi, lo = jnp.maximum(a[i], a[l]), jnp.minimum(a[i], a[l])
    a[i], a[l] = (hi, lo) if desc else (lo, hi)


def _top16_desc(a):
    a = list(a)
    n = len(a)
    k = 2
    while k <= n:
        j = k // 2
        while j >= 1:
            for i in range(n):
                l = i ^ j
                if l > i:
                    _cmpx(a, i, l, (i & k) == 0)
            j //= 2
        k *= 2
    shift = SUBLANES // 2
    while shift >= 1:
        other = [pltpu.roll(v, shift, 0) for v in a]
        a = [jnp.maximum(a[i], other[n - 1 - i]) for i in range(n)]
        j = n // 2
        while j >= 1:
            for i in range(n):
                l = i ^ j
                if l > i:
                    _cmpx(a, i, l, True)
            j //= 2
        shift //= 2
    return a


def _topk_kernel(qy_ref, k1_ref, k2_ref, theta_ref, e1_ref, s2_ref, m2_ref):
    K = PEER_TOPK
    half = PEER_DQ // 2
    tt = qy_ref.shape[0]
    G = N_KEYS // SUBLANES
    sub = lax.broadcasted_iota(jnp.int32, (SUBLANES, LANES), 0)

    def stagger(d, base):
        out = d[base]
        for r in range(1, SUBLANES):
            out = jnp.where(sub == r, d[base + r], out)
        return out

    def head(h, carry):
        col = pl.multiple_of(h * PEER_DQ, PEER_DQ)
        q1 = qy_ref[:, pl.ds(col, half)]
        q2 = qy_ref[:, pl.ds(col + half, half)]
        s1 = _dot_nt(k1_ref[...], q1, HIGHEST)
        s2 = _dot_nt(k2_ref[...], q2, HIGHEST)
        for c in range(tt // LANES):
            cs = slice(c * LANES, (c + 1) * LANES)
            s2_ref[h, c] = s2[:, cs]
            s1g = [s1[SUBLANES * g:SUBLANES * (g + 1), cs] for g in range(G)]
            d1 = _top16_desc(s1g)
            d2 = _top16_desc([s2[SUBLANES * g:SUBLANES * (g + 1), cs] for g in range(G)])
            d2_lo, d2_hi, d1_hi = stagger(d2, 0), stagger(d2, SUBLANES), stagger(d1, SUBLANES)
            cand = [d1[0] + d2_lo, d1[0] + d2_hi, d1_hi + d2[0]] + [d1[i] + d2_lo for i in range(1, SUBLANES)]
            cand += [jnp.full((SUBLANES, LANES), -jnp.inf, F32)] * (K - len(cand))
            top = _top16_desc(cand)
            z = jnp.ones_like(top[0])
            for i in range(1, K):
                z = z + jnp.exp(top[i] - top[0])
            zinv = 1.0 / z
            tau = top[K - 1]
            m2_ref[h, :, cs] = d2[0][0:1, :]
            for g in range(G):
                theta = jnp.full((SUBLANES, LANES), jnp.inf, F32)
                for j in range(K):
                    theta = jnp.where(s1g[g] + d2[j] >= tau, d2[j], theta)
                theta_ref[h, c, SUBLANES * g:SUBLANES * (g + 1), :] = theta
                e1_ref[h, c, SUBLANES * g:SUBLANES * (g + 1), :] = jnp.exp(s1g[g] - d1[0]) * zinv
        return carry

    lax.fori_loop(0, PEER_HEADS, head, 0, unroll=4)


def _topk(qy, keys1, keys2, tt):
    T, nq = qy.shape
    H = PEER_HEADS
    full = lambda a: pl.BlockSpec(a.shape, lambda i: (0,) * a.ndim)
    big = pl.BlockSpec((H, tt // LANES, N_KEYS, LANES), lambda i: (0, i, 0, 0))
    return pl.pallas_call(
        _topk_kernel,
        grid=(T // tt,),
        in_specs=[pl.BlockSpec((tt, nq), lambda i: (i, 0)), full(keys1), full(keys2)],
        out_specs=[big, big, big, pl.BlockSpec((H, 1, tt), lambda i: (0, 0, i))],
        out_shape=[jax.ShapeDtypeStruct((H, T // LANES, N_KEYS, LANES), F32)] * 3
        + [jax.ShapeDtypeStruct((H, 1, T), F32)],
        compiler_params=_params(("arbitrary",)),
        name="topk",
    )(qy, keys1, keys2)


def _peer_kernel(h2t_ref, u_ref, vt_ref, theta_ref, e1_ref, s2_ref, m2_ref, x1_ref, gt2_ref, g_ref, o_ref,
                 acc_ref, e2_ref, st_ref, wt_ref, *, na):
    j = pl.program_id(1)
    H = PEER_HEADS
    tt = h2t_ref.shape[1]
    vreg = (SUBLANES, LANES)
    ncol = tt // LANES
    nkb = N_KEYS // GATE_KB
    blk = (GATE_KB // SUBLANES, SUBLANES, LANES)

    @pl.when(j == 0)
    def _():
        acc_ref[...] = jnp.zeros(acc_ref.shape, F32)
        for h in range(H):
            for c in range(ncol):
                e2_ref[h, c] = jnp.exp(s2_ref[h, c] - m2_ref[h, :, c * LANES:(c + 1) * LANES])

    st = _dot(u_ref[...], h2t_ref[...])
    for c in range(ncol):
        st_ref[c] = st[:, c * LANES:(c + 1) * LANES].reshape(na, nkb, GATE_KB, LANES)

    def gate_blocks(a0, nab):
        def block(i, carry):
            c = lax.shift_right_logical(i, nkb.bit_length() - 1)
            kb = lax.bitwise_and(i, nkb - 1)
            was = [jnp.zeros(blk, F32) for _ in range(nab)]
            for h in range(H):
                s2 = s2_ref[h, c, kb].reshape(blk)
                e2 = e2_ref[h, c, kb].reshape(blk)
                for t in range(nab):
                    thb = jnp.broadcast_to(theta_ref[h, c, a0 + t:a0 + t + 1, :], vreg)
                    e1b = jnp.broadcast_to(e1_ref[h, c, a0 + t:a0 + t + 1, :], vreg)
                    was[t] = was[t] + jnp.where(s2 >= thb, e1b * e2, 0.0)
            for t in range(nab):
                sv = st_ref[c, a0 + t, kb].reshape(blk)
                act = 0.5 * sv * (1.0 + lax.erf(sv * (0.5 ** 0.5)))
                wt_ref[c, a0 + t, kb] = (was[t] * act).reshape(GATE_KB, LANES).astype(BF16)
            return carry

        lax.fori_loop(0, ncol * nkb, block, 0, unroll=4)

    for a0 in range(0, na, GATE_AB):
        gate_blocks(a0, GATE_AB)
    gated = jnp.concatenate([wt_ref[c].reshape(na * N_KEYS, LANES) for c in range(ncol)], axis=1)
    acc_ref[...] += _dot(vt_ref[...], gated)

    @pl.when(j == pl.num_programs(1) - 1)
    def _():
        y = acc_ref[...].T
        o_ref[...] = x1_ref[...] + gt2_ref[...] * _rms(y, g_ref[...])


def _peer(h2t, u_b, vt_b, theta, e1, s2, m2, x1, gt2, g_post, tt, ec, tiles_per_batch):
    D, T = h2t.shape
    E = u_b.shape[0]
    H = PEER_HEADS
    na = ec // N_KEYS
    ncol = tt // LANES
    nkb = N_KEYS // GATE_KB
    s2 = s2.reshape(H, T // LANES, nkb, GATE_KB, LANES)
    rows = pl.BlockSpec((H, ncol, na, LANES), lambda i, j: (0, i, j, 0))
    slab = (ncol, na, nkb, GATE_KB, LANES)
    return pl.pallas_call(
        functools.partial(_peer_kernel, na=na),
        grid=(T // tt, E // ec),
        in_specs=[pl.BlockSpec((D, tt), lambda i, j: (0, i)),
                  pl.BlockSpec((ec, D), lambda i, j: (j, 0)),
                  pl.BlockSpec((D, ec), lambda i, j: (0, j)),
                  rows, rows,
                  pl.BlockSpec((H, ncol, nkb, GATE_KB, LANES), lambda i, j: (0, i, 0, 0, 0)),
                  pl.BlockSpec((H, 1, tt), lambda i, j: (0, 0, i)),
                  pl.BlockSpec((tt, D), lambda i, j: (i, 0)),
                  pl.BlockSpec((None, 1, D), lambda i, j: (i // tiles_per_batch, 0, 0)),
                  pl.BlockSpec(g_post.shape, lambda i, j: (0, 0))],
        out_specs=pl.BlockSpec((tt, D), lambda i, j: (i, 0)),
        out_shape=jax.ShapeDtypeStruct((T, D), F32),
        scratch_shapes=[pltpu.VMEM((D, tt), F32), pltpu.VMEM((H, ncol, nkb, GATE_KB, LANES), F32),
                        pltpu.VMEM(slab, F32), pltpu.VMEM(slab, BF16)],
        compiler_params=_params(("arbitrary", "arbitrary")),
        name="peer",
    )(h2t, u_b, vt_b, theta, e1, s2, m2, x1, gt2, g_post)


def _pick(n, pref):
    t = pref
    while n % t:
        t //= 2
    return t


def _layer(x, c_pad, w_ada, b_ada, g_pre_mix, g_post_mix, w_in, b_fgate, fox_w_o, conv_w, a_log, dt_bias,
           gdn_norm_w, gdn_w_o, b_branch_gate, w_out, g_pre_ffn, g_post_ffn, peer_w_query, peer_keys1,
           peer_keys2, peer_u, peer_v):
    B, S, D = x.shape
    T = B * S
    mod = _ada(c_pad, w_ada, b_ada.reshape(1, -1))[:B]
    sh1, sc1, gt1, sh2, sc2, gt2 = [m.reshape(B, 1, D) for m in jnp.split(mod, 6, axis=-1)]

    off = [0]
    for s in SPLITS:
        off.append(off[-1] + s)
    col = lambda i, j: w_in[:, off[i]:off[j]]
    wf = col(0, 3).astype(BF16)
    wg = jnp.concatenate([col(4, 8)], axis=1).astype(BF16)
    wgate = col(10, 12).astype(BF16)
    pad = LANES - (FOX_HEADS + 2 * GDN_HEADS)
    ws = jnp.concatenate([col(3, 4), col(8, 10), jnp.zeros((D, pad), F32)], axis=1).astype(BF16)
    zpad = jnp.zeros((pad,), F32)
    bsm = jnp.concatenate([b_fgate, jnp.zeros((GDN_HEADS,), F32), dt_bias, zpad]).reshape(1, LANES)
    alog_row = jnp.concatenate([jnp.zeros((FOX_HEADS + GDN_HEADS,), F32), a_log, zpad]).reshape(1, LANES)
    bgate = b_branch_gate.reshape(1, 2 * D)

    tm = _pick(S, 512)
    fqkv, gqkv, gz, gates, small = _inproj(x, sc1, sh1, g_pre_mix.reshape(1, D), wf, wg, wgate, ws, bgate,
                                           bsm, alog_row, tm)
    cum = _cumsum(small, _pick(S, 512))
    ya = _fox(fqkv, cum, _pick(S, 512))
    yb = _gdn(gqkv, gz, small, conv_w, gdn_norm_w.reshape(1, GDN_DV))
    x1, h2t, qy = _outproj(ya, yb, gates, x, gt1, sc2, sh2, g_post_mix.reshape(1, D), g_pre_ffn.reshape(1, D),
                          fox_w_o.astype(BF16), gdn_w_o.astype(BF16), w_out.astype(BF16),
                          peer_w_query.astype(BF16), tm)
    qy = qy.reshape(T, -1)
    theta, e1, s2, m2 = _topk(qy, peer_keys1, peer_keys2, _pick(T, 256))
    tt = _pick(S, 512)
    out = _peer(h2t, peer_u.astype(BF16), peer_v.T.astype(BF16), theta, e1, s2, m2,
                x1.reshape(T, D), gt2, g_post_ffn.reshape(1, D), tt, 2048, S // tt)
    return out.reshape(B, S, D)


def kernel(x, c, w_ada, b_ada, g_pre_mix, g_post_mix, w_in, b_fgate, fox_w_o, conv_w, a_log, dt_bias,
           gdn_norm_w, gdn_w_o, b_branch_gate, w_out, g_pre_ffn, g_post_ffn, peer_w_query, peer_keys1,
           peer_keys2, peer_u, peer_v):
    B = x.shape[0]
    c_pad = jnp.pad(c, ((0, (-B) % 8), (0, 0)))
    for l in range(w_ada.shape[0]):
        x = _layer(x, c_pad, w_ada[l], b_ada[l], g_pre_mix[l], g_post_mix[l], w_in[l], b_fgate[l],
                   fox_w_o[l], conv_w[l], a_log[l], dt_bias[l], gdn_norm_w[l], gdn_w_o[l], b_branch_gate[l],
                   w_out[l], g_pre_ffn[l], g_post_ffn[l], peer_w_query[l], peer_keys1[l], peer_keys2[l],
                   peer_u[l], peer_v[l])
    return x
```

```python
import functools
import math

import jax
import jax.numpy as jnp
from jax import lax
from jax.experimental import pallas as pl
from jax.experimental.pallas import tpu as pltpu

F32 = jnp.float32
BF16 = jnp.bfloat16
HIGHEST = lax.Precision.HIGHEST

D_MODEL = 1024
EPS = 1e-6
FOX_HEADS = 8
FOX_DH = 64
GDN_HEADS = 4
GDN_DK = 128
GDN_DV = 128
CONV_W = 4
PEER_HEADS = 8
N_KEYS = 128
PEER_DQ = 256
PEER_TOPK = 16
FOX_W = FOX_HEADS * FOX_DH
GDN_KW = GDN_HEADS * GDN_DK
GDN_VW = GDN_HEADS * GDN_DV
SPLITS = (FOX_W, FOX_W, FOX_W, FOX_HEADS, GDN_KW, GDN_KW, GDN_VW, GDN_VW, GDN_HEADS, GDN_HEADS,
          D_MODEL, D_MODEL)

LANES = 128
GDN_CHUNK = 128
FOX_ROW_SPLIT = 1
FOX_TILES_PER_ITER = 4
GATE_KB = 16
GATE_AB = 8
NEG_BIG = -1e30
LOG2E = 1.4426950408889634
VMEM_LIMIT = 56 * 1024 * 1024


def _params(sem, flags=None):
    return pltpu.CompilerParams(dimension_semantics=sem, vmem_limit_bytes=VMEM_LIMIT, flags=flags)


def _dot(a, b, precision=None):
    return jnp.dot(a, b, preferred_element_type=F32, precision=precision)


def _dot_nt(a, b, precision=None):
    return lax.dot_general(a, b, (((1,), (1,)), ((), ())), preferred_element_type=F32,
                           precision=precision)


def _sigmoid(x):
    return 1.0 / (1.0 + jnp.exp(-x))


def _silu(x):
    return x * _sigmoid(x)


def _softplus(x):
    return jnp.maximum(x, 0.0) + jnp.log(1.0 + jnp.exp(-jnp.abs(x)))


def _log_sigmoid(x):
    return -_softplus(-x)


def _rms(x, g):
    return x * lax.rsqrt(jnp.mean(x * x, axis=-1, keepdims=True) + EPS) * g


def _ada_kernel(c_ref, w_ref, b_ref, o_ref):
    c = c_ref[...]
    o_ref[...] = _dot(_silu(c), w_ref[...], HIGHEST) + b_ref[...]


def _ada(c_pad, w, b):
    n = w.shape[1]
    bn = 1024
    return pl.pallas_call(
        _ada_kernel,
        grid=(n // bn,),
        in_specs=[pl.BlockSpec(c_pad.shape, lambda j: (0, 0)),
                  pl.BlockSpec((D_MODEL, bn), lambda j: (0, j)),
                  pl.BlockSpec((1, bn), lambda j: (0, j))],
        out_specs=pl.BlockSpec((c_pad.shape[0], bn), lambda j: (0, j)),
        out_shape=jax.ShapeDtypeStruct((c_pad.shape[0], n), F32),
        compiler_params=_params(("arbitrary",)),
        name="ada",
    )(c_pad, w, b)


def _inproj_kernel(x_ref, sc_ref, sh_ref, g_ref, wf_ref, wg_ref, wgate_ref, ws_ref, bgate_ref,
                   bsm_ref, alog_ref, fqkv_ref, gqkv_ref, gz_ref, gates_ref, small_ref):
    x = x_ref[...]
    h = _rms(x, g_ref[...]) * (1.0 + sc_ref[...]) + sh_ref[...]
    hb = h.astype(BF16)
    pf = _dot(hb, wf_ref[...])
    lane = lax.broadcasted_iota(jnp.int32, (1, 3 * FOX_W), 1)
    pf = jnp.where(lane < FOX_W, pf * (FOX_DH ** -0.5 * LOG2E), pf)
    fqkv_ref[...] = pf.astype(BF16)
    pg = _dot(hb, wg_ref[...])
    gqkv_ref[...] = pg[:, :3 * GDN_KW]
    gz_ref[...] = pg[:, 3 * GDN_KW:]
    gates_ref[...] = _sigmoid(_dot(hb, wgate_ref[...]) + bgate_ref[...])
    z = _dot(hb, ws_ref[...]) + bsm_ref[...]
    sl = lax.broadcasted_iota(jnp.int32, (1, LANES), 1)
    neg_a = -jnp.exp(alog_ref[...])
    small_ref[...] = jnp.where(sl < FOX_HEADS, _log_sigmoid(z),
                               jnp.where(sl < FOX_HEADS + GDN_HEADS, _sigmoid(z), neg_a * _softplus(z)))


def _inproj(x, sc1, sh1, g_pre, wf, wg, wgate, ws, bgate, bsm, alog_row, tm):
    B, S, D = x.shape
    full = lambda a: pl.BlockSpec(a.shape, lambda b, i: (0,) * a.ndim)
    mod = pl.BlockSpec((None, 1, D), lambda b, i: (b, 0, 0))
    tile = lambda n: pl.BlockSpec((None, tm, n), lambda b, i: (b, i, 0))
    return pl.pallas_call(
        _inproj_kernel,
        grid=(B, S // tm),
        in_specs=[tile(D), mod, mod, full(g_pre), full(wf), full(wg), full(wgate), full(ws),
                  full(bgate), full(bsm), full(alog_row)],
        out_specs=[tile(3 * FOX_W), tile(3 * GDN_KW), tile(GDN_VW), tile(2 * D), tile(LANES)],
        out_shape=[jax.ShapeDtypeStruct((B, S, 3 * FOX_W), BF16),
                   jax.ShapeDtypeStruct((B, S, 3 * GDN_KW), F32),
                   jax.ShapeDtypeStruct((B, S, GDN_VW), F32),
                   jax.ShapeDtypeStruct((B, S, 2 * D), F32),
                   jax.ShapeDtypeStruct((B, S, LANES), F32)],
        compiler_params=_params(("arbitrary", "arbitrary")),
        name="inproj",
    )(x, sc1, sh1, g_pre, wf, wg, wgate, ws, bgate, bsm, alog_row)


def _cumsum_kernel(s_ref, o_ref, carry_ref):
    @pl.when(pl.program_id(1) == 0)
    def _():
        carry_ref[...] = jnp.zeros_like(carry_ref)

    v = s_ref[...]
    n = v.shape[0]
    r = lax.broadcasted_iota(jnp.int32, (n, n), 0)
    c = lax.broadcasted_iota(jnp.int32, (n, n), 1)
    tri = jnp.where(r >= c, 1.0, 0.0).astype(F32)
    cs = _dot(tri, v, HIGHEST) + carry_ref[...]
    o_ref[...] = cs
    carry_ref[...] = cs[n - 1:n, :]


def _cumsum(small, tc):
    B, S, L = small.shape
    return pl.pallas_call(
        _cumsum_kernel,
        grid=(B, S // tc),
        in_specs=[pl.BlockSpec((None, tc, L), lambda b, i: (b, i, 0))],
        out_specs=pl.BlockSpec((None, tc, L), lambda b, i: (b, i, 0)),
        out_shape=jax.ShapeDtypeStruct((B, S, L), F32),
        scratch_shapes=[pltpu.VMEM((1, L), F32)],
        compiler_params=_params(("arbitrary", "arbitrary")),
        name="cumsum",
    )(small)


def _split3(x):
    p1 = x.astype(BF16).astype(F32)
    r = x - p1
    p2 = r.astype(BF16).astype(F32)
    p3 = (r - p2).astype(BF16).astype(F32)
    return p1, p2, p3


def _fox_kernel(q_ref, k_ref, v_ref, cum_ref, o_ref, kmod_ref, vmod_ref, m_ref, acc_ref, *, tq):
    hp = pl.program_id(1)
    i = pl.program_id(2)
    S = k_ref.shape[0]
    lane = lax.broadcasted_iota(jnp.int32, (1, LANES), 1)

    def head_col(tile, h):
        return jnp.sum(jnp.where(lane == h, tile, 0.0), axis=1, keepdims=True)

    def with_bias(base, hh, pieces, ones_first):
        sp = (1 - hh) * FOX_DH
        in_head = (lane >= hh * FOX_DH) & (lane < (hh + 1) * FOX_DH)
        out = jnp.where(in_head, base, 0.0)
        po, oo = (3, 0) if ones_first else (0, 3)
        for t in range(3):
            out = jnp.where(lane == sp + po + t, pieces[t], out)
            out = jnp.where(lane == sp + oo + t, 1.0, out)
        return out.astype(BF16)

    @pl.when(i == 0)
    def _():
        def chunk(ci, carry):
            start = pl.multiple_of(ci * tq, tq)
            k2 = k_ref[pl.ds(start, tq), :].astype(F32)
            v2 = v_ref[pl.ds(start, tq), :].astype(F32)
            cum = cum_ref[pl.ds(start, tq), :]
            for hh in range(2):
                ck = head_col(cum, hp * 2 + hh)
                kmod_ref[hh, pl.ds(start, tq), :] = with_bias(k2, hh, _split3(-ck * LOG2E), True)
                in_head = (lane >= hh * FOX_DH) & (lane < (hh + 1) * FOX_DH)
                vm = jnp.where(in_head, v2, jnp.where(lane == (1 - hh) * FOX_DH, 1.0, 0.0))
                vmod_ref[hh, pl.ds(start, tq), :] = vm.astype(BF16)
            return carry
        lax.fori_loop(0, S // tq, chunk, 0)

    q2 = q_ref[...].astype(F32)
    cumq = cum_ref[pl.ds(pl.multiple_of(i * tq, tq), tq), :]
    qmods = [with_bias(q2, hh, _split3(head_col(cumq, hp * 2 + hh) * LOG2E), False) for hh in range(2)]
    rr = lax.broadcasted_iota(jnp.int32, (tq // FOX_ROW_SPLIT, tq), 0)
    cc = lax.broadcasted_iota(jnp.int32, (tq // FOX_ROW_SPLIT, tq), 1)
    m_ref[...] = jnp.full(m_ref.shape, NEG_BIG, F32)
    acc_ref[...] = jnp.zeros(acc_ref.shape, F32)

    RS = FOX_ROW_SPLIT
    tr = tq // RS

    def step(j, masked):
        start = pl.multiple_of(j * tq, tq)
        chains = [(hh, r) for hh in range(2) for r in range(RS)]
        scores = [_dot_nt(qmods[hh][r * tr:(r + 1) * tr], kmod_ref[hh, pl.ds(start, tq), :]) for hh, r in chains]
        for (hh, r), s in zip(chains, scores):
            rows = slice(r * tr, (r + 1) * tr)
            if masked:
                s = jnp.where(rr + r * tr >= cc, s, NEG_BIG)
            m_prev = m_ref[hh, rows, :]
            m_new = jnp.maximum(m_prev, jnp.max(s, axis=1, keepdims=True))
            p = jnp.exp2(s - jnp.concatenate([m_new] * (tq // LANES), axis=1)).astype(BF16)
            acc_ref[hh, rows, :] = (jnp.exp2(m_prev - m_new) * acc_ref[hh, rows, :]
                                    + _dot(p, vmod_ref[hh, pl.ds(start, tq), :]))
            m_ref[hh, rows, :] = m_new

    UN = FOX_TILES_PER_ITER

    def body(jj, carry):
        for t in range(UN):
            step(UN * jj + t, False)
        return carry

    def single(j, carry):
        step(j, False)
        return carry

    nfull = i // UN
    lax.fori_loop(0, nfull, body, 0)
    lax.fori_loop(nfull * UN, i, single, 0)
    step(i, True)
    outs = []
    for hh in range(2):
        acc = acc_ref[hh]
        outs.append(acc / head_col(acc, (1 - hh) * FOX_DH))
    o_ref[...] = jnp.where(lane < FOX_DH, outs[0], outs[1]).astype(o_ref.dtype)


def _fox(fqkv, cum, tq):
    B, S, _ = fqkv.shape
    nhp = FOX_HEADS // 2
    return pl.pallas_call(
        functools.partial(_fox_kernel, tq=tq),
        grid=(B, nhp, S // tq),
        in_specs=[pl.BlockSpec((None, tq, LANES), lambda b, hp, i: (b, i, hp)),
                  pl.BlockSpec((None, S, LANES), lambda b, hp, i: (b, 0, nhp + hp)),
                  pl.BlockSpec((None, S, LANES), lambda b, hp, i: (b, 0, 2 * nhp + hp)),
                  pl.BlockSpec((None, S, LANES), lambda b, hp, i: (b, 0, 0))],
        out_specs=pl.BlockSpec((None, tq, LANES), lambda b, hp, i: (b, i, hp)),
        out_shape=jax.ShapeDtypeStruct((B, S, FOX_W), BF16),
        scratch_shapes=[pltpu.VMEM((2, S, LANES), BF16), pltpu.VMEM((2, S, LANES), BF16),
                        pltpu.VMEM((2, tq, LANES), F32), pltpu.VMEM((2, tq, LANES), F32)],
        compiler_params=_params(("arbitrary", "arbitrary", "arbitrary")),
        name="fox",
    )(fqkv, fqkv, fqkv, cum)


def _level_masks(n):
    r = lax.broadcasted_iota(jnp.int32, (n, n), 0)
    c = lax.broadcasted_iota(jnp.int32, (n, n), 1)
    masks = []
    s = 1
    while s < n:
        sel = ((r // (2 * s)) == (c // (2 * s))) & (((r // s) % 2) == 1) & (((c // s) % 2) == 0)
        masks.append(jnp.where(sel, 1.0, 0.0).astype(F32))
        s *= 2
    return masks


def _unit_lower_inverses(lows, masks):
    n = lows[0].shape[0]
    r = lax.broadcasted_iota(jnp.int32, (n, n), 0)
    c = lax.broadcasted_iota(jnp.int32, (n, n), 1)
    xs = [jnp.where(r == c, 1.0, 0.0).astype(F32)] * len(lows)
    for mask in masks:
        xbs = [x.astype(BF16) for x in xs]
        xcs = [_dot(xb, (low * mask).astype(BF16)) for xb, low in zip(xbs, lows)]
        xs = [x - _dot(xc.astype(BF16), xb) for x, xc, xb in zip(xs, xcs, xbs)]
    return xs


def _gdn_kernel(x_ref, gz_ref, sm_ref, cw_ref, nw_ref, o_ref, state_ref, carry_ref):
    ci = pl.program_id(0)
    NB = x_ref.shape[0]
    b0 = pl.program_id(1) * NB
    C = GDN_CHUNK

    @pl.when(ci == 0)
    def _():
        for bb in range(NB):
            state_ref[b0 + bb] = jnp.zeros(state_ref.shape[1:], F32)
            carry_ref[b0 + bb] = jnp.zeros(carry_ref.shape[1:], F32)

    row8 = lax.broadcasted_iota(jnp.int32, (8, 1), 0)
    r = lax.broadcasted_iota(jnp.int32, (C, C), 0)
    c = lax.broadcasted_iota(jnp.int32, (C, C), 1)
    causal = r >= c
    strict = r > c
    tri = jnp.where(causal, 1.0, 0.0).astype(F32)
    nw = nw_ref[...]
    masks = _level_masks(C)

    ys, sms, gcums, gcum_ts = [], [], [], []
    for bb in range(NB):
        x = x_ref[bb]
        prev = carry_ref[b0 + bb]
        y = cw_ref[CONV_W - 1:CONV_W, :] * x
        for j in range(1, CONV_W):
            xr = pltpu.roll(x, j, 0)
            pr = pltpu.roll(prev, j, 0)
            first = jnp.where(row8 < j, pr, xr[0:8, :])
            y = y + cw_ref[CONV_W - 1 - j:CONV_W - j, :] * jnp.concatenate([first, xr[8:, :]], axis=0)
        carry_ref[b0 + bb] = x[C - 8:C, :]
        ys.append(_silu(y))
        sms.append(sm_ref[bb])
        gcums.append(_dot(tri, sms[bb], HIGHEST))
        gcum_ts.append(gcums[bb].T)

    streams = [(bb, h) for bb in range(NB) for h in range(GDN_HEADS)]
    idx = range(len(streams))
    qs, ks, vs, betas, g_cols, g_lasts, decays = [], [], [], [], [], [], []
    for bb, h in streams:
        y, sm, gcum = ys[bb], sms[bb], gcums[bb]
        q = y[:, h * GDN_DK:(h + 1) * GDN_DK]
        k = y[:, GDN_KW + h * GDN_DK:GDN_KW + (h + 1) * GDN_DK]
        qs.append(q * lax.rsqrt(jnp.sum(q * q, axis=1, keepdims=True) + EPS) * (GDN_DK ** -0.5))
        ks.append(k * lax.rsqrt(jnp.sum(k * k, axis=1, keepdims=True) + EPS))
        vs.append(y[:, 2 * GDN_KW + h * GDN_DV:2 * GDN_KW + (h + 1) * GDN_DV])
        betas.append(sm[:, FOX_HEADS + h:FOX_HEADS + h + 1])
        gl = FOX_HEADS + GDN_HEADS + h
        g_cols.append(gcum[:, gl:gl + 1])
        g_lasts.append(gcum[C - 1:C, gl:gl + 1])
        diff = g_cols[-1] - gcum_ts[bb][gl:gl + 1, :]
        decays.append(jnp.where(causal, jnp.exp(jnp.where(causal, diff, 0.0)), 0.0))
    kbs = [ks[i] * betas[i] for i in idx]
    egs = [jnp.exp(g_cols[i]) for i in idx]
    kqs = [_dot_nt(jnp.concatenate([kbs[i], qs[i]], axis=0).astype(BF16), ks[i].astype(BF16)) for i in idx]
    lows = [jnp.where(strict, kqs[i][:C] * decays[i], 0.0) for i in idx]
    attns = [(kqs[i][C:] * decays[i]).astype(BF16) for i in idx]
    tinvs = _unit_lower_inverses(lows, masks)
    uws = [_dot(tinvs[i].astype(BF16),
                jnp.concatenate([vs[i] * betas[i], kbs[i] * egs[i]], axis=1).astype(BF16)) for i in idx]
    sts = [state_ref[b0 + bb, h] for bb, h in streams]
    wqs = [_dot(jnp.concatenate([uws[i][:, GDN_DV:], qs[i] * egs[i]], axis=0).astype(BF16),
                sts[i].astype(BF16)) for i in idx]
    v_news = [(uws[i][:, :GDN_DV] - wqs[i][:C]).astype(BF16) for i in idx]
    k_dec_ts = [(ks[i] * jnp.exp(g_lasts[i] - g_cols[i])).T.astype(BF16) for i in idx]
    os_ = [wqs[i][C:] + _dot(attns[i], v_news[i]) for i in idx]
    new_sts = [sts[i] * jnp.exp(g_lasts[i]) + _dot(k_dec_ts[i], v_news[i]) for i in idx]
    for i, (bb, h) in enumerate(streams):
        state_ref[b0 + bb, h] = new_sts[i]
        o = _rms(os_[i], nw) * _silu(gz_ref[bb, :, h * GDN_DV:(h + 1) * GDN_DV])
        o_ref[bb, :, h * GDN_DV:(h + 1) * GDN_DV] = o.astype(o_ref.dtype)


def _gdn(gqkv, gz, small, conv_w, norm_w):
    B, S, _ = gqkv.shape
    C = GDN_CHUNK
    nb = _pick(B, 4)
    tile = lambda n: pl.BlockSpec((nb, C, n), lambda ci, b: (b, ci, 0))
    full = lambda a: pl.BlockSpec(a.shape, lambda ci, b: (0,) * a.ndim)
    return pl.pallas_call(
        _gdn_kernel,
        grid=(S // C, B // nb),
        in_specs=[tile(3 * GDN_KW), tile(GDN_VW), tile(LANES), full(conv_w), full(norm_w)],
        out_specs=tile(GDN_VW),
        out_shape=jax.ShapeDtypeStruct((B, S, GDN_VW), BF16),
        scratch_shapes=[pltpu.VMEM((B, GDN_HEADS, GDN_DK, GDN_DV), F32),
                        pltpu.VMEM((B, 8, 3 * GDN_KW), F32)],
        compiler_params=_params(("arbitrary", "arbitrary")),
        name="gdn",
    )(gqkv, gz, small, conv_w, norm_w)


def _outproj_kernel(ya_ref, yb_ref, gates_ref, x_ref, gt1_ref, sc2_ref, sh2_ref, gpm_ref, gpf_ref,
                    wa_ref, wb_ref, wo_ref, wq_ref, x1_ref, h2t_ref, qy_ref):
    ya = _dot(ya_ref[...], wa_ref[...])
    yb = _dot(yb_ref[...], wb_ref[...])
    gates = gates_ref[...]
    mix_in = gates[:, :D_MODEL] * ya + gates[:, D_MODEL:] * yb
    mix = _dot(mix_in.astype(BF16), wo_ref[...])
    x1 = x_ref[...] + gt1_ref[...] * _rms(mix, gpm_ref[...])
    x1_ref[...] = x1
    h2 = _rms(x1, gpf_ref[...]) * (1.0 + sc2_ref[...]) + sh2_ref[...]
    h2t_ref[...] = h2.T.astype(BF16)
    qy_ref[...] = _dot(h2.astype(BF16), wq_ref[...])


def _outproj(ya, yb, gates, x, gt1, sc2, sh2, gpm, gpf, wa, wb, wo, wq, tm):
    B, S, D = x.shape
    full = lambda a: pl.BlockSpec(a.shape, lambda b, i: (0,) * a.ndim)
    mod = pl.BlockSpec((None, 1, D), lambda b, i: (b, 0, 0))
    tile = lambda n: pl.BlockSpec((None, tm, n), lambda b, i: (b, i, 0))
    nq = wq.shape[1]
    return pl.pallas_call(
        _outproj_kernel,
        grid=(B, S // tm),
        in_specs=[tile(FOX_W), tile(GDN_VW), tile(2 * D), tile(D), mod, mod, mod, full(gpm), full(gpf),
                  full(wa), full(wb), full(wo), full(wq)],
        out_specs=[tile(D), pl.BlockSpec((D, tm), lambda b, i: (0, b * (S // tm) + i)), tile(nq)],
        out_shape=[jax.ShapeDtypeStruct((B, S, D), F32),
                   jax.ShapeDtypeStruct((D, B * S), BF16),
                   jax.ShapeDtypeStruct((B, S, nq), F32)],
        compiler_params=_params(("arbitrary", "arbitrary")),
        name="outproj",
    )(ya, yb, gates, x, gt1, sc2, sh2, gpm, gpf, wa, wb, wo, wq)


SUBLANES = 8


def _cmpx(a, i, l, desc):
    hi, lo = jnp.maximum(a[i], a[l]), jnp.minimum(a[i], a[l])
    a[i], a[l] = (hi, lo) if desc else (lo, hi)


def _top16_desc(a):
    a = list(a)
    n = len(a)
    k = 2
    while k <= n:
        j = k // 2
        while j >= 1:
            for i in range(n):
                l = i ^ j
                if l > i:
                    _cmpx(a, i, l, (i & k) == 0)
            j //= 2
        k *= 2
    shift = SUBLANES // 2
    while shift >= 1:
        other = [pltpu.roll(v, shift, 0) for v in a]
        a = [jnp.maximum(a[i], other[n - 1 - i]) for i in range(n)]
        j = n // 2
        while j >= 1:
            for i in range(n):
                l = i ^ j
                if l > i:
                    _cmpx(a, i, l, True)
            j //= 2
        shift //= 2
    return a


def _topk_kernel(qy_ref, k1_ref, k2_ref, theta_ref, e1_ref, s2_ref, m2_ref):
    K = PEER_TOPK
    half = PEER_DQ // 2
    tt = qy_ref.shape[0]
    G = N_KEYS // SUBLANES
    sub = lax.broadcasted_iota(jnp.int32, (SUBLANES, LANES), 0)

    def stagger(d, base):
        out = d[base]
        for r in range(1, SUBLANES):
            out = jnp.where(sub == r, d[base + r], out)
        return out

    def head(h, carry):
        col = pl.multiple_of(h * PEER_DQ, PEER_DQ)
        q1 = qy_ref[:, pl.ds(col, half)]
        q2 = qy_ref[:, pl.ds(col + half, half)]
        s1 = _dot_nt(k1_ref[...], q1, HIGHEST)
        s2 = _dot_nt(k2_ref[...], q2, HIGHEST)
        for c in range(tt // LANES):
            cs = slice(c * LANES, (c + 1) * LANES)
            s2_ref[h, c] = s2[:, cs]
            s1g = [s1[SUBLANES * g:SUBLANES * (g + 1), cs] for g in range(G)]
            d1 = _top16_desc(s1g)
            d2 = _top16_desc([s2[SUBLANES * g:SUBLANES * (g + 1), cs] for g in range(G)])
            d2_lo, d2_hi, d1_hi = stagger(d2, 0), stagger(d2, SUBLANES), stagger(d1, SUBLANES)
            cand = [d1[0] + d2_lo, d1[0] + d2_hi, d1_hi + d2[0]] + [d1[i] + d2_lo for i in range(1, SUBLANES)]
            cand += [jnp.full((SUBLANES, LANES), -jnp.inf, F32)] * (K - len(cand))
            top = _top16_desc(cand)
            z = jnp.ones_like(top[0])
            for i in range(1, K):
                z = z + jnp.exp(top[i] - top[0])
            zinv = 1.0 / z
            tau = top[K - 1]
            m2_ref[h, :, cs] = d2[0][0:1, :]
            for g in range(G):
                theta = jnp.full((SUBLANES, LANES), jnp.inf, F32)
                for j in range(K):
                    theta = jnp.where(s1g[g] + d2[j] >= tau, d2[j], theta)
                theta_ref[h, c, SUBLANES * g:SUBLANES * (g + 1), :] = theta
                e1_ref[h, c, SUBLANES * g:SUBLANES * (g + 1), :] = jnp.exp(s1g[g] - d1[0]) * zinv
        return carry

    lax.fori_loop(0, PEER_HEADS, head, 0, unroll=4)


def _topk(qy, keys1, keys2, tt):
    T, nq = qy.shape
    H = PEER_HEADS
    full = lambda a: pl.BlockSpec(a.shape, lambda i: (0,) * a.ndim)
    big = pl.BlockSpec((H, tt // LANES, N_KEYS, LANES), lambda i: (0, i, 0, 0))
    return pl.pallas_call(
        _topk_kernel,
        grid=(T // tt,),
        in_specs=[pl.BlockSpec((tt, nq), lambda i: (i, 0)), full(keys1), full(keys2)],
        out_specs=[big, big, big, pl.BlockSpec((H, 1, tt), lambda i: (0, 0, i))],
        out_shape=[jax.ShapeDtypeStruct((H, T // LANES, N_KEYS, LANES), F32)] * 3
        + [jax.ShapeDtypeStruct((H, 1, T), F32)],
        compiler_params=_params(("arbitrary",)),
        name="topk",
    )(qy, keys1, keys2)


def _peer_kernel(h2t_ref, u_ref, vt_ref, theta_ref, e1_ref, s2_ref, m2_ref, x1_ref, gt2_ref, g_ref, o_ref,
                 acc_ref, e2_ref, st_ref, wt_ref, *, na):
    j = pl.program_id(1)
    H = PEER_HEADS
    tt = h2t_ref.shape[1]
    vreg = (SUBLANES, LANES)
    ncol = tt // LANES
    nkb = N_KEYS // GATE_KB
    blk = (GATE_KB // SUBLANES, SUBLANES, LANES)

    @pl.when(j == 0)
    def _():
        acc_ref[...] = jnp.zeros(acc_ref.shape, F32)
        for h in range(H):
            for c in range(ncol):
                e2_ref[h, c] = jnp.exp(s2_ref[h, c] - m2_ref[h, :, c * LANES:(c + 1) * LANES])

    st = _dot(u_ref[...], h2t_ref[...])
    for c in range(ncol):
        st_ref[c] = st[:, c * LANES:(c + 1) * LANES].reshape(na, nkb, GATE_KB, LANES)

    def gate_blocks(a0, nab):
        def block(i, carry):
            c = lax.shift_right_logical(i, nkb.bit_length() - 1)
            kb = lax.bitwise_and(i, nkb - 1)
            was = [jnp.zeros(blk, F32) for _ in range(nab)]
            for h in range(H):
                s2 = s2_ref[h, c, kb].reshape(blk)
                e2 = e2_ref[h, c, kb].reshape(blk)
                for t in range(nab):
                    thb = jnp.broadcast_to(theta_ref[h, c, a0 + t:a0 + t + 1, :], vreg)
                    e1b = jnp.broadcast_to(e1_ref[h, c, a0 + t:a0 + t + 1, :], vreg)
                    was[t] = was[t] + jnp.where(s2 >= thb, e1b * e2, 0.0)
            for t in range(nab):
                sv = st_ref[c, a0 + t, kb].reshape(blk)
                act = 0.5 * sv * (1.0 + lax.erf(sv * (0.5 ** 0.5)))
                wt_ref[c, a0 + t, kb] = (was[t] * act).reshape(GATE_KB, LANES).astype(BF16)
            return carry

        lax.fori_loop(0, ncol * nkb, block, 0, unroll=4)

    for a0 in range(0, na, GATE_AB):
        gate_blocks(a0, GATE_AB)
    gated = jnp.concatenate([wt_ref[c].reshape(na * N_KEYS, LANES) for c in range(ncol)], axis=1)
    acc_ref[...] += _dot(vt_ref[...], gated)

    @pl.when(j == pl.num_programs(1) - 1)
    def _():
        y = acc_ref[...].T
        o_ref[...] = x1_ref[...] + gt2_ref[...] * _rms(y, g_ref[...])


def _peer(h2t, u_b, vt_b, theta, e1, s2, m2, x1, gt2, g_post, tt, ec, tiles_per_batch):
    D, T = h2t.shape
    E = u_b.shape[0]
    H = PEER_HEADS
    na = ec // N_KEYS
    ncol = tt // LANES
    nkb = N_KEYS // GATE_KB
    s2 = s2.reshape(H, T // LANES, nkb, GATE_KB, LANES)
    rows = pl.BlockSpec((H, ncol, na, LANES), lambda i, j: (0, i, j, 0))
    slab = (ncol, na, nkb, GATE_KB, LANES)
    return pl.pallas_call(
        functools.partial(_peer_kernel, na=na),
        grid=(T // tt, E // ec),
        in_specs=[pl.BlockSpec((D, tt), lambda i, j: (0, i)),
                  pl.BlockSpec((ec, D), lambda i, j: (j, 0)),
                  pl.BlockSpec((D, ec), lambda i, j: (0, j)),
                  rows, rows,
                  pl.BlockSpec((H, ncol, nkb, GATE_KB, LANES), lambda i, j: (0, i, 0, 0, 0)),
                  pl.BlockSpec((H, 1, tt), lambda i, j: (0, 0, i)),
                  pl.BlockSpec((tt, D), lambda i, j: (i, 0)),
                  pl.BlockSpec((None, 1, D), lambda i, j: (i // tiles_per_batch, 0, 0)),
                  pl.BlockSpec(g_post.shape, lambda i, j: (0, 0))],
        out_specs=pl.BlockSpec((tt, D), lambda i, j: (i, 0)),
        out_shape=jax.ShapeDtypeStruct((T, D), F32),
        scratch_shapes=[pltpu.VMEM((D, tt), F32), pltpu.VMEM((H, ncol, nkb, GATE_KB, LANES), F32),
                        pltpu.VMEM(slab, F32), pltpu.VMEM(slab, BF16)],
        compiler_params=_params(("arbitrary", "arbitrary")),
        name="peer",
    )(h2t, u_b, vt_b, theta, e1, s2, m2, x1, gt2, g_post)


def _pick(n, pref):
    t = pref
    while n % t:
        t //= 2
    return t


def _layer(x, c_pad, w_ada, b_ada, g_pre_mix, g_post_mix, w_in, b_fgate, fox_w_o, conv_w, a_log, dt_bias,
           gdn_norm_w, gdn_w_o, b_branch_gate, w_out, g_pre_ffn, g_post_ffn, peer_w_query, peer_keys1,
           peer_keys2, peer_u, peer_v):
    B, S, D = x.shape
    T = B * S
    mod = _ada(c_pad, w_ada, b_ada.reshape(1, -1))[:B]
    sh1, sc1, gt1, sh2, sc2, gt2 = [m.reshape(B, 1, D) for m in jnp.split(mod, 6, axis=-1)]

    off = [0]
    for s in SPLITS:
        off.append(off[-1] + s)
    col = lambda i, j: w_in[:, off[i]:off[j]]
    wf = col(0, 3).astype(BF16)
    wg = jnp.concatenate([col(4, 8)], axis=1).astype(BF16)
    wgate = col(10, 12).astype(BF16)
    pad = LANES - (FOX_HEADS + 2 * GDN_HEADS)
    ws = jnp.concatenate([col(3, 4), col(8, 10), jnp.zeros((D, pad), F32)], axis=1).astype(BF16)
    zpad = jnp.zeros((pad,), F32)
    bsm = jnp.concatenate([b_fgate, jnp.zeros((GDN_HEADS,), F32), dt_bias, zpad]).reshape(1, LANES)
    alog_row = jnp.concatenate([jnp.zeros((FOX_HEADS + GDN_HEADS,), F32), a_log, zpad]).reshape(1, LANES)
    bgate = b_branch_gate.reshape(1, 2 * D)

    tm = _pick(S, 512)
    fqkv, gqkv, gz, gates, small = _inproj(x, sc1, sh1, g_pre_mix.reshape(1, D), wf, wg, wgate, ws, bgate,
                                           bsm, alog_row, tm)
    cum = _cumsum(small, _pick(S, 512))
    ya = _fox(fqkv, cum, _pick(S, 512))
    yb = _gdn(gqkv, gz, small, conv_w, gdn_norm_w.reshape(1, GDN_DV))
    x1, h2t, qy = _outproj(ya, yb, gates, x, gt1, sc2, sh2, g_post_mix.reshape(1, D), g_pre_ffn.reshape(1, D),
                          fox_w_o.astype(BF16), gdn_w_o.astype(BF16), w_out.astype(BF16),
                          peer_w_query.astype(BF16), tm)
    qy = qy.reshape(T, -1)
    theta, e1, s2, m2 = _topk(qy, peer_keys1, peer_keys2, _pick(T, 256))
    tt = _pick(S, 512)
    out = _peer(h2t, peer_u.astype(BF16), peer_v.T.astype(BF16), theta, e1, s2, m2,
                x1.reshape(T, D), gt2, g_post_ffn.reshape(1, D), tt, 2048, S // tt)
    return out.reshape(B, S, D)


def kernel(x, c, w_ada, b_ada, g_pre_mix, g_post_mix, w_in, b_fgate, fox_w_o, conv_w, a_log, dt_bias,
           gdn_norm_w, gdn_w_o, b_branch_gate, w_out, g_pre_ffn, g_post_ffn, peer_w_query, peer_keys1,
           peer_keys2, peer_u, peer_v):
    B = x.shape[0]
    c_pad = jnp.pad(c, ((0, (-B) % 8), (0, 0)))
    for l in range(w_ada.shape[0]):
        x = _layer(x, c_pad, w_ada[l], b_ada[l], g_pre_mix[l], g_post_mix[l], w_in[l], b_fgate[l],
                   fox_w_o[l], conv_w[l], a_log[l], dt_bias[l], gdn_norm_w[l], gdn_w_o[l], b_branch_gate[l],
                   w_out[l], g_pre_ffn[l], g_post_ffn[l], peer_w_query[l], peer_keys1[l], peer_keys2[l],
                   peer_u[l], peer_v[l])
    return x
```

```python
import functools
import math

import jax
import jax.numpy as jnp
from jax import lax
from jax.experimental import pallas as pl
from jax.experimental.pallas import tpu as pltpu

F32 = jnp.float32
BF16 = jnp.bfloat16
HIGHEST = lax.Precision.HIGHEST

D_MODEL = 1024
EPS = 1e-6
FOX_HEADS = 8
FOX_DH = 64
GDN_HEADS = 4
GDN_DK = 128
GDN_DV = 128
CONV_W = 4
PEER_HEADS = 8
N_KEYS = 128
PEER_DQ = 256
PEER_TOPK = 16
FOX_W = FOX_HEADS * FOX_DH
GDN_KW = GDN_HEADS * GDN_DK
GDN_VW = GDN_HEADS * GDN_DV
SPLITS = (FOX_W, FOX_W, FOX_W, FOX_HEADS, GDN_KW, GDN_KW, GDN_VW, GDN_VW, GDN_HEADS, GDN_HEADS,
          D_MODEL, D_MODEL)

LANES = 128
GDN_CHUNK = 128
FOX_ROW_SPLIT = 1
FOX_TILES_PER_ITER = 4
GATE_KB = 16
GATE_AB = 8
NEG_BIG = -1e30
LOG2E = 1.4426950408889634
VMEM_LIMIT = 56 * 1024 * 1024


def _params(sem, flags=None):
    return pltpu.CompilerParams(dimension_semantics=sem, vmem_limit_bytes=VMEM_LIMIT, flags=flags)


def _dot(a, b, precision=None):
    return jnp.dot(a, b, preferred_element_type=F32, precision=precision)


def _dot_nt(a, b, precision=None):
    return lax.dot_general(a, b, (((1,), (1,)), ((), ())), preferred_element_type=F32,
                           precision=precision)


def _sigmoid(x):
    return 1.0 / (1.0 + jnp.exp(-x))


def _silu(x):
    return x * _sigmoid(x)


def _softplus(x):
    return jnp.maximum(x, 0.0) + jnp.log(1.0 + jnp.exp(-jnp.abs(x)))


def _log_sigmoid(x):
    return -_softplus(-x)


def _rms(x, g):
    return x * lax.rsqrt(jnp.mean(x * x, axis=-1, keepdims=True) + EPS) * g


def _ada_kernel(c_ref, w_ref, b_ref, o_ref):
    c = c_ref[...]
    o_ref[...] = _dot(_silu(c), w_ref[...], HIGHEST) + b_ref[...]


def _ada(c_pad, w, b):
    n = w.shape[1]
    bn = 1024
    return pl.pallas_call(
        _ada_kernel,
        grid=(n // bn,),
        in_specs=[pl.BlockSpec(c_pad.shape, lambda j: (0, 0)),
                  pl.BlockSpec((D_MODEL, bn), lambda j: (0, j)),
                  pl.BlockSpec((1, bn), lambda j: (0, j))],
        out_specs=pl.BlockSpec((c_pad.shape[0], bn), lambda j: (0, j)),
        out_shape=jax.ShapeDtypeStruct((c_pad.shape[0], n), F32),
        compiler_params=_params(("arbitrary",)),
        name="ada",
    )(c_pad, w, b)


def _inproj_kernel(x_ref, sc_ref, sh_ref, g_ref, wf_ref, wg_ref, wgate_ref, ws_ref, bgate_ref,
                   bsm_ref, alog_ref, fqkv_ref, gqkv_ref, gz_ref, gates_ref, small_ref):
    x = x_ref[...]
    h = _rms(x, g_ref[...]) * (1.0 + sc_ref[...]) + sh_ref[...]
    hb = h.astype(BF16)
    pf = _dot(hb, wf_ref[...])
    lane = lax.broadcasted_iota(jnp.int32, (1, 3 * FOX_W), 1)
    pf = jnp.where(lane < FOX_W, pf * (FOX_DH ** -0.5 * LOG2E), pf)
    fqkv_ref[...] = pf.astype(BF16)
    pg = _dot(hb, wg_ref[...])
    gqkv_ref[...] = pg[:, :3 * GDN_KW]
    gz_ref[...] = pg[:, 3 * GDN_KW:]
    gates_ref[...] = _sigmoid(_dot(hb, wgate_ref[...]) + bgate_ref[...])
    z = _dot(hb, ws_ref[...]) + bsm_ref[...]
    sl = lax.broadcasted_iota(jnp.int32, (1, LANES), 1)
    neg_a = -jnp.exp(alog_ref[...])
    small_ref[...] = jnp.where(sl < FOX_HEADS, _log_sigmoid(z),
                               jnp.where(sl < FOX_HEADS + GDN_HEADS, _sigmoid(z), neg_a * _softplus(z)))


def _inproj(x, sc1, sh1, g_pre, wf, wg, wgate, ws, bgate, bsm, alog_row, tm):
    B, S, D = x.shape
    full = lambda a: pl.BlockSpec(a.shape, lambda b, i: (0,) * a.ndim)
    mod = pl.BlockSpec((None, 1, D), lambda b, i: (b, 0, 0))
    tile = lambda n: pl.BlockSpec((None, tm, n), lambda b, i: (b, i, 0))
    return pl.pallas_call(
        _inproj_kernel,
        grid=(B, S // tm),
        in_specs=[tile(D), mod, mod, full(g_pre), full(wf), full(wg), full(wgate), full(ws),
                  full(bgate), full(bsm), full(alog_row)],
        out_specs=[tile(3 * FOX_W), tile(3 * GDN_KW), tile(GDN_VW), tile(2 * D), tile(LANES)],
        out_shape=[jax.ShapeDtypeStruct((B, S, 3 * FOX_W), BF16),
                   jax.ShapeDtypeStruct((B, S, 3 * GDN_KW), F32),
                   jax.ShapeDtypeStruct((B, S, GDN_VW), F32),
                   jax.ShapeDtypeStruct((B, S, 2 * D), F32),
                   jax.ShapeDtypeStruct((B, S, LANES), F32)],
        compiler_params=_params(("arbitrary", "arbitrary")),
        name="inproj",
    )(x, sc1, sh1, g_pre, wf, wg, wgate, ws, bgate, bsm, alog_row)


def _cumsum_kernel(s_ref, o_ref, carry_ref):
    @pl.when(pl.program_id(1) == 0)
    def _():
        carry_ref[...] = jnp.zeros_like(carry_ref)

    v = s_ref[...]
    n = v.shape[0]
    r = lax.broadcasted_iota(jnp.int32, (n, n), 0)
    c = lax.broadcasted_iota(jnp.int32, (n, n), 1)
    tri = jnp.where(r >= c, 1.0, 0.0).astype(F32)
    cs = _dot(tri, v, HIGHEST) + carry_ref[...]
    o_ref[...] = cs
    carry_ref[...] = cs[n - 1:n, :]


def _cumsum(small, tc):
    B, S, L = small.shape
    return pl.pallas_call(
        _cumsum_kernel,
        grid=(B, S // tc),
        in_specs=[pl.BlockSpec((None, tc, L), lambda b, i: (b, i, 0))],
        out_specs=pl.BlockSpec((None, tc, L), lambda b, i: (b, i, 0)),
        out_shape=jax.ShapeDtypeStruct((B, S, L), F32),
        scratch_shapes=[pltpu.VMEM((1, L), F32)],
        compiler_params=_params(("arbitrary", "arbitrary")),
        name="cumsum",
    )(small)


def _split3(x):
    p1 = x.astype(BF16).astype(F32)
    r = x - p1
    p2 = r.astype(BF16).astype(F32)
    p3 = (r - p2).astype(BF16).astype(F32)
    return p1, p2, p3


def _fox_kernel(q_ref, k_ref, v_ref, cum_ref, o_ref, kmod_ref, vmod_ref, m_ref, acc_ref, *, tq):
    hp = pl.program_id(1)
    i = pl.program_id(2)
    S = k_ref.shape[0]
    lane = lax.broadcasted_iota(jnp.int32, (1, LANES), 1)

    def head_col(tile, h):
        return jnp.sum(jnp.where(lane == h, tile, 0.0), axis=1, keepdims=True)

    def with_bias(base, hh, pieces, ones_first):
        sp = (1 - hh) * FOX_DH
        in_head = (lane >= hh * FOX_DH) & (lane < (hh + 1) * FOX_DH)
        out = jnp.where(in_head, base, 0.0)
        po, oo = (3, 0) if ones_first else (0, 3)
        for t in range(3):
            out = jnp.where(lane == sp + po + t, pieces[t], out)
            out = jnp.where(lane == sp + oo + t, 1.0, out)
        return out.astype(BF16)

    @pl.when(i == 0)
    def _():
        def chunk(ci, carry):
            start = pl.multiple_of(ci * tq, tq)
            k2 = k_ref[pl.ds(start, tq), :].astype(F32)
            v2 = v_ref[pl.ds(start, tq), :].astype(F32)
            cum = cum_ref[pl.ds(start, tq), :]
            for hh in range(2):
                ck = head_col(cum, hp * 2 + hh)
                kmod_ref[hh, pl.ds(start, tq), :] = with_bias(k2, hh, _split3(-ck * LOG2E), True)
                in_head = (lane >= hh * FOX_DH) & (lane < (hh + 1) * FOX_DH)
                vm = jnp.where(in_head, v2, jnp.where(lane == (1 - hh) * FOX_DH, 1.0, 0.0))
                vmod_ref[hh, pl.ds(start, tq), :] = vm.astype(BF16)
            return carry
        lax.fori_loop(0, S // tq, chunk, 0)

    q2 = q_ref[...].astype(F32)
    cumq = cum_ref[pl.ds(pl.multiple_of(i * tq, tq), tq), :]
    qmods = [with_bias(q2, hh, _split3(head_col(cumq, hp * 2 + hh) * LOG2E), False) for hh in range(2)]
    rr = lax.broadcasted_iota(jnp.int32, (tq // FOX_ROW_SPLIT, tq), 0)
    cc = lax.broadcasted_iota(jnp.int32, (tq // FOX_ROW_SPLIT, tq), 1)
    m_ref[...] = jnp.full(m_ref.shape, NEG_BIG, F32)
    acc_ref[...] = jnp.zeros(acc_ref.shape, F32)

    RS = FOX_ROW_SPLIT
    tr = tq // RS

    def step(j, masked):
        start = pl.multiple_of(j * tq, tq)
        chains = [(hh, r) for hh in range(2) for r in range(RS)]
        scores = [_dot_nt(qmods[hh][r * tr:(r + 1) * tr], kmod_ref[hh, pl.ds(start, tq), :]) for hh, r in chains]
        for (hh, r), s in zip(chains, scores):
            rows = slice(r * tr, (r + 1) * tr)
            if masked:
                s = jnp.where(rr + r * tr >= cc, s, NEG_BIG)
            m_prev = m_ref[hh, rows, :]
            m_new = jnp.maximum(m_prev, jnp.max(s, axis=1, keepdims=True))
            p = jnp.exp2(s - jnp.concatenate([m_new] * (tq // LANES), axis=1)).astype(BF16)
            acc_ref[hh, rows, :] = (jnp.exp2(m_prev - m_new) * acc_ref[hh, rows, :]
                                    + _dot(p, vmod_ref[hh, pl.ds(start, tq), :]))
            m_ref[hh, rows, :] = m_new

    UN = FOX_TILES_PER_ITER

    def body(jj, carry):
        for t in range(UN):
            step(UN * jj + t, False)
        return carry

    def single(j, carry):
        step(j, False)
        return carry

    nfull = i // UN
    lax.fori_loop(0, nfull, body, 0)
    lax.fori_loop(nfull * UN, i, single, 0)
    step(i, True)
    outs = []
    for hh in range(2):
        acc = acc_ref[hh]
        outs.append(acc / head_col(acc, (1 - hh) * FOX_DH))
    o_ref[...] = jnp.where(lane < FOX_DH, outs[0], outs[1]).astype(o_ref.dtype)


def _fox(fqkv, cum, tq):
    B, S, _ = fqkv.shape
    nhp = FOX_HEADS // 2
    return pl.pallas_call(
        functools.partial(_fox_kernel, tq=tq),
        grid=(B, nhp, S // tq),
        in_specs=[pl.BlockSpec((None, tq, LANES), lambda b, hp, i: (b, i, hp)),
                  pl.BlockSpec((None, S, LANES), lambda b, hp, i: (b, 0, nhp + hp)),
                  pl.BlockSpec((None, S, LANES), lambda b, hp, i: (b, 0, 2 * nhp + hp)),
                  pl.BlockSpec((None, S, LANES), lambda b, hp, i: (b, 0, 0))],
        out_specs=pl.BlockSpec((None, tq, LANES), lambda b, hp, i: (b, i, hp)),
        out_shape=jax.ShapeDtypeStruct((B, S, FOX_W), BF16),
        scratch_shapes=[pltpu.VMEM((2, S, LANES), BF16), pltpu.VMEM((2, S, LANES), BF16),
                        pltpu.VMEM((2, tq, LANES), F32), pltpu.VMEM((2, tq, LANES), F32)],
        compiler_params=_params(("arbitrary", "arbitrary", "arbitrary")),
        name="fox",
    )(fqkv, fqkv, fqkv, cum)


def _level_masks(n):
    r = lax.broadcasted_iota(jnp.int32, (n, n), 0)
    c = lax.broadcasted_iota(jnp.int32, (n, n), 1)
    masks = []
    s = 1
    while s < n:
        sel = ((r // (2 * s)) == (c // (2 * s))) & (((r // s) % 2) == 1) & (((c // s) % 2) == 0)
        masks.append(jnp.where(sel, 1.0, 0.0).astype(F32))
        s *= 2
    return masks


def _unit_lower_inverses(lows, masks):
    n = lows[0].shape[0]
    r = lax.broadcasted_iota(jnp.int32, (n, n), 0)
    c = lax.broadcasted_iota(jnp.int32, (n, n), 1)
    xs = [jnp.where(r == c, 1.0, 0.0).astype(F32)] * len(lows)
    for mask in masks:
        xbs = [x.astype(BF16) for x in xs]
        xcs = [_dot(xb, (low * mask).astype(BF16)) for xb, low in zip(xbs, lows)]
        xs = [x - _dot(xc.astype(BF16), xb) for x, xc, xb in zip(xs, xcs, xbs)]
    return xs


def _gdn_kernel(x_ref, gz_ref, sm_ref, cw_ref, nw_ref, o_ref, state_ref, carry_ref):
    ci = pl.program_id(0)
    NB = x_ref.shape[0]
    b0 = pl.program_id(1) * NB
    C = GDN_CHUNK

    @pl.when(ci == 0)
    def _():
        for bb in range(NB):
            state_ref[b0 + bb] = jnp.zeros(state_ref.shape[1:], F32)
            carry_ref[b0 + bb] = jnp.zeros(carry_ref.shape[1:], F32)

    row8 = lax.broadcasted_iota(jnp.int32, (8, 1), 0)
    r = lax.broadcasted_iota(jnp.int32, (C, C), 0)
    c = lax.broadcasted_iota(jnp.int32, (C, C), 1)
    causal = r >= c
    strict = r > c
    tri = jnp.where(causal, 1.0, 0.0).astype(F32)
    nw = nw_ref[...]
    masks = _level_masks(C)

    ys, sms, gcums, gcum_ts = [], [], [], []
    for bb in range(NB):
        x = x_ref[bb]
        prev = carry_ref[b0 + bb]
        y = cw_ref[CONV_W - 1:CONV_W, :] * x
        for j in range(1, CONV_W):
            xr = pltpu.roll(x, j, 0)
            pr = pltpu.roll(prev, j, 0)
            first = jnp.where(row8 < j, pr, xr[0:8, :])
            y = y + cw_ref[CONV_W - 1 - j:CONV_W - j, :] * jnp.concatenate([first, xr[8:, :]], axis=0)
        carry_ref[b0 + bb] = x[C - 8:C, :]
        ys.append(_silu(y))
        sms.append(sm_ref[bb])
        gcums.append(_dot(tri, sms[bb], HIGHEST))
        gcum_ts.append(gcums[bb].T)

    streams = [(bb, h) for bb in range(NB) for h in range(GDN_HEADS)]
    idx = range(len(streams))
    qs, ks, vs, betas, g_cols, g_lasts, decays = [], [], [], [], [], [], []
    for bb, h in streams:
        y, sm, gcum = ys[bb], sms[bb], gcums[bb]
        q = y[:, h * GDN_DK:(h + 1) * GDN_DK]
        k = y[:, GDN_KW + h * GDN_DK:GDN_KW + (h + 1) * GDN_DK]
        qs.append(q * lax.rsqrt(jnp.sum(q * q, axis=1, keepdims=True) + EPS) * (GDN_DK ** -0.5))
        ks.append(k * lax.rsqrt(jnp.sum(k * k, axis=1, keepdims=True) + EPS))
        vs.append(y[:, 2 * GDN_KW + h * GDN_DV:2 * GDN_KW + (h + 1) * GDN_DV])
        betas.append(sm[:, FOX_HEADS + h:FOX_HEADS + h + 1])
        gl = FOX_HEADS + GDN_HEADS + h
        g_cols.append(gcum[:, gl:gl + 1])
        g_lasts.append(gcum[C - 1:C, gl:gl + 1])
        diff = g_cols[-1] - gcum_ts[bb][gl:gl + 1, :]
        decays.append(jnp.where(causal, jnp.exp(jnp.where(causal, diff, 0.0)), 0.0))
    kbs = [ks[i] * betas[i] for i in idx]
    egs = [jnp.exp(g_cols[i]) for i in idx]
    kqs = [_dot_nt(jnp.concatenate([kbs[i], qs[i]], axis=0).astype(BF16), ks[i].astype(BF16)) for i in idx]
    lows = [jnp.where(strict, kqs[i][:C] * decays[i], 0.0) for i in idx]
    attns = [(kqs[i][C:] * decays[i]).astype(BF16) for i in idx]
    tinvs = _unit_lower_inverses(lows, masks)
    uws = [_dot(tinvs[i].astype(BF16),
                jnp.concatenate([vs[i] * betas[i], kbs[i] * egs[i]], axis=1).astype(BF16)) for i in idx]
    sts = [state_ref[b0 + bb, h] for bb, h in streams]
    wqs = [_dot(jnp.concatenate([uws[i][:, GDN_DV:], qs[i] * egs[i]], axis=0).astype(BF16),
                sts[i].astype(BF16)) for i in idx]
    v_news = [(uws[i][:, :GDN_DV] - wqs[i][:C]).astype(BF16) for i in idx]
    k_dec_ts = [(ks[i] * jnp.exp(g_lasts[i] - g_cols[i])).T.astype(BF16) for i in idx]
    os_ = [wqs[i][C:] + _dot(attns[i], v_news[i]) for i in idx]
    new_sts = [sts[i] * jnp.exp(g_lasts[i]) + _dot(k_dec_ts[i], v_news[i]) for i in idx]
    for i, (bb, h) in enumerate(streams):
        state_ref[b0 + bb, h] = new_sts[i]
        o = _rms(os_[i], nw) * _silu(gz_ref[bb, :, h * GDN_DV:(h + 1) * GDN_DV])
        o_ref[bb, :, h * GDN_DV:(h + 1) * GDN_DV] = o.astype(o_ref.dtype)


def _gdn(gqkv, gz, small, conv_w, norm_w):
    B, S, _ = gqkv.shape
    C = GDN_CHUNK
    nb = _pick(B, 4)
    tile = lambda n: pl.BlockSpec((nb, C, n), lambda ci, b: (b, ci, 0))
    full = lambda a: pl.BlockSpec(a.shape, lambda ci, b: (0,) * a.ndim)
    return pl.pallas_call(
        _gdn_kernel,
        grid=(S // C, B // nb),
        in_specs=[tile(3 * GDN_KW), tile(GDN_VW), tile(LANES), full(conv_w), full(norm_w)],
        out_specs=tile(GDN_VW),
        out_shape=jax.ShapeDtypeStruct((B, S, GDN_VW), BF16),
        scratch_shapes=[pltpu.VMEM((B, GDN_HEADS, GDN_DK, GDN_DV), F32),
                        pltpu.VMEM((B, 8, 3 * GDN_KW), F32)],
        compiler_params=_params(("arbitrary", "arbitrary")),
        name="gdn",
    )(gqkv, gz, small, conv_w, norm_w)


def _outproj_kernel(ya_ref, yb_ref, gates_ref, x_ref, gt1_ref, sc2_ref, sh2_ref, gpm_ref, gpf_ref,
                    wa_ref, wb_ref, wo_ref, wq_ref, x1_ref, h2t_ref, qy_ref):
    ya = _dot(ya_ref[...], wa_ref[...])
    yb = _dot(yb_ref[...], wb_ref[...])
    gates = gates_ref[...]
    mix_in = gates[:, :D_MODEL] * ya + gates[:, D_MODEL:] * yb
    mix = _dot(mix_in.astype(BF16), wo_ref[...])
    x1 = x_ref[...] + gt1_ref[...] * _rms(mix, gpm_ref[...])
    x1_ref[...] = x1
    h2 = _rms(x1, gpf_ref[...]) * (1.0 + sc2_ref[...]) + sh2_ref[...]
    h2t_ref[...] = h2.T.astype(BF16)
    qy_ref[...] = _dot(h2.astype(BF16), wq_ref[...])


def _outproj(ya, yb, gates, x, gt1, sc2, sh2, gpm, gpf, wa, wb, wo, wq, tm):
    B, S, D = x.shape
    full = lambda a: pl.BlockSpec(a.shape, lambda b, i: (0,) * a.ndim)
    mod = pl.BlockSpec((None, 1, D), lambda b, i: (b, 0, 0))
    tile = lambda n: pl.BlockSpec((None, tm, n), lambda b, i: (b, i, 0))
    nq = wq.shape[1]
    return pl.pallas_call(
        _outproj_kernel,
        grid=(B, S // tm),
        in_specs=[tile(FOX_W), tile(GDN_VW), tile(2 * D), tile(D), mod, mod, mod, full(gpm), full(gpf),
                  full(wa), full(wb), full(wo), full(wq)],
        out_specs=[tile(D), pl.BlockSpec((D, tm), lambda b, i: (0, b * (S // tm) + i)), tile(nq)],
        out_shape=[jax.ShapeDtypeStruct((B, S, D), F32),
                   jax.ShapeDtypeStruct((D, B * S), BF16),
                   jax.ShapeDtypeStruct((B, S, nq), F32)],
        compiler_params=_params(("arbitrary", "arbitrary")),
        name="outproj",
    )(ya, yb, gates, x, gt1, sc2, sh2, gpm, gpf, wa, wb, wo, wq)


SUBLANES = 8


def _cmpx(a, i, l, desc):
    hi, lo = jnp.maximum(a[i], a[l]), jnp.minimum(a[i], a[l])
    a[i], a[l] = (hi, lo) if desc else (lo, hi)


def _top16_desc(a):
    a = list(a)
    n = len(a)
    k = 2
    while k <= n:
        j = k // 2
        while j >= 1:
            for i in range(n):
                l = i ^ j
                if l > i:
                    _cmpx(a, i, l, (i & k) == 0)
            j //= 2
        k *= 2
    shift = SUBLANES // 2
    while shift >= 1:
        other = [pltpu.roll(v, shift, 0) for v in a]
        a = [jnp.maximum(a[i], other[n - 1 - i]) for i in range(n)]
        j = n // 2
        while j >= 1:
            for i in range(n):
                l = i ^ j
                if l > i:
                    _cmpx(a, i, l, True)
            j //= 2
        shift //= 2
    return a


def _topk_kernel(qy_ref, k1_ref, k2_ref, theta_ref, e1_ref, s2_ref, m2_ref):
    K = PEER_TOPK
    half = PEER_DQ // 2
    tt = qy_ref.shape[0]
    G = N_KEYS // SUBLANES
    sub = lax.broadcasted_iota(jnp.int32, (SUBLANES, LANES), 0)

    def stagger(d, base):
        out = d[base]
        for r in range(1, SUBLANES):
            out = jnp.where(sub == r, d[base + r], out)
        return out

    def head(h, carry):
        col = pl.multiple_of(h * PEER_DQ, PEER_DQ)
        q1 = qy_ref[:, pl.ds(col, half)]
        q2 = qy_ref[:, pl.ds(col + half, half)]
        s1 = _dot_nt(k1_ref[...], q1, HIGHEST)
        s2 = _dot_nt(k2_ref[...], q2, HIGHEST)
        for c in range(tt // LANES):
            cs = slice(c * LANES, (c + 1) * LANES)
            s2_ref[h, c] = s2[:, cs]
            s1g = [s1[SUBLANES * g:SUBLANES * (g + 1), cs] for g in range(G)]
            d1 = _top16_desc(s1g)
            d2 = _top16_desc([s2[SUBLANES * g:SUBLANES * (g + 1), cs] for g in range(G)])
            d2_lo, d2_hi, d1_hi = stagger(d2, 0), stagger(d2, SUBLANES), stagger(d1, SUBLANES)
            cand = [d1[0] + d2_lo, d1[0] + d2_hi, d1_hi + d2[0]] + [d1[i] + d2_lo for i in range(1, SUBLANES)]
            cand += [jnp.full((SUBLANES, LANES), -jnp.inf, F32)] * (K - len(cand))
            top = _top16_desc(cand)
            z = jnp.ones_like(top[0])
            for i in range(1, K):
                z = z + jnp.exp(top[i] - top[0])
            zinv = 1.0 / z
            tau = top[K - 1]
            m2_ref[h, :, cs] = d2[0][0:1, :]
            for g in range(G):
                theta = jnp.full((SUBLANES, LANES), jnp.inf, F32)
                for j in range(K):
                    theta = jnp.where(s1g[g] + d2[j] >= tau, d2[j], theta)
                theta_ref[h, c, SUBLANES * g:SUBLANES * (g + 1), :] = theta
                e1_ref[h, c, SUBLANES * g:SUBLANES * (g + 1), :] = jnp.exp(s1g[g] - d1[0]) * zinv
        return carry

    lax.fori_loop(0, PEER_HEADS, head, 0, unroll=4)


def _topk(qy, keys1, keys2, tt):
    T, nq = qy.shape
    H = PEER_HEADS
    full = lambda a: pl.BlockSpec(a.shape, lambda i: (0,) * a.ndim)
    big = pl.BlockSpec((H, tt // LANES, N_KEYS, LANES), lambda i: (0, i, 0, 0))
    return pl.pallas_call(
        _topk_kernel,
        grid=(T // tt,),
        in_specs=[pl.BlockSpec((tt, nq), lambda i: (i, 0)), full(keys1), full(keys2)],
        out_specs=[big, big, big, pl.BlockSpec((H, 1, tt), lambda i: (0, 0, i))],
        out_shape=[jax.ShapeDtypeStruct((H, T // LANES, N_KEYS, LANES), F32)] * 3
        + [jax.ShapeDtypeStruct((H, 1, T), F32)],
        compiler_params=_params(("arbitrary",)),
        name="topk",
    )(qy, keys1, keys2)


def _peer_kernel(h2t_ref, u_ref, vt_ref, theta_ref, e1_ref, s2_ref, m2_ref, x1_ref, gt2_ref, g_ref, o_ref,
                 acc_ref, e2_ref, st_ref, wt_ref, *, na):
    j = pl.program_id(1)
    H = PEER_HEADS
    tt = h2t_ref.shape[1]
    vreg = (SUBLANES, LANES)
    ncol = tt // LANES
    nkb = N_KEYS // GATE_KB
    blk = (GATE_KB // SUBLANES, SUBLANES, LANES)

    @pl.when(j == 0)
    def _():
        acc_ref[...] = jnp.zeros(acc_ref.shape, F32)
        for h in range(H):
            for c in range(ncol):
                e2_ref[h, c] = jnp.exp(s2_ref[h, c] - m2_ref[h, :, c * LANES:(c + 1) * LANES])

    st = _dot(u_ref[...], h2t_ref[...])
    st = 0.5 * st * (1.0 + lax.erf(st * (0.5 ** 0.5)))
    for c in range(ncol):
        st_ref[c] = st[:, c * LANES:(c + 1) * LANES].reshape(na, nkb, GATE_KB, LANES)

    def gate_blocks(a0, nab):
        def block(i, carry):
            c = lax.shift_right_logical(i, nkb.bit_length() - 1)
            kb = lax.bitwise_and(i, nkb - 1)
            was = [jnp.zeros(blk, F32) for _ in range(nab)]
            for h in range(H):
                s2 = s2_ref[h, c, kb].reshape(blk)
                e2 = e2_ref[h, c, kb].reshape(blk)
                for t in range(nab):
                    thb = jnp.broadcast_to(theta_ref[h, c, a0 + t:a0 + t + 1, :], vreg)
                    e1b = jnp.broadcast_to(e1_ref[h, c, a0 + t:a0 + t + 1, :], vreg)
                    was[t] = was[t] + jnp.where(s2 >= thb, e1b * e2, 0.0)
            for t in range(nab):
                act = st_ref[c, a0 + t, kb].reshape(blk)
                wt_ref[c, a0 + t, kb] = (was[t] * act).reshape(GATE_KB, LANES).astype(BF16)
            return carry

        lax.fori_loop(0, ncol * nkb, block, 0, unroll=4)

    for a0 in range(0, na, GATE_AB):
        gate_blocks(a0, GATE_AB)
    gated = jnp.concatenate([wt_ref[c].reshape(na * N_KEYS, LANES) for c in range(ncol)], axis=1)
    acc_ref[...] += _dot(vt_ref[...], gated)

    @pl.when(j == pl.num_programs(1) - 1)
    def _():
        y = acc_ref[...].T
        o_ref[...] = x1_ref[...] + gt2_ref[...] * _rms(y, g_ref[...])


def _peer(h2t, u_b, vt_b, theta, e1, s2, m2, x1, gt2, g_post, tt, ec, tiles_per_batch):
    D, T = h2t.shape
    E = u_b.shape[0]
    H = PEER_HEADS
    na = ec // N_KEYS
    ncol = tt // LANES
    nkb = N_KEYS // GATE_KB
    s2 = s2.reshape(H, T // LANES, nkb, GATE_KB, LANES)
    rows = pl.BlockSpec((H, ncol, na, LANES), lambda i, j: (0, i, j, 0))
    slab = (ncol, na, nkb, GATE_KB, LANES)
    return pl.pallas_call(
        functools.partial(_peer_kernel, na=na),
        grid=(T // tt, E // ec),
        in_specs=[pl.BlockSpec((D, tt), lambda i, j: (0, i)),
                  pl.BlockSpec((ec, D), lambda i, j: (j, 0)),
                  pl.BlockSpec((D, ec), lambda i, j: (0, j)),
                  rows, rows,
                  pl.BlockSpec((H, ncol, nkb, GATE_KB, LANES), lambda i, j: (0, i, 0, 0, 0)),
                  pl.BlockSpec((H, 1, tt), lambda i, j: (0, 0, i)),
                  pl.BlockSpec((tt, D), lambda i, j: (i, 0)),
                  pl.BlockSpec((None, 1, D), lambda i, j: (i // tiles_per_batch, 0, 0)),
                  pl.BlockSpec(g_post.shape, lambda i, j: (0, 0))],
        out_specs=pl.BlockSpec((tt, D), lambda i, j: (i, 0)),
        out_shape=jax.ShapeDtypeStruct((T, D), F32),
        scratch_shapes=[pltpu.VMEM((D, tt), F32), pltpu.VMEM((H, ncol, nkb, GATE_KB, LANES), F32),
                        pltpu.VMEM(slab, F32), pltpu.VMEM(slab, BF16)],
        compiler_params=_params(("arbitrary", "arbitrary")),
        name="peer",
    )(h2t, u_b, vt_b, theta, e1, s2, m2, x1, gt2, g_post)


def _pick(n, pref):
    t = pref
    while n % t:
        t //= 2
    return t


def _layer(x, c_pad, w_ada, b_ada, g_pre_mix, g_post_mix, w_in, b_fgate, fox_w_o, conv_w, a_log, dt_bias,
           gdn_norm_w, gdn_w_o, b_branch_gate, w_out, g_pre_ffn, g_post_ffn, peer_w_query, peer_keys1,
           peer_keys2, peer_u, peer_v):
    B, S, D = x.shape
    T = B * S
    mod = _ada(c_pad, w_ada, b_ada.reshape(1, -1))[:B]
    sh1, sc1, gt1, sh2, sc2, gt2 = [m.reshape(B, 1, D) for m in jnp.split(mod, 6, axis=-1)]

    off = [0]
    for s in SPLITS:
        off.append(off[-1] + s)
    col = lambda i, j: w_in[:, off[i]:off[j]]
    wf = col(0, 3).astype(BF16)
    wg = jnp.concatenate([col(4, 8)], axis=1).astype(BF16)
    wgate = col(10, 12).astype(BF16)
    pad = LANES - (FOX_HEADS + 2 * GDN_HEADS)
    ws = jnp.concatenate([col(3, 4), col(8, 10), jnp.zeros((D, pad), F32)], axis=1).astype(BF16)
    zpad = jnp.zeros((pad,), F32)
    bsm = jnp.concatenate([b_fgate, jnp.zeros((GDN_HEADS,), F32), dt_bias, zpad]).reshape(1, LANES)
    alog_row = jnp.concatenate([jnp.zeros((FOX_HEADS + GDN_HEADS,), F32), a_log, zpad]).reshape(1, LANES)
    bgate = b_branch_gate.reshape(1, 2 * D)

    tm = _pick(S, 512)
    fqkv, gqkv, gz, gates, small = _inproj(x, sc1, sh1, g_pre_mix.reshape(1, D), wf, wg, wgate, ws, bgate,
                                           bsm, alog_row, tm)
    cum = _cumsum(small, _pick(S, 512))
    ya = _fox(fqkv, cum, _pick(S, 512))
    yb = _gdn(gqkv, gz, small, conv_w, gdn_norm_w.reshape(1, GDN_DV))
    x1, h2t, qy = _outproj(ya, yb, gates, x, gt1, sc2, sh2, g_post_mix.reshape(1, D), g_pre_ffn.reshape(1, D),
                          fox_w_o.astype(BF16), gdn_w_o.astype(BF16), w_out.astype(BF16),
                          peer_w_query.astype(BF16), tm)
    qy = qy.reshape(T, -1)
    theta, e1, s2, m2 = _topk(qy, peer_keys1, peer_keys2, _pick(T, 256))
    tt = _pick(S, 512)
    out = _peer(h2t, peer_u.astype(BF16), peer_v.T.astype(BF16), theta, e1, s2, m2,
                x1.reshape(T, D), gt2, g_post_ffn.reshape(1, D), tt, 2048, S // tt)
    return out.reshape(B, S, D)


def kernel(x, c, w_ada, b_ada, g_pre_mix, g_post_mix, w_in, b_fgate, fox_w_o, conv_w, a_log, dt_bias,
           gdn_norm_w, gdn_w_o, b_branch_gate, w_out, g_pre_ffn, g_post_ffn, peer_w_query, peer_keys1,
           peer_keys2, peer_u, peer_v):
    B = x.shape[0]
    c_pad = jnp.pad(c, ((0, (-B) % 8), (0, 0)))
    for l in range(w_ada.shape[0]):
        x = _layer(x, c_pad, w_ada[l], b_ada[l], g_pre_mix[l], g_post_mix[l], w_in[l], b_fgate[l],
                   fox_w_o[l], conv_w[l], a_log[l], dt_bias[l], gdn_norm_w[l], gdn_w_o[l], b_branch_gate[l],
                   w_out[l], g_pre_ffn[l], g_post_ffn[l], peer_w_query[l], peer_keys1[l], peer_keys2[l],
                   peer_u[l], peer_v[l])
    return x
```

```python
import functools

import jax
import jax.numpy as jnp
from jax import lax
from jax.experimental import pallas as pl
from jax.experimental.pallas import tpu as pltpu

F32 = jnp.float32
BF16 = jnp.bfloat16
HIGHEST = lax.Precision.HIGHEST

D_MODEL = 1024
EPS = 1e-6
FOX_HEADS = 8
FOX_DH = 64
GDN_HEADS = 4
GDN_DK = 128
GDN_DV = 128
CONV_W = 4
PEER_HEADS = 8
N_KEYS = 128
PEER_DQ = 256
PEER_TOPK = 16
FOX_W = FOX_HEADS * FOX_DH
GDN_KW = GDN_HEADS * GDN_DK
GDN_VW = GDN_HEADS * GDN_DV
SPLITS = (FOX_W, FOX_W, FOX_W, FOX_HEADS, GDN_KW, GDN_KW, GDN_VW, GDN_VW, GDN_HEADS, GDN_HEADS,
          D_MODEL, D_MODEL)

LANES = 128
SUBLANES = 8
GDN_CHUNK = 128
FOX_TILES_PER_ITER = 4
GATE_KB = 16
GATE_AB = 8
NEG_BIG = -1e30
LOG2E = 1.4426950408889634
VMEM_LIMIT = 56 * 1024 * 1024


def _params(sem):
    return pltpu.CompilerParams(dimension_semantics=sem, vmem_limit_bytes=VMEM_LIMIT)


def _dot(a, b, precision=None):
    return jnp.dot(a, b, preferred_element_type=F32, precision=precision)


def _dot_nt(a, b, precision=None):
    return lax.dot_general(a, b, (((1,), (1,)), ((), ())), preferred_element_type=F32,
                           precision=precision)


def _sigmoid(x):
    return 1.0 / (1.0 + jnp.exp(-x))


def _silu(x):
    return x * _sigmoid(x)


def _softplus(x):
    return jnp.maximum(x, 0.0) + jnp.log(1.0 + jnp.exp(-jnp.abs(x)))


def _log_sigmoid(x):
    return -_softplus(-x)


def _rms(x, g):
    return x * lax.rsqrt(jnp.mean(x * x, axis=-1, keepdims=True) + EPS) * g


def _ada_kernel(c_ref, w_ref, b_ref, o_ref):
    c = c_ref[...]
    o_ref[...] = _dot(_silu(c), w_ref[...], HIGHEST) + b_ref[...]


def _ada(c_pad, w, b):
    n = w.shape[1]
    bn = 1024
    return pl.pallas_call(
        _ada_kernel,
        grid=(n // bn,),
        in_specs=[pl.BlockSpec(c_pad.shape, lambda j: (0, 0)),
                  pl.BlockSpec((D_MODEL, bn), lambda j: (0, j)),
                  pl.BlockSpec((1, bn), lambda j: (0, j))],
        out_specs=pl.BlockSpec((c_pad.shape[0], bn), lambda j: (0, j)),
        out_shape=jax.ShapeDtypeStruct((c_pad.shape[0], n), F32),
        compiler_params=_params(("arbitrary",)),
        name="ada",
    )(c_pad, w, b)


def _inproj_kernel(x_ref, sc_ref, sh_ref, g_ref, wf_ref, wg_ref, wgate_ref, ws_ref, bgate_ref,
                   bsm_ref, alog_ref, fqkv_ref, gqkv_ref, gz_ref, gates_ref, small_ref):
    x = x_ref[...]
    h = _rms(x, g_ref[...]) * (1.0 + sc_ref[...]) + sh_ref[...]
    hb = h.astype(BF16)
    pf = _dot(hb, wf_ref[...])
    lane = lax.broadcasted_iota(jnp.int32, (1, 3 * FOX_W), 1)
    pf = jnp.where(lane < FOX_W, pf * (FOX_DH ** -0.5 * LOG2E), pf)
    fqkv_ref[...] = pf.astype(BF16)
    pg = _dot(hb, wg_ref[...])
    gqkv_ref[...] = pg[:, :3 * GDN_KW]
    gz_ref[...] = pg[:, 3 * GDN_KW:]
    gates_ref[...] = _sigmoid(_dot(hb, wgate_ref[...]) + bgate_ref[...])
    z = _dot(hb, ws_ref[...]) + bsm_ref[...]
    sl = lax.broadcasted_iota(jnp.int32, (1, LANES), 1)
    neg_a = -jnp.exp(alog_ref[...])
    small_ref[...] = jnp.where(sl < FOX_HEADS, _log_sigmoid(z),
                               jnp.where(sl < FOX_HEADS + GDN_HEADS, _sigmoid(z), neg_a * _softplus(z)))


def _inproj(x, sc1, sh1, g_pre, wf, wg, wgate, ws, bgate, bsm, alog_row, tm):
    B, S, D = x.shape
    full = lambda a: pl.BlockSpec(a.shape, lambda b, i: (0,) * a.ndim)
    mod = pl.BlockSpec((None, 1, D), lambda b, i: (b, 0, 0))
    tile = lambda n: pl.BlockSpec((None, tm, n), lambda b, i: (b, i, 0))
    return pl.pallas_call(
        _inproj_kernel,
        grid=(B, S // tm),
        in_specs=[tile(D), mod, mod, full(g_pre), full(wf), full(wg), full(wgate), full(ws),
                  full(bgate), full(bsm), full(alog_row)],
        out_specs=[tile(3 * FOX_W), tile(3 * GDN_KW), tile(GDN_VW), tile(2 * D), tile(LANES)],
        out_shape=[jax.ShapeDtypeStruct((B, S, 3 * FOX_W), BF16),
                   jax.ShapeDtypeStruct((B, S, 3 * GDN_KW), F32),
                   jax.ShapeDtypeStruct((B, S, GDN_VW), F32),
                   jax.ShapeDtypeStruct((B, S, 2 * D), F32),
                   jax.ShapeDtypeStruct((B, S, LANES), F32)],
        compiler_params=_params(("arbitrary", "arbitrary")),
        name="inproj",
    )(x, sc1, sh1, g_pre, wf, wg, wgate, ws, bgate, bsm, alog_row)


def _cumsum_kernel(s_ref, o_ref, carry_ref):
    @pl.when(pl.program_id(1) == 0)
    def _():
        carry_ref[...] = jnp.zeros_like(carry_ref)

    v = s_ref[...]
    n = v.shape[0]
    r = lax.broadcasted_iota(jnp.int32, (n, n), 0)
    c = lax.broadcasted_iota(jnp.int32, (n, n), 1)
    tri = jnp.where(r >= c, 1.0, 0.0).astype(F32)
    cs = _dot(tri, v, HIGHEST) + carry_ref[...]
    o_ref[...] = cs
    carry_ref[...] = cs[n - 1:n, :]


def _cumsum(small, tc):
    B, S, L = small.shape
    return pl.pallas_call(
        _cumsum_kernel,
        grid=(B, S // tc),
        in_specs=[pl.BlockSpec((None, tc, L), lambda b, i: (b, i, 0))],
        out_specs=pl.BlockSpec((None, tc, L), lambda b, i: (b, i, 0)),
        out_shape=jax.ShapeDtypeStruct((B, S, L), F32),
        scratch_shapes=[pltpu.VMEM((1, L), F32)],
        compiler_params=_params(("arbitrary", "arbitrary")),
        name="cumsum",
    )(small)


def _split3(x):
    p1 = x.astype(BF16).astype(F32)
    r = x - p1
    p2 = r.astype(BF16).astype(F32)
    p3 = (r - p2).astype(BF16).astype(F32)
    return p1, p2, p3


def _fox_kernel(q_ref, k_ref, v_ref, cum_ref, o_ref, kmod_ref, vmod_ref, m_ref, acc_ref, *, tq):
    hp = pl.program_id(1)
    i = pl.program_id(2)
    S = k_ref.shape[0]
    lane = lax.broadcasted_iota(jnp.int32, (1, LANES), 1)

    def head_col(tile, h):
        return jnp.sum(jnp.where(lane == h, tile, 0.0), axis=1, keepdims=True)

    def with_bias(base, hh, pieces, ones_first):
        sp = (1 - hh) * FOX_DH
        in_head = (lane >= hh * FOX_DH) & (lane < (hh + 1) * FOX_DH)
        out = jnp.where(in_head, base, 0.0)
        po, oo = (3, 0) if ones_first else (0, 3)
        for t in range(3):
            out = jnp.where(lane == sp + po + t, pieces[t], out)
            out = jnp.where(lane == sp + oo + t, 1.0, out)
        return out.astype(BF16)

    @pl.when(i == 0)
    def _():
        def chunk(ci, carry):
            start = pl.multiple_of(ci * tq, tq)
            k2 = k_ref[pl.ds(start, tq), :].astype(F32)
            v2 = v_ref[pl.ds(start, tq), :].astype(F32)
            cum = cum_ref[pl.ds(start, tq), :]
            for hh in range(2):
                ck = head_col(cum, hp * 2 + hh)
                kmod_ref[hh, pl.ds(start, tq), :] = with_bias(k2, hh, _split3(-ck * LOG2E), True)
                in_head = (lane >= hh * FOX_DH) & (lane < (hh + 1) * FOX_DH)
                vm = jnp.where(in_head, v2, jnp.where(lane == (1 - hh) * FOX_DH, 1.0, 0.0))
                vmod_ref[hh, pl.ds(start, tq), :] = vm.astype(BF16)
            return carry
        lax.fori_loop(0, S // tq, chunk, 0)

    q2 = q_ref[...].astype(F32)
    cumq = cum_ref[pl.ds(pl.multiple_of(i * tq, tq), tq), :]
    qmods = [with_bias(q2, hh, _split3(head_col(cumq, hp * 2 + hh) * LOG2E), False) for hh in range(2)]
    rr = lax.broadcasted_iota(jnp.int32, (tq, tq), 0)
    cc = lax.broadcasted_iota(jnp.int32, (tq, tq), 1)
    m_ref[...] = jnp.full(m_ref.shape, NEG_BIG, F32)
    acc_ref[...] = jnp.zeros(acc_ref.shape, F32)

    def step(j, masked):
        start = pl.multiple_of(j * tq, tq)
        scores = [_dot_nt(qmods[hh], kmod_ref[hh, pl.ds(start, tq), :]) for hh in range(2)]
        for hh in range(2):
            s = scores[hh]
            if masked:
                s = jnp.where(rr >= cc, s, NEG_BIG)
            m_prev = m_ref[hh]
            m_new = jnp.maximum(m_prev, jnp.max(s, axis=1, keepdims=True))
            p = jnp.exp2(s - jnp.concatenate([m_new] * (tq // LANES), axis=1)).astype(BF16)
            acc_ref[hh] = jnp.exp2(m_prev - m_new) * acc_ref[hh] + _dot(p, vmod_ref[hh, pl.ds(start, tq), :])
            m_ref[hh] = m_new

    UN = FOX_TILES_PER_ITER

    def body(jj, carry):
        for t in range(UN):
            step(UN * jj + t, False)
        return carry

    def single(j, carry):
        step(j, False)
        return carry

    nfull = i // UN
    lax.fori_loop(0, nfull, body, 0)
    lax.fori_loop(nfull * UN, i, single, 0)
    step(i, True)
    outs = []
    for hh in range(2):
        acc = acc_ref[hh]
        outs.append(acc / head_col(acc, (1 - hh) * FOX_DH))
    o_ref[...] = jnp.where(lane < FOX_DH, outs[0], outs[1]).astype(o_ref.dtype)


def _fox(fqkv, cum, tq):
    B, S, _ = fqkv.shape
    nhp = FOX_HEADS // 2
    return pl.pallas_call(
        functools.partial(_fox_kernel, tq=tq),
        grid=(B, nhp, S // tq),
        in_specs=[pl.BlockSpec((None, tq, LANES), lambda b, hp, i: (b, i, hp)),
                  pl.BlockSpec((None, S, LANES), lambda b, hp, i: (b, 0, nhp + hp)),
                  pl.BlockSpec((None, S, LANES), lambda b, hp, i: (b, 0, 2 * nhp + hp)),
                  pl.BlockSpec((None, S, LANES), lambda b, hp, i: (b, 0, 0))],
        out_specs=pl.BlockSpec((None, tq, LANES), lambda b, hp, i: (b, i, hp)),
        out_shape=jax.ShapeDtypeStruct((B, S, FOX_W), BF16),
        scratch_shapes=[pltpu.VMEM((2, S, LANES), BF16), pltpu.VMEM((2, S, LANES), BF16),
                        pltpu.VMEM((2, tq, LANES), F32), pltpu.VMEM((2, tq, LANES), F32)],
        compiler_params=_params(("arbitrary", "arbitrary", "arbitrary")),
        name="fox",
    )(fqkv, fqkv, fqkv, cum)


def _level_masks(n):
    r = lax.broadcasted_iota(jnp.int32, (n, n), 0)
    c = lax.broadcasted_iota(jnp.int32, (n, n), 1)
    masks = []
    s = 1
    while s < n:
        sel = ((r // (2 * s)) == (c // (2 * s))) & (((r // s) % 2) == 1) & (((c // s) % 2) == 0)
        masks.append(jnp.where(sel, 1.0, 0.0).astype(F32))
        s *= 2
    return masks


def _unit_lower_inverses(lows, masks):
    n = lows[0].shape[0]
    r = lax.broadcasted_iota(jnp.int32, (n, n), 0)
    c = lax.broadcasted_iota(jnp.int32, (n, n), 1)
    xs = [jnp.where(r == c, 1.0, 0.0).astype(F32)] * len(lows)
    for mask in masks:
        xbs = [x.astype(BF16) for x in xs]
        xcs = [_dot(xb, (low * mask).astype(BF16)) for xb, low in zip(xbs, lows)]
        xs = [x - _dot(xc.astype(BF16), xb) for x, xc, xb in zip(xs, xcs, xbs)]
    return xs


def _gdn_kernel(x_ref, gz_ref, sm_ref, cw_ref, nw_ref, o_ref, state_ref, carry_ref):
    ci = pl.program_id(0)
    NB = x_ref.shape[0]
    b0 = pl.program_id(1) * NB
    C = GDN_CHUNK

    @pl.when(ci == 0)
    def _():
        for bb in range(NB):
            state_ref[b0 + bb] = jnp.zeros(state_ref.shape[1:], F32)
            carry_ref[b0 + bb] = jnp.zeros(carry_ref.shape[1:], F32)

    row8 = lax.broadcasted_iota(jnp.int32, (SUBLANES, 1), 0)
    r = lax.broadcasted_iota(jnp.int32, (C, C), 0)
    c = lax.broadcasted_iota(jnp.int32, (C, C), 1)
    causal = r >= c
    strict = r > c
    tri = jnp.where(causal, 1.0, 0.0).astype(F32)
    nw = nw_ref[...]
    masks = _level_masks(C)

    ys, sms, gcums, gcum_ts = [], [], [], []
    for bb in range(NB):
        x = x_ref[bb]
        prev = carry_ref[b0 + bb]
        y = cw_ref[CONV_W - 1:CONV_W, :] * x
        for j in range(1, CONV_W):
            xr = pltpu.roll(x, j, 0)
            pr = pltpu.roll(prev, j, 0)
            first = jnp.where(row8 < j, pr, xr[0:SUBLANES, :])
            y = y + cw_ref[CONV_W - 1 - j:CONV_W - j, :] * jnp.concatenate([first, xr[SUBLANES:, :]], axis=0)
        carry_ref[b0 + bb] = x[C - SUBLANES:C, :]
        ys.append(_silu(y))
        sms.append(sm_ref[bb])
        gcums.append(_dot(tri, sms[bb], HIGHEST))
        gcum_ts.append(gcums[bb].T)

    streams = [(bb, h) for bb in range(NB) for h in range(GDN_HEADS)]
    idx = range(len(streams))
    qs, ks, vs, betas, g_cols, g_lasts, decays = [], [], [], [], [], [], []
    for bb, h in streams:
        y, sm, gcum = ys[bb], sms[bb], gcums[bb]
        q = y[:, h * GDN_DK:(h + 1) * GDN_DK]
        k = y[:, GDN_KW + h * GDN_DK:GDN_KW + (h + 1) * GDN_DK]
        qs.append(q * lax.rsqrt(jnp.sum(q * q, axis=1, keepdims=True) + EPS) * (GDN_DK ** -0.5))
        ks.append(k * lax.rsqrt(jnp.sum(k * k, axis=1, keepdims=True) + EPS))
        vs.append(y[:, 2 * GDN_KW + h * GDN_DV:2 * GDN_KW + (h + 1) * GDN_DV])
        betas.append(sm[:, FOX_HEADS + h:FOX_HEADS + h + 1])
        gl = FOX_HEADS + GDN_HEADS + h
        g_cols.append(gcum[:, gl:gl + 1])
        g_lasts.append(gcum[C - 1:C, gl:gl + 1])
        diff = g_cols[-1] - gcum_ts[bb][gl:gl + 1, :]
        decays.append(jnp.where(causal, jnp.exp(jnp.where(causal, diff, 0.0)), 0.0))
    kbs = [ks[i] * betas[i] for i in idx]
    egs = [jnp.exp(g_cols[i]) for i in idx]
    kqs = [_dot_nt(jnp.concatenate([kbs[i], qs[i]], axis=0).astype(BF16), ks[i].astype(BF16)) for i in idx]
    lows = [jnp.where(strict, kqs[i][:C] * decays[i], 0.0) for i in idx]
    attns = [(kqs[i][C:] * decays[i]).astype(BF16) for i in idx]
    tinvs = _unit_lower_inverses(lows, masks)
    uws = [_dot(tinvs[i].astype(BF16),
                jnp.concatenate([vs[i] * betas[i], kbs[i] * egs[i]], axis=1).astype(BF16)) for i in idx]
    sts = [state_ref[b0 + bb, h] for bb, h in streams]
    wqs = [_dot(jnp.concatenate([uws[i][:, GDN_DV:], qs[i] * egs[i]], axis=0).astype(BF16),
                sts[i].astype(BF16)) for i in idx]
    v_news = [(uws[i][:, :GDN_DV] - wqs[i][:C]).astype(BF16) for i in idx]
    k_dec_ts = [(ks[i] * jnp.exp(g_lasts[i] - g_cols[i])).T.astype(BF16) for i in idx]
    os_ = [wqs[i][C:] + _dot(attns[i], v_news[i]) for i in idx]
    new_sts = [sts[i] * jnp.exp(g_lasts[i]) + _dot(k_dec_ts[i], v_news[i]) for i in idx]
    for i, (bb, h) in enumerate(streams):
        state_ref[b0 + bb, h] = new_sts[i]
        o = _rms(os_[i], nw) * _silu(gz_ref[bb, :, h * GDN_DV:(h + 1) * GDN_DV])
        o_ref[bb, :, h * GDN_DV:(h + 1) * GDN_DV] = o.astype(o_ref.dtype)


def _gdn(gqkv, gz, small, conv_w, norm_w):
    B, S, _ = gqkv.shape
    C = GDN_CHUNK
    nb = _pick(B, 4)
    tile = lambda n: pl.BlockSpec((nb, C, n), lambda ci, b: (b, ci, 0))
    full = lambda a: pl.BlockSpec(a.shape, lambda ci, b: (0,) * a.ndim)
    return pl.pallas_call(
        _gdn_kernel,
        grid=(S // C, B // nb),
        in_specs=[tile(3 * GDN_KW), tile(GDN_VW), tile(LANES), full(conv_w), full(norm_w)],
        out_specs=tile(GDN_VW),
        out_shape=jax.ShapeDtypeStruct((B, S, GDN_VW), BF16),
        scratch_shapes=[pltpu.VMEM((B, GDN_HEADS, GDN_DK, GDN_DV), F32),
                        pltpu.VMEM((B, SUBLANES, 3 * GDN_KW), F32)],
        compiler_params=_params(("arbitrary", "arbitrary")),
        name="gdn",
    )(gqkv, gz, small, conv_w, norm_w)


def _outproj_kernel(ya_ref, yb_ref, gates_ref, x_ref, gt1_ref, sc2_ref, sh2_ref, gpm_ref, gpf_ref,
                    wa_ref, wb_ref, wo_ref, wq_ref, x1_ref, h2t_ref, qy_ref):
    ya = _dot(ya_ref[...], wa_ref[...])
    yb = _dot(yb_ref[...], wb_ref[...])
    gates = gates_ref[...]
    mix_in = gates[:, :D_MODEL] * ya + gates[:, D_MODEL:] * yb
    mix = _dot(mix_in.astype(BF16), wo_ref[...])
    x1 = x_ref[...] + gt1_ref[...] * _rms(mix, gpm_ref[...])
    x1_ref[...] = x1
    h2 = _rms(x1, gpf_ref[...]) * (1.0 + sc2_ref[...]) + sh2_ref[...]
    h2t_ref[...] = h2.T.astype(BF16)
    qy_ref[...] = _dot(h2.astype(BF16), wq_ref[...])


def _outproj(ya, yb, gates, x, gt1, sc2, sh2, gpm, gpf, wa, wb, wo, wq, tm):
    B, S, D = x.shape
    full = lambda a: pl.BlockSpec(a.shape, lambda b, i: (0,) * a.ndim)
    mod = pl.BlockSpec((None, 1, D), lambda b, i: (b, 0, 0))
    tile = lambda n: pl.BlockSpec((None, tm, n), lambda b, i: (b, i, 0))
    nq = wq.shape[1]
    return pl.pallas_call(
        _outproj_kernel,
        grid=(B, S // tm),
        in_specs=[tile(FOX_W), tile(GDN_VW), tile(2 * D), tile(D), mod, mod, mod, full(gpm), full(gpf),
                  full(wa), full(wb), full(wo), full(wq)],
        out_specs=[tile(D), pl.BlockSpec((D, tm), lambda b, i: (0, b * (S // tm) + i)), tile(nq)],
        out_shape=[jax.ShapeDtypeStruct((B, S, D), F32),
                   jax.ShapeDtypeStruct((D, B * S), BF16),
                   jax.ShapeDtypeStruct((B, S, nq), F32)],
        compiler_params=_params(("arbitrary", "arbitrary")),
        name="outproj",
    )(ya, yb, gates, x, gt1, sc2, sh2, gpm, gpf, wa, wb, wo, wq)


def _cmpx(a, i, l, desc):
    hi, lo = jnp.maximum(a[i], a[l]), jnp.minimum(a[i], a[l])
    a[i], a[l] = (hi, lo) if desc else (lo, hi)


def _top16_desc(a):
    a = list(a)
    n = len(a)
    k = 2
    while k <= n:
        j = k // 2
        while j >= 1:
            for i in range(n):
                l = i ^ j
                if l > i:
                    _cmpx(a, i, l, (i & k) == 0)
            j //= 2
        k *= 2
    shift = SUBLANES // 2
    while shift >= 1:
        other = [pltpu.roll(v, shift, 0) for v in a]
        a = [jnp.maximum(a[i], other[n - 1 - i]) for i in range(n)]
        j = n // 2
        while j >= 1:
            for i in range(n):
                l = i ^ j
                if l > i:
                    _cmpx(a, i, l, True)
            j //= 2
        shift //= 2
    return a


def _topk_kernel(qy_ref, k1_ref, k2_ref, theta_ref, e1_ref, s2_ref, m2_ref):
    K = PEER_TOPK
    half = PEER_DQ // 2
    tt = qy_ref.shape[0]
    G = N_KEYS // SUBLANES
    sub = lax.broadcasted_iota(jnp.int32, (SUBLANES, LANES), 0)

    def stagger(d, base):
        out = d[base]
        for r in range(1, SUBLANES):
            out = jnp.where(sub == r, d[base + r], out)
        return out

    def head(h, carry):
        col = pl.multiple_of(h * PEER_DQ, PEER_DQ)
        q1 = qy_ref[:, pl.ds(col, half)]
        q2 = qy_ref[:, pl.ds(col + half, half)]
        s1 = _dot_nt(k1_ref[...], q1, HIGHEST)
        s2 = _dot_nt(k2_ref[...], q2, HIGHEST)
        for c in range(tt // LANES):
            cs = slice(c * LANES, (c + 1) * LANES)
            s2_ref[h, c] = s2[:, cs]
            s1g = [s1[SUBLANES * g:SUBLANES * (g + 1), cs] for g in range(G)]
            d1 = _top16_desc(s1g)
            d2 = _top16_desc([s2[SUBLANES * g:SUBLANES * (g + 1), cs] for g in range(G)])
            d2_lo, d2_hi, d1_hi = stagger(d2, 0), stagger(d2, SUBLANES), stagger(d1, SUBLANES)
            cand = [d1[0] + d2_lo, d1[0] + d2_hi, d1_hi + d2[0]] + [d1[i] + d2_lo for i in range(1, SUBLANES)]
            cand += [jnp.full((SUBLANES, LANES), -jnp.inf, F32)] * (K - len(cand))
            top = _top16_desc(cand)
            z = jnp.ones_like(top[0])
            for i in range(1, K):
                z = z + jnp.exp(top[i] - top[0])
            zinv = 1.0 / z
            tau = top[K - 1]
            m2_ref[h, :, cs] = d2[0][0:1, :]
            for g in range(G):
                theta = jnp.full((SUBLANES, LANES), jnp.inf, F32)
                for j in range(K):
                    theta = jnp.where(s1g[g] + d2[j] >= tau, d2[j], theta)
                theta_ref[h, c, SUBLANES * g:SUBLANES * (g + 1), :] = theta
                e1_ref[h, c, SUBLANES * g:SUBLANES * (g + 1), :] = jnp.exp(s1g[g] - d1[0]) * zinv
        return carry

    lax.fori_loop(0, PEER_HEADS, head, 0, unroll=4)


def _topk(qy, keys1, keys2, tt):
    T, nq = qy.shape
    H = PEER_HEADS
    full = lambda a: pl.BlockSpec(a.shape, lambda i: (0,) * a.ndim)
    big = pl.BlockSpec((H, tt // LANES, N_KEYS, LANES), lambda i: (0, i, 0, 0))
    return pl.pallas_call(
        _topk_kernel,
        grid=(T // tt,),
        in_specs=[pl.BlockSpec((tt, nq), lambda i: (i, 0)), full(keys1), full(keys2)],
        out_specs=[big, big, big, pl.BlockSpec((H, 1, tt), lambda i: (0, 0, i))],
        out_shape=[jax.ShapeDtypeStruct((H, T // LANES, N_KEYS, LANES), F32)] * 3
        + [jax.ShapeDtypeStruct((H, 1, T), F32)],
        compiler_params=_params(("arbitrary",)),
        name="topk",
    )(qy, keys1, keys2)


def _peer_kernel(h2t_ref, u_ref, vt_ref, theta_ref, e1_ref, s2_ref, m2_ref, x1_ref, gt2_ref, g_ref, o_ref,
                 acc_ref, e2_ref, st_ref, wt_ref, *, na):
    j = pl.program_id(1)
    H = PEER_HEADS
    tt = h2t_ref.shape[1]
    vreg = (SUBLANES, LANES)
    ncol = tt // LANES
    nkb = N_KEYS // GATE_KB
    blk = (GATE_KB // SUBLANES, SUBLANES, LANES)

    @pl.when(j == 0)
    def _():
        acc_ref[...] = jnp.zeros(acc_ref.shape, F32)
        for h in range(H):
            for c in range(ncol):
                e2_ref[h, c] = jnp.exp(s2_ref[h, c] - m2_ref[h, :, c * LANES:(c + 1) * LANES])

    st = _dot(u_ref[...], h2t_ref[...])
    st = 0.5 * st * (1.0 + lax.erf(st * (0.5 ** 0.5)))
    for c in range(ncol):
        st_ref[c] = st[:, c * LANES:(c + 1) * LANES].reshape(na, nkb, GATE_KB, LANES)

    def gate_blocks(a0, nab):
        def block(i, carry):
            c = lax.shift_right_logical(i, nkb.bit_length() - 1)
            kb = lax.bitwise_and(i, nkb - 1)
            was = [jnp.zeros(blk, F32) for _ in range(nab)]
            for h in range(H):
                s2 = s2_ref[h, c, kb].reshape(blk)
                e2 = e2_ref[h, c, kb].reshape(blk)
                for t in range(nab):
                    thb = jnp.broadcast_to(theta_ref[h, c, a0 + t:a0 + t + 1, :], vreg)
                    e1b = jnp.broadcast_to(e1_ref[h, c, a0 + t:a0 + t + 1, :], vreg)
                    was[t] = was[t] + jnp.where(s2 >= thb, e1b * e2, 0.0)
            for t in range(nab):
                act = st_ref[c, a0 + t, kb].reshape(blk)
                wt_ref[c, a0 + t, kb] = (was[t] * act).reshape(GATE_KB, LANES).astype(BF16)
            return carry

        lax.fori_loop(0, ncol * nkb, block, 0, unroll=4)

    for a0 in range(0, na, GATE_AB):
        gate_blocks(a0, GATE_AB)
    gated = jnp.concatenate([wt_ref[c].reshape(na * N_KEYS, LANES) for c in range(ncol)], axis=1)
    acc_ref[...] += _dot(vt_ref[...], gated)

    @pl.when(j == pl.num_programs(1) - 1)
    def _():
        y = acc_ref[...].T
        o_ref[...] = x1_ref[...] + gt2_ref[...] * _rms(y, g_ref[...])


def _peer(h2t, u_b, vt_b, theta, e1, s2, m2, x1, gt2, g_post, tt, ec, tiles_per_batch):
    D, T = h2t.shape
    E = u_b.shape[0]
    H = PEER_HEADS
    na = ec // N_KEYS
    ncol = tt // LANES
    nkb = N_KEYS // GATE_KB
    s2 = s2.reshape(H, T // LANES, nkb, GATE_KB, LANES)
    rows = pl.BlockSpec((H, ncol, na, LANES), lambda i, j: (0, i, j, 0))
    slab = (ncol, na, nkb, GATE_KB, LANES)
    return pl.pallas_call(
        functools.partial(_peer_kernel, na=na),
        grid=(T // tt, E // ec),
        in_specs=[pl.BlockSpec((D, tt), lambda i, j: (0, i)),
                  pl.BlockSpec((ec, D), lambda i, j: (j, 0)),
                  pl.BlockSpec((D, ec), lambda i, j: (0, j)),
                  rows, rows,
                  pl.BlockSpec((H, ncol, nkb, GATE_KB, LANES), lambda i, j: (0, i, 0, 0, 0)),
                  pl.BlockSpec((H, 1, tt), lambda i, j: (0, 0, i)),
                  pl.BlockSpec((tt, D), lambda i, j: (i, 0)),
                  pl.BlockSpec((None, 1, D), lambda i, j: (i // tiles_per_batch, 0, 0)),
                  pl.BlockSpec(g_post.shape, lambda i, j: (0, 0))],
        out_specs=pl.BlockSpec((tt, D), lambda i, j: (i, 0)),
        out_shape=jax.ShapeDtypeStruct((T, D), F32),
        scratch_shapes=[pltpu.VMEM((D, tt), F32), pltpu.VMEM((H, ncol, nkb, GATE_KB, LANES), F32),
                        pltpu.VMEM(slab, F32), pltpu.VMEM(slab, BF16)],
        compiler_params=_params(("arbitrary", "arbitrary")),
        name="peer",
    )(h2t, u_b, vt_b, theta, e1, s2, m2, x1, gt2, g_post)


def _pick(n, pref):
    t = pref
    while n % t:
        t //= 2
    return t


def _layer(x, c_pad, w_ada, b_ada, g_pre_mix, g_post_mix, w_in, b_fgate, fox_w_o, conv_w, a_log, dt_bias,
           gdn_norm_w, gdn_w_o, b_branch_gate, w_out, g_pre_ffn, g_post_ffn, peer_w_query, peer_keys1,
           peer_keys2, peer_u, peer_v):
    B, S, D = x.shape
    T = B * S
    mod = _ada(c_pad, w_ada, b_ada.reshape(1, -1))[:B]
    sh1, sc1, gt1, sh2, sc2, gt2 = [m.reshape(B, 1, D) for m in jnp.split(mod, 6, axis=-1)]

    off = [0]
    for s in SPLITS:
        off.append(off[-1] + s)
    col = lambda i, j: w_in[:, off[i]:off[j]]
    wf = col(0, 3).astype(BF16)
    wg = col(4, 8).astype(BF16)
    wgate = col(10, 12).astype(BF16)
    pad = LANES - (FOX_HEADS + 2 * GDN_HEADS)
    ws = jnp.concatenate([col(3, 4), col(8, 10), jnp.zeros((D, pad), F32)], axis=1).astype(BF16)
    zpad = jnp.zeros((pad,), F32)
    bsm = jnp.concatenate([b_fgate, jnp.zeros((GDN_HEADS,), F32), dt_bias, zpad]).reshape(1, LANES)
    alog_row = jnp.concatenate([jnp.zeros((FOX_HEADS + GDN_HEADS,), F32), a_log, zpad]).reshape(1, LANES)
    bgate = b_branch_gate.reshape(1, 2 * D)

    tm = _pick(S, 512)
    fqkv, gqkv, gz, gates, small = _inproj(x, sc1, sh1, g_pre_mix.reshape(1, D), wf, wg, wgate, ws, bgate,
                                           bsm, alog_row, tm)
    cum = _cumsum(small, _pick(S, 512))
    ya = _fox(fqkv, cum, _pick(S, 512))
    yb = _gdn(gqkv, gz, small, conv_w, gdn_norm_w.reshape(1, GDN_DV))
    x1, h2t, qy = _outproj(ya, yb, gates, x, gt1, sc2, sh2, g_post_mix.reshape(1, D), g_pre_ffn.reshape(1, D),
                          fox_w_o.astype(BF16), gdn_w_o.astype(BF16), w_out.astype(BF16),
                          peer_w_query.astype(BF16), tm)
    qy = qy.reshape(T, -1)
    theta, e1, s2, m2 = _topk(qy, peer_keys1, peer_keys2, _pick(T, 256))
    tt = _pick(S, 512)
    out = _peer(h2t, peer_u.astype(BF16), peer_v.T.astype(BF16), theta, e1, s2, m2,
                x1.reshape(T, D), gt2, g_post_ffn.reshape(1, D), tt, 2048, S // tt)
    return out.reshape(B, S, D)


def kernel(x, c, w_ada, b_ada, g_pre_mix, g_post_mix, w_in, b_fgate, fox_w_o, conv_w, a_log, dt_bias,
           gdn_norm_w, gdn_w_o, b_branch_gate, w_out, g_pre_ffn, g_post_ffn, peer_w_query, peer_keys1,
           peer_keys2, peer_u, peer_v):
    B = x.shape[0]
    c_pad = jnp.pad(c, ((0, (-B) % 8), (0, 0)))
    for l in range(w_ada.shape[0]):
        x = _layer(x, c_pad, w_ada[l], b_ada[l], g_pre_mix[l], g_post_mix[l], w_in[l], b_fgate[l],
                   fox_w_o[l], conv_w[l], a_log[l], dt_bias[l], gdn_norm_w[l], gdn_w_o[l], b_branch_gate[l],
                   w_out[l], g_pre_ffn[l], g_post_ffn[l], peer_w_query[l], peer_keys1[l], peer_keys2[l],
                   peer_u[l], peer_v[l])
    return x
```

```python
import functools

import jax
import jax.numpy as jnp
from jax import lax
from jax.experimental import pallas as pl
from jax.experimental.pallas import tpu as pltpu

F32 = jnp.float32
BF16 = jnp.bfloat16
HIGHEST = lax.Precision.HIGHEST

D_MODEL = 1024
EPS = 1e-6
FOX_HEADS = 8
FOX_DH = 64
GDN_HEADS = 4
GDN_DK = 128
GDN_DV = 128
CONV_W = 4
PEER_HEADS = 8
N_KEYS = 128
PEER_DQ = 256
PEER_TOPK = 16
FOX_W = FOX_HEADS * FOX_DH
GDN_KW = GDN_HEADS * GDN_DK
GDN_VW = GDN_HEADS * GDN_DV
SPLITS = (FOX_W, FOX_W, FOX_W, FOX_HEADS, GDN_KW, GDN_KW, GDN_VW, GDN_VW, GDN_HEADS, GDN_HEADS,
          D_MODEL, D_MODEL)

LANES = 128
SUBLANES = 8
GDN_CHUNK = 128
FOX_TILES_PER_ITER = 4
GATE_KB = 16
GATE_AB = 8
NEG_BIG = -1e30
LOG2E = 1.4426950408889634
VMEM_LIMIT = 56 * 1024 * 1024


def _params(sem):
    return pltpu.CompilerParams(dimension_semantics=sem, vmem_limit_bytes=VMEM_LIMIT)


def _dot(a, b, precision=None):
    return jnp.dot(a, b, preferred_element_type=F32, precision=precision)


def _dot_nt(a, b, precision=None):
    return lax.dot_general(a, b, (((1,), (1,)), ((), ())), preferred_element_type=F32,
                           precision=precision)


def _sigmoid(x):
    return 1.0 / (1.0 + jnp.exp(-x))


def _silu(x):
    return x * _sigmoid(x)


def _softplus(x):
    return jnp.maximum(x, 0.0) + jnp.log(1.0 + jnp.exp(-jnp.abs(x)))


def _log_sigmoid(x):
    return -_softplus(-x)


def _rms(x, g):
    return x * lax.rsqrt(jnp.mean(x * x, axis=-1, keepdims=True) + EPS) * g


def _ada_kernel(c_ref, w_ref, b_ref, o_ref):
    c = c_ref[...]
    o_ref[...] = _dot(_silu(c), w_ref[...], HIGHEST) + b_ref[...]


def _ada(c_pad, w, b):
    n = w.shape[1]
    bn = 1024
    return pl.pallas_call(
        _ada_kernel,
        grid=(n // bn,),
        in_specs=[pl.BlockSpec(c_pad.shape, lambda j: (0, 0)),
                  pl.BlockSpec((D_MODEL, bn), lambda j: (0, j)),
                  pl.BlockSpec((1, bn), lambda j: (0, j))],
        out_specs=pl.BlockSpec((c_pad.shape[0], bn), lambda j: (0, j)),
        out_shape=jax.ShapeDtypeStruct((c_pad.shape[0], n), F32),
        compiler_params=_params(("arbitrary",)),
        name="ada",
    )(c_pad, w, b)


def _inproj_kernel(x_ref, sc_ref, sh_ref, g_ref, wf_ref, wg_ref, wgate_ref, ws_ref, bgate_ref,
                   bsm_ref, alog_ref, fqkv_ref, gqkv_ref, gz_ref, gates_ref, small_ref):
    x = x_ref[...]
    h = _rms(x, g_ref[...]) * (1.0 + sc_ref[...]) + sh_ref[...]
    hb = h.astype(BF16)
    pf = _dot(hb, wf_ref[...])
    lane = lax.broadcasted_iota(jnp.int32, (1, 3 * FOX_W), 1)
    pf = jnp.where(lane < FOX_W, pf * (FOX_DH ** -0.5 * LOG2E), pf)
    fqkv_ref[...] = pf.astype(BF16)
    pg = _dot(hb, wg_ref[...])
    gqkv_ref[...] = pg[:, :3 * GDN_KW]
    gz_ref[...] = pg[:, 3 * GDN_KW:]
    gates_ref[...] = _sigmoid(_dot(hb, wgate_ref[...]) + bgate_ref[...])
    z = _dot(hb, ws_ref[...]) + bsm_ref[...]
    sl = lax.broadcasted_iota(jnp.int32, (1, LANES), 1)
    neg_a = -jnp.exp(alog_ref[...])
    small_ref[...] = jnp.where(sl < FOX_HEADS, _log_sigmoid(z),
                               jnp.where(sl < FOX_HEADS + GDN_HEADS, _sigmoid(z), neg_a * _softplus(z)))


def _inproj(x, sc1, sh1, g_pre, wf, wg, wgate, ws, bgate, bsm, alog_row, tm):
    B, S, D = x.shape
    full = lambda a: pl.BlockSpec(a.shape, lambda b, i: (0,) * a.ndim)
    mod = pl.BlockSpec((None, 1, D), lambda b, i: (b, 0, 0))
    tile = lambda n: pl.BlockSpec((None, tm, n), lambda b, i: (b, i, 0))
    return pl.pallas_call(
        _inproj_kernel,
        grid=(B, S // tm),
        in_specs=[tile(D), mod, mod, full(g_pre), full(wf), full(wg), full(wgate), full(ws),
                  full(bgate), full(bsm), full(alog_row)],
        out_specs=[tile(3 * FOX_W), tile(3 * GDN_KW), tile(GDN_VW), tile(2 * D), tile(LANES)],
        out_shape=[jax.ShapeDtypeStruct((B, S, 3 * FOX_W), BF16),
                   jax.ShapeDtypeStruct((B, S, 3 * GDN_KW), F32),
                   jax.ShapeDtypeStruct((B, S, GDN_VW), F32),
                   jax.ShapeDtypeStruct((B, S, 2 * D), F32),
                   jax.ShapeDtypeStruct((B, S, LANES), F32)],
        compiler_params=_params(("arbitrary", "arbitrary")),
        name="inproj",
    )(x, sc1, sh1, g_pre, wf, wg, wgate, ws, bgate, bsm, alog_row)


def _cumsum_kernel(s_ref, o_ref, carry_ref):
    @pl.when(pl.program_id(1) == 0)
    def _():
        carry_ref[...] = jnp.zeros_like(carry_ref)

    v = s_ref[...]
    n = v.shape[0]
    r = lax.broadcasted_iota(jnp.int32, (n, n), 0)
    c = lax.broadcasted_iota(jnp.int32, (n, n), 1)
    tri = jnp.where(r >= c, 1.0, 0.0).astype(F32)
    cs = _dot(tri, v, HIGHEST) + carry_ref[...]
    o_ref[...] = cs
    carry_ref[...] = cs[n - 1:n, :]


def _cumsum(small, tc):
    B, S, L = small.shape
    return pl.pallas_call(
        _cumsum_kernel,
        grid=(B, S // tc),
        in_specs=[pl.BlockSpec((None, tc, L), lambda b, i: (b, i, 0))],
        out_specs=pl.BlockSpec((None, tc, L), lambda b, i: (b, i, 0)),
        out_shape=jax.ShapeDtypeStruct((B, S, L), F32),
        scratch_shapes=[pltpu.VMEM((1, L), F32)],
        compiler_params=_params(("arbitrary", "arbitrary")),
        name="cumsum",
    )(small)


def _split3(x):
    p1 = x.astype(BF16).astype(F32)
    r = x - p1
    p2 = r.astype(BF16).astype(F32)
    p3 = (r - p2).astype(BF16).astype(F32)
    return p1, p2, p3


def _fox_kernel(q_ref, k_ref, v_ref, cum_ref, o_ref, kmod_ref, vmod_ref, m_ref, acc_ref, *, tq):
    hp = pl.program_id(1)
    i = pl.program_id(2)
    S = k_ref.shape[0]
    lane = lax.broadcasted_iota(jnp.int32, (1, LANES), 1)

    def head_col(tile, h):
        return jnp.sum(jnp.where(lane == h, tile, 0.0), axis=1, keepdims=True)

    def with_bias(base, hh, pieces, ones_first):
        sp = (1 - hh) * FOX_DH
        in_head = (lane >= hh * FOX_DH) & (lane < (hh + 1) * FOX_DH)
        out = jnp.where(in_head, base, 0.0)
        po, oo = (3, 0) if ones_first else (0, 3)
        for t in range(3):
            out = jnp.where(lane == sp + po + t, pieces[t], out)
            out = jnp.where(lane == sp + oo + t, 1.0, out)
        return out.astype(BF16)

    @pl.when(i == 0)
    def _():
        def chunk(ci, carry):
            start = pl.multiple_of(ci * tq, tq)
            k2 = k_ref[pl.ds(start, tq), :].astype(F32)
            v2 = v_ref[pl.ds(start, tq), :].astype(F32)
            cum = cum_ref[pl.ds(start, tq), :]
            for hh in range(2):
                ck = head_col(cum, hp * 2 + hh)
                kmod_ref[hh, pl.ds(start, tq), :] = with_bias(k2, hh, _split3(-ck * LOG2E), True)
                in_head = (lane >= hh * FOX_DH) & (lane < (hh + 1) * FOX_DH)
                vm = jnp.where(in_head, v2, jnp.where(lane == (1 - hh) * FOX_DH, 1.0, 0.0))
                vmod_ref[hh, pl.ds(start, tq), :] = vm.astype(BF16)
            return carry
        lax.fori_loop(0, S // tq, chunk, 0)

    q2 = q_ref[...].astype(F32)
    cumq = cum_ref[pl.ds(pl.multiple_of(i * tq, tq), tq), :]
    qmods = [with_bias(q2, hh, _split3(head_col(cumq, hp * 2 + hh) * LOG2E), False) for hh in range(2)]
    rr = lax.broadcasted_iota(jnp.int32, (tq, tq), 0)
    cc = lax.broadcasted_iota(jnp.int32, (tq, tq), 1)
    m_ref[...] = jnp.full(m_ref.shape, NEG_BIG, F32)
    acc_ref[...] = jnp.zeros(acc_ref.shape, F32)

    def step(j, masked):
        start = pl.multiple_of(j * tq, tq)
        scores = [_dot_nt(qmods[hh], kmod_ref[hh, pl.ds(start, tq), :]) for hh in range(2)]
        for hh in range(2):
            s = scores[hh]
            if masked:
                s = jnp.where(rr >= cc, s, NEG_BIG)
            m_prev = m_ref[hh]
            m_new = jnp.maximum(m_prev, jnp.max(s, axis=1, keepdims=True))
            p = jnp.exp2(s - jnp.concatenate([m_new] * (tq // LANES), axis=1)).astype(BF16)
            acc_ref[hh] = jnp.exp2(m_prev - m_new) * acc_ref[hh] + _dot(p, vmod_ref[hh, pl.ds(start, tq), :])
            m_ref[hh] = m_new

    UN = FOX_TILES_PER_ITER

    def body(jj, carry):
        for t in range(UN):
            step(UN * jj + t, False)
        return carry

    def single(j, carry):
        step(j, False)
        return carry

    nfull = i // UN
    lax.fori_loop(0, nfull, body, 0)
    lax.fori_loop(nfull * UN, i, single, 0)
    step(i, True)
    outs = []
    for hh in range(2):
        acc = acc_ref[hh]
        outs.append(acc / head_col(acc, (1 - hh) * FOX_DH))
    o_ref[...] = jnp.where(lane < FOX_DH, outs[0], outs[1]).astype(o_ref.dtype)


def _fox(fqkv, cum, tq):
    B, S, _ = fqkv.shape
    nhp = FOX_HEADS // 2
    return pl.pallas_call(
        functools.partial(_fox_kernel, tq=tq),
        grid=(B, nhp, S // tq),
        in_specs=[pl.BlockSpec((None, tq, LANES), lambda b, hp, i: (b, i, hp)),
                  pl.BlockSpec((None, S, LANES), lambda b, hp, i: (b, 0, nhp + hp)),
                  pl.BlockSpec((None, S, LANES), lambda b, hp, i: (b, 0, 2 * nhp + hp)),
                  pl.BlockSpec((None, S, LANES), lambda b, hp, i: (b, 0, 0))],
        out_specs=pl.BlockSpec((None, tq, LANES), lambda b, hp, i: (b, i, hp)),
        out_shape=jax.ShapeDtypeStruct((B, S, FOX_W), BF16),
        scratch_shapes=[pltpu.VMEM((2, S, LANES), BF16), pltpu.VMEM((2, S, LANES), BF16),
                        pltpu.VMEM((2, tq, LANES), F32), pltpu.VMEM((2, tq, LANES), F32)],
        compiler_params=_params(("arbitrary", "arbitrary", "arbitrary")),
        name="fox",
    )(fqkv, fqkv, fqkv, cum)


def _level_masks(n):
    r = lax.broadcasted_iota(jnp.int32, (n, n), 0)
    c = lax.broadcasted_iota(jnp.int32, (n, n), 1)
    masks = []
    s = 1
    while s < n:
        sel = ((r // (2 * s)) == (c // (2 * s))) & (((r // s) % 2) == 1) & (((c // s) % 2) == 0)
        masks.append(jnp.where(sel, 1.0, 0.0).astype(F32))
        s *= 2
    return masks


def _unit_lower_inverses(lows, masks):
    n = lows[0].shape[0]
    r = lax.broadcasted_iota(jnp.int32, (n, n), 0)
    c = lax.broadcasted_iota(jnp.int32, (n, n), 1)
    xs = [jnp.where(r == c, 1.0, 0.0).astype(F32)] * len(lows)
    for mask in masks:
        xbs = [x.astype(BF16) for x in xs]
        xcs = [_dot(xb, (low * mask).astype(BF16)) for xb, low in zip(xbs, lows)]
        xs = [x - _dot(xc.astype(BF16), xb) for x, xc, xb in zip(xs, xcs, xbs)]
    return xs


def _gdn_kernel(x_ref, gz_ref, sm_ref, cw_ref, nw_ref, o_ref, state_ref, carry_ref):
    ci = pl.program_id(0)
    NB = x_ref.shape[0]
    b0 = pl.program_id(1) * NB
    C = GDN_CHUNK

    @pl.when(ci == 0)
    def _():
        for bb in range(NB):
            state_ref[b0 + bb] = jnp.zeros(state_ref.shape[1:], F32)
            carry_ref[b0 + bb] = jnp.zeros(carry_ref.shape[1:], F32)

    row8 = lax.broadcasted_iota(jnp.int32, (SUBLANES, 1), 0)
    r = lax.broadcasted_iota(jnp.int32, (C, C), 0)
    c = lax.broadcasted_iota(jnp.int32, (C, C), 1)
    causal = r >= c
    strict = r > c
    tri = jnp.where(causal, 1.0, 0.0).astype(F32)
    nw = nw_ref[...]
    masks = _level_masks(C)

    ys, sms, gcums, gcum_ts = [], [], [], []
    for bb in range(NB):
        x = x_ref[bb]
        prev = carry_ref[b0 + bb]
        y = cw_ref[CONV_W - 1:CONV_W, :] * x
        for j in range(1, CONV_W):
            xr = pltpu.roll(x, j, 0)
            pr = pltpu.roll(prev, j, 0)
            first = jnp.where(row8 < j, pr, xr[0:SUBLANES, :])
            y = y + cw_ref[CONV_W - 1 - j:CONV_W - j, :] * jnp.concatenate([first, xr[SUBLANES:, :]], axis=0)
        carry_ref[b0 + bb] = x[C - SUBLANES:C, :]
        ys.append(_silu(y))
        sms.append(sm_ref[bb])
        gcums.append(_dot(tri, sms[bb], HIGHEST))
        gcum_ts.append(gcums[bb].T)

    streams = [(bb, h) for bb in range(NB) for h in range(GDN_HEADS)]
    idx = range(len(streams))
    qs, ks, vs, betas, g_cols, g_lasts, decays = [], [], [], [], [], [], []
    for bb, h in streams:
        y, sm, gcum = ys[bb], sms[bb], gcums[bb]
        q = y[:, h * GDN_DK:(h + 1) * GDN_DK]
        k = y[:, GDN_KW + h * GDN_DK:GDN_KW + (h + 1) * GDN_DK]
        qs.append(q * lax.rsqrt(jnp.sum(q * q, axis=1, keepdims=True) + EPS) * (GDN_DK ** -0.5))
        ks.append(k * lax.rsqrt(jnp.sum(k * k, axis=1, keepdims=True) + EPS))
        vs.append(y[:, 2 * GDN_KW + h * GDN_DV:2 * GDN_KW + (h + 1) * GDN_DV])
        betas.append(sm[:, FOX_HEADS + h:FOX_HEADS + h + 1])
        gl = FOX_HEADS + GDN_HEADS + h
        g_cols.append(gcum[:, gl:gl + 1])
        g_lasts.append(gcum[C - 1:C, gl:gl + 1])
        diff = g_cols[-1] - gcum_ts[bb][gl:gl + 1, :]
        decays.append(jnp.where(causal, jnp.exp(jnp.where(causal, diff, 0.0)), 0.0))
    kbs = [ks[i] * betas[i] for i in idx]
    egs = [jnp.exp(g_cols[i]) for i in idx]
    kqs = [_dot_nt(jnp.concatenate([kbs[i], qs[i]], axis=0).astype(BF16), ks[i].astype(BF16)) for i in idx]
    lows = [jnp.where(strict, kqs[i][:C] * decays[i], 0.0) for i in idx]
    attns = [(kqs[i][C:] * decays[i]).astype(BF16) for i in idx]
    tinvs = _unit_lower_inverses(lows, masks)
    uws = [_dot(tinvs[i].astype(BF16),
                jnp.concatenate([vs[i] * betas[i], kbs[i] * egs[i]], axis=1).astype(BF16)) for i in idx]
    sts = [state_ref[b0 + bb, h] for bb, h in streams]
    wqs = [_dot(jnp.concatenate([uws[i][:, GDN_DV:], qs[i] * egs[i]], axis=0).astype(BF16),
                sts[i].astype(BF16)) for i in idx]
    v_news = [(uws[i][:, :GDN_DV] - wqs[i][:C]).astype(BF16) for i in idx]
    k_dec_ts = [(ks[i] * jnp.exp(g_lasts[i] - g_cols[i])).T.astype(BF16) for i in idx]
    os_ = [wqs[i][C:] + _dot(attns[i], v_news[i]) for i in idx]
    new_sts = [sts[i] * jnp.exp(g_lasts[i]) + _dot(k_dec_ts[i], v_news[i]) for i in idx]
    for i, (bb, h) in enumerate(streams):
        state_ref[b0 + bb, h] = new_sts[i]
        o = _rms(os_[i], nw) * _silu(gz_ref[bb, :, h * GDN_DV:(h + 1) * GDN_DV])
        o_ref[bb, :, h * GDN_DV:(h + 1) * GDN_DV] = o.astype(o_ref.dtype)


def _gdn(gqkv, gz, small, conv_w, norm_w):
    B, S, _ = gqkv.shape
    C = GDN_CHUNK
    nb = _pick(B, 4)
    tile = lambda n: pl.BlockSpec((nb, C, n), lambda ci, b: (b, ci, 0))
    full = lambda a: pl.BlockSpec(a.shape, lambda ci, b: (0,) * a.ndim)
    return pl.pallas_call(
        _gdn_kernel,
        grid=(S // C, B // nb),
        in_specs=[tile(3 * GDN_KW), tile(GDN_VW), tile(LANES), full(conv_w), full(norm_w)],
        out_specs=tile(GDN_VW),
        out_shape=jax.ShapeDtypeStruct((B, S, GDN_VW), BF16),
        scratch_shapes=[pltpu.VMEM((B, GDN_HEADS, GDN_DK, GDN_DV), F32),
                        pltpu.VMEM((B, SUBLANES, 3 * GDN_KW), F32)],
        compiler_params=_params(("arbitrary", "arbitrary")),
        name="gdn",
    )(gqkv, gz, small, conv_w, norm_w)


def _outproj_kernel(ya_ref, yb_ref, gates_ref, x_ref, gt1_ref, sc2_ref, sh2_ref, gpm_ref, gpf_ref,
                    wa_ref, wb_ref, wo_ref, wq_ref, x1_ref, h2t_ref, qy_ref):
    ya = _dot(ya_ref[...], wa_ref[...])
    yb = _dot(yb_ref[...], wb_ref[...])
    gates = gates_ref[...]
    mix_in = gates[:, :D_MODEL] * ya + gates[:, D_MODEL:] * yb
    mix = _dot(mix_in.astype(BF16), wo_ref[...])
    x1 = x_ref[...] + gt1_ref[...] * _rms(mix, gpm_ref[...])
    x1_ref[...] = x1
    h2 = _rms(x1, gpf_ref[...]) * (1.0 + sc2_ref[...]) + sh2_ref[...]
    h2t_ref[...] = h2.T.astype(BF16)
    qy_ref[...] = _dot(h2.astype(BF16), wq_ref[...])


def _outproj(ya, yb, gates, x, gt1, sc2, sh2, gpm, gpf, wa, wb, wo, wq, tm):
    B, S, D = x.shape
    full = lambda a: pl.BlockSpec(a.shape, lambda b, i: (0,) * a.ndim)
    mod = pl.BlockSpec((None, 1, D), lambda b, i: (b, 0, 0))
    tile = lambda n: pl.BlockSpec((None, tm, n), lambda b, i: (b, i, 0))
    nq = wq.shape[1]
    return pl.pallas_call(
        _outproj_kernel,
        grid=(B, S // tm),
        in_specs=[tile(FOX_W), tile(GDN_VW), tile(2 * D), tile(D), mod, mod, mod, full(gpm), full(gpf),
                  full(wa), full(wb), full(wo), full(wq)],
        out_specs=[tile(D), pl.BlockSpec((D, tm), lambda b, i: (0, b * (S // tm) + i)), tile(nq)],
        out_shape=[jax.ShapeDtypeStruct((B, S, D), F32),
                   jax.ShapeDtypeStruct((D, B * S), BF16),
                   jax.ShapeDtypeStruct((B, S, nq), F32)],
        compiler_params=_params(("arbitrary", "arbitrary")),
        name="outproj",
    )(ya, yb, gates, x, gt1, sc2, sh2, gpm, gpf, wa, wb, wo, wq)


def _cmpx(a, i, l, desc):
    hi, lo = jnp.maximum(a[i], a[l]), jnp.minimum(a[i], a[l])
    a[i], a[l] = (hi, lo) if desc else (lo, hi)


def _top16_desc(a):
    a = list(a)
    n = len(a)
    k = 2
    while k <= n:
        j = k // 2
        while j >= 1:
            for i in range(n):
                l = i ^ j
                if l > i:
                    _cmpx(a, i, l, (i & k) == 0)
            j //= 2
        k *= 2
    shift = SUBLANES // 2
    while shift >= 1:
        other = [pltpu.roll(v, shift, 0) for v in a]
        a = [jnp.maximum(a[i], other[n - 1 - i]) for i in range(n)]
        j = n // 2
        while j >= 1:
            for i in range(n):
                l = i ^ j
                if l > i:
                    _cmpx(a, i, l, True)
            j //= 2
        shift //= 2
    return a


def _topk_kernel(qy_ref, k1_ref, k2_ref, theta_ref, e1_ref, s2_ref, m2_ref):
    K = PEER_TOPK
    half = PEER_DQ // 2
    tt = qy_ref.shape[0]
    G = N_KEYS // SUBLANES
    sub = lax.broadcasted_iota(jnp.int32, (SUBLANES, LANES), 0)

    def stagger(d, base):
        out = d[base]
        for r in range(1, SUBLANES):
            out = jnp.where(sub == r, d[base + r], out)
        return out

    def head(h, carry):
        col = pl.multiple_of(h * PEER_DQ, PEER_DQ)
        q1 = qy_ref[:, pl.ds(col, half)]
        q2 = qy_ref[:, pl.ds(col + half, half)]
        s1 = _dot_nt(k1_ref[...], q1, HIGHEST)
        s2 = _dot_nt(k2_ref[...], q2, HIGHEST)
        for c in range(tt // LANES):
            cs = slice(c * LANES, (c + 1) * LANES)
            s2_ref[h, c] = s2[:, cs]
            s1g = [s1[SUBLANES * g:SUBLANES * (g + 1), cs] for g in range(G)]
            d1 = _top16_desc(s1g)
            d2 = _top16_desc([s2[SUBLANES * g:SUBLANES * (g + 1), cs] for g in range(G)])
            d2_lo, d2_hi, d1_hi = stagger(d2, 0), stagger(d2, SUBLANES), stagger(d1, SUBLANES)
            cand = [d1[0] + d2_lo, d1[0] + d2_hi, d1_hi + d2[0]] + [d1[i] + d2_lo for i in range(1, SUBLANES)]
            cand += [jnp.full((SUBLANES, LANES), -jnp.inf, F32)] * (K - len(cand))
            top = _top16_desc(cand)
            z = jnp.ones_like(top[0])
            for i in range(1, K):
                z = z + jnp.exp(top[i] - top[0])
            zinv = 1.0 / z
            tau = top[K - 1]
            m2_ref[h, :, cs] = d2[0][0:1, :]
            for g in range(G):
                theta = jnp.full((SUBLANES, LANES), jnp.inf, F32)
                for j in range(K):
                    theta = jnp.where(s1g[g] + d2[j] >= tau, d2[j], theta)
                theta_ref[h, c, SUBLANES * g:SUBLANES * (g + 1), :] = theta
                e1_ref[h, c, SUBLANES * g:SUBLANES * (g + 1), :] = jnp.exp(s1g[g] - d1[0]) * zinv
        return carry

    lax.fori_loop(0, PEER_HEADS, head, 0, unroll=4)


def _topk(qy, keys1, keys2, tt):
    T, nq = qy.shape
    H = PEER_HEADS
    full = lambda a: pl.BlockSpec(a.shape, lambda i: (0,) * a.ndim)
    big = pl.BlockSpec((H, tt // LANES, N_KEYS, LANES), lambda i: (0, i, 0, 0))
    return pl.pallas_call(
        _topk_kernel,
        grid=(T // tt,),
        in_specs=[pl.BlockSpec((tt, nq), lambda i: (i, 0)), full(keys1), full(keys2)],
        out_specs=[big, big, big, pl.BlockSpec((H, 1, tt), lambda i: (0, 0, i))],
        out_shape=[jax.ShapeDtypeStruct((H, T // LANES, N_KEYS, LANES), F32)] * 3
        + [jax.ShapeDtypeStruct((H, 1, T), F32)],
        compiler_params=_params(("arbitrary",)),
        name="topk",
    )(qy, keys1, keys2)


def _peer_kernel(h2t_ref, u_ref, vt_ref, theta_ref, e1_ref, s2_ref, m2_ref, x1_ref, gt2_ref, g_ref, o_ref,
                 acc_ref, e2_ref, st_ref, wt_ref, *, na):
    j = pl.program_id(1)
    H = PEER_HEADS
    tt = h2t_ref.shape[1]
    vreg = (SUBLANES, LANES)
    ncol = tt // LANES
    nkb = N_KEYS // GATE_KB
    blk = (GATE_KB // SUBLANES, SUBLANES, LANES)

    @pl.when(j == 0)
    def _():
        acc_ref[...] = jnp.zeros(acc_ref.shape, F32)
        for h in range(H):
            for c in range(ncol):
                e2_ref[h, c] = jnp.exp(s2_ref[h, c] - m2_ref[h, :, c * LANES:(c + 1) * LANES])

    st = _dot(u_ref[...], h2t_ref[...])
    st = 0.5 * st * (1.0 + lax.erf(st * (0.5 ** 0.5)))
    for c in range(ncol):
        st_ref[c] = st[:, c * LANES:(c + 1) * LANES].reshape(na, nkb, GATE_KB, LANES)

    def gate_blocks(a0, nab):
        def block(i, carry):
            c = lax.shift_right_logical(i, nkb.bit_length() - 1)
            kb = lax.bitwise_and(i, nkb - 1)
            was = [jnp.zeros(blk, F32) for _ in range(nab)]
            for h in range(H):
                s2 = s2_ref[h, c, kb].reshape(blk)
                e2 = e2_ref[h, c, kb].reshape(blk)
                for t in range(nab):
                    thb = jnp.broadcast_to(theta_ref[h, c, a0 + t:a0 + t + 1, :], vreg)
                    e1b = jnp.broadcast_to(e1_ref[h, c, a0 + t:a0 + t + 1, :], vreg)
                    was[t] = was[t] + jnp.where(s2 >= thb, e1b * e2, 0.0)
            for t in range(nab):
                act = st_ref[c, a0 + t, kb].reshape(blk)
                wt_ref[c, a0 + t, kb] = (was[t] * act).reshape(GATE_KB, LANES).astype(BF16)
            return carry

        lax.fori_loop(0, ncol * nkb, block, 0, unroll=8)

    for a0 in range(0, na, GATE_AB):
        gate_blocks(a0, GATE_AB)
    gated = jnp.concatenate([wt_ref[c].reshape(na * N_KEYS, LANES) for c in range(ncol)], axis=1)
    acc_ref[...] += _dot(vt_ref[...], gated)

    @pl.when(j == pl.num_programs(1) - 1)
    def _():
        y = acc_ref[...].T
        o_ref[...] = x1_ref[...] + gt2_ref[...] * _rms(y, g_ref[...])


def _peer(h2t, u_b, vt_b, theta, e1, s2, m2, x1, gt2, g_post, tt, ec, tiles_per_batch):
    D, T = h2t.shape
    E = u_b.shape[0]
    H = PEER_HEADS
    na = ec // N_KEYS
    ncol = tt // LANES
    nkb = N_KEYS // GATE_KB
    s2 = s2.reshape(H, T // LANES, nkb, GATE_KB, LANES)
    rows = pl.BlockSpec((H, ncol, na, LANES), lambda i, j: (0, i, j, 0))
    slab = (ncol, na, nkb, GATE_KB, LANES)
    return pl.pallas_call(
        functools.partial(_peer_kernel, na=na),
        grid=(T // tt, E // ec),
        in_specs=[pl.BlockSpec((D, tt), lambda i, j: (0, i)),
                  pl.BlockSpec((ec, D), lambda i, j: (j, 0)),
                  pl.BlockSpec((D, ec), lambda i, j: (0, j)),
                  rows, rows,
                  pl.BlockSpec((H, ncol, nkb, GATE_KB, LANES), lambda i, j: (0, i, 0, 0, 0)),
                  pl.BlockSpec((H, 1, tt), lambda i, j: (0, 0, i)),
                  pl.BlockSpec((tt, D), lambda i, j: (i, 0)),
                  pl.BlockSpec((None, 1, D), lambda i, j: (i // tiles_per_batch, 0, 0)),
                  pl.BlockSpec(g_post.shape, lambda i, j: (0, 0))],
        out_specs=pl.BlockSpec((tt, D), lambda i, j: (i, 0)),
        out_shape=jax.ShapeDtypeStruct((T, D), F32),
        scratch_shapes=[pltpu.VMEM((D, tt), F32), pltpu.VMEM((H, ncol, nkb, GATE_KB, LANES), F32),
                        pltpu.VMEM(slab, F32), pltpu.VMEM(slab, BF16)],
        compiler_params=_params(("arbitrary", "arbitrary")),
        name="peer",
    )(h2t, u_b, vt_b, theta, e1, s2, m2, x1, gt2, g_post)


def _pick(n, pref):
    t = pref
    while n % t:
        t //= 2
    return t


def _layer(x, c_pad, w_ada, b_ada, g_pre_mix, g_post_mix, w_in, b_fgate, fox_w_o, conv_w, a_log, dt_bias,
           gdn_norm_w, gdn_w_o, b_branch_gate, w_out, g_pre_ffn, g_post_ffn, peer_w_query, peer_keys1,
           peer_keys2, peer_u, peer_v):
    B, S, D = x.shape
    T = B * S
    mod = _ada(c_pad, w_ada, b_ada.reshape(1, -1))[:B]
    sh1, sc1, gt1, sh2, sc2, gt2 = [m.reshape(B, 1, D) for m in jnp.split(mod, 6, axis=-1)]

    off = [0]
    for s in SPLITS:
        off.append(off[-1] + s)
    col = lambda i, j: w_in[:, off[i]:off[j]]
    wf = col(0, 3).astype(BF16)
    wg = col(4, 8).astype(BF16)
    wgate = col(10, 12).astype(BF16)
    pad = LANES - (FOX_HEADS + 2 * GDN_HEADS)
    ws = jnp.concatenate([col(3, 4), col(8, 10), jnp.zeros((D, pad), F32)], axis=1).astype(BF16)
    zpad = jnp.zeros((pad,), F32)
    bsm = jnp.concatenate([b_fgate, jnp.zeros((GDN_HEADS,), F32), dt_bias, zpad]).reshape(1, LANES)
    alog_row = jnp.concatenate([jnp.zeros((FOX_HEADS + GDN_HEADS,), F32), a_log, zpad]).reshape(1, LANES)
    bgate = b_branch_gate.reshape(1, 2 * D)

    tm = _pick(S, 512)
    fqkv, gqkv, gz, gates, small = _inproj(x, sc1, sh1, g_pre_mix.reshape(1, D), wf, wg, wgate, ws, bgate,
                                           bsm, alog_row, tm)
    cum = _cumsum(small, _pick(S, 512))
    ya = _fox(fqkv, cum, _pick(S, 512))
    yb = _gdn(gqkv, gz, small, conv_w, gdn_norm_w.reshape(1, GDN_DV))
    x1, h2t, qy = _outproj(ya, yb, gates, x, gt1, sc2, sh2, g_post_mix.reshape(1, D), g_pre_ffn.reshape(1, D),
                          fox_w_o.astype(BF16), gdn_w_o.astype(BF16), w_out.astype(BF16),
                          peer_w_query.astype(BF16), tm)
    qy = qy.reshape(T, -1)
    theta, e1, s2, m2 = _topk(qy, peer_keys1, peer_keys2, _pick(T, 512))
    tt = _pick(S, 512)
    out = _peer(h2t, peer_u.astype(BF16), peer_v.T.astype(BF16), theta, e1, s2, m2,
                x1.reshape(T, D), gt2, g_post_ffn.reshape(1, D), tt, 2048, S // tt)
    return out.reshape(B, S, D)


def kernel(x, c, w_ada, b_ada, g_pre_mix, g_post_mix, w_in, b_fgate, fox_w_o, conv_w, a_log, dt_bias,
           gdn_norm_w, gdn_w_o, b_branch_gate, w_out, g_pre_ffn, g_post_ffn, peer_w_query, peer_keys1,
           peer_keys2, peer_u, peer_v):
    B = x.shape[0]
    c_pad = jnp.pad(c, ((0, (-B) % 8), (0, 0)))
    for l in range(w_ada.shape[0]):
        x = _layer(x, c_pad, w_ada[l], b_ada[l], g_pre_mix[l], g_post_mix[l], w_in[l], b_fgate[l],
                   fox_w_o[l], conv_w[l], a_log[l], dt_bias[l], gdn_norm_w[l], gdn_w_o[l], b_branch_gate[l],
                   w_out[l], g_pre_ffn[l], g_post_ffn[l], peer_w_query[l], peer_keys1[l], peer_keys2[l],
                   peer_u[l], peer_v[l])
    return x
```
